```python
import jax, jax.numpy as jnp
from jax import lax
import numpy as np

D_MODEL = 1024
BATCH = 2
SEQ = 8192
DEPTH = 4

N_MIXERS = 2
N_POOL_LAYERS = (DEPTH + 1) // 2
N_SB_LAYERS = DEPTH // 2
POOL_WINDOWS = (2, 4, 8, 16)
N_POOL_GROUPS = len(POOL_WINDOWS)
POOL_GROUP_DIM = D_MODEL // N_POOL_GROUPS
SB_HEADS = 16
SB_HEAD_DIM = D_MODEL // SB_HEADS
SB_BLOCK = 128
N_GROUPS = 4
EXPERTS_PER_GROUP = 8
N_EXPERTS = N_GROUPS * EXPERTS_PER_GROUP
TOP_K_IN_GROUP = 2
D_EXPERT = D_MODEL // 4
RMS_EPS = 1e-6

kernel_name = 'hybrid_pool_stickbreak_hmoe'


def rms_norm(x, gain):
    xf = x.astype(jnp.float32)
    y = xf * lax.rsqrt(jnp.mean(xf * xf, axis=-1, keepdims=True) + RMS_EPS)
    return (y * gain.astype(jnp.float32)).astype(x.dtype)


def pool_mixer(h, w, b, scale):
    B_, S_, D_ = h.shape
    hf = h.astype(jnp.float32)
    cs = jnp.cumsum(hf, axis=1)
    cs0 = jnp.concatenate([jnp.zeros_like(cs[:, :1]), cs], axis=1)
    pos = jnp.arange(S_)
    pooled = []
    for g, win in enumerate(POOL_WINDOWS):
        sl = slice(g * POOL_GROUP_DIM, (g + 1) * POOL_GROUP_DIM)
        c_in = cs[:, :, sl]
        c_lo = jnp.pad(cs0[:, :S_ + 1 - win, sl], ((0, 0), (win - 1, 0), (0, 0)))
        count = jnp.minimum(pos + 1, win).astype(jnp.float32)
        pooled.append((c_in - c_lo) / count[None, :, None])
    pooled = jnp.stack(pooled, axis=2)
    diff = pooled - hf.reshape(B_, S_, N_POOL_GROUPS, POOL_GROUP_DIM)
    y = jnp.einsum('bsgc,gcd->bsgd', diff, w.astype(jnp.float32)) + b.astype(jnp.float32)
    return (y.reshape(B_, S_, D_) * scale.astype(jnp.float32)).astype(h.dtype)


def stick_breaking_attention(q, k, v):
    B_, H_, S_, Dh = q.shape
    n_blocks = S_ // SB_BLOCK
    qf = q.astype(jnp.float32) * (Dh ** -0.5)
    kf = k.astype(jnp.float32)
    vf = v.astype(jnp.float32)
    offs = jnp.arange(SB_BLOCK)

    def query_block(i):
        q_start = i * SB_BLOCK
        qb = lax.dynamic_slice_in_dim(qf, q_start, SB_BLOCK, axis=2)
        q_pos = q_start + offs

        def key_step(step, carry):
            acc, log_rem = carry
            k_start = (i - step) * SB_BLOCK
            kb = lax.dynamic_slice_in_dim(kf, k_start, SB_BLOCK, axis=2)
            vb = lax.dynamic_slice_in_dim(vf, k_start, SB_BLOCK, axis=2)
            mask = (k_start + offs)[None, :] < q_pos[:, None]
            z = jnp.einsum('bhqd,bhkd->bhqk', qb, kb)
            log_keep = jnp.where(mask, jax.nn.log_sigmoid(-z), 0.0)
            later = lax.cumsum(log_keep, axis=3, reverse=True) - log_keep + log_rem[..., None]
            a = jnp.where(mask, jnp.exp(jax.nn.log_sigmoid(z) + later), 0.0)
            acc = acc + jnp.einsum('bhqk,bhkd->bhqd', a, vb)
            log_rem = log_rem + jnp.sum(log_keep, axis=-1)
            return acc, log_rem

        init = (jnp.zeros((B_, H_, SB_BLOCK, Dh), jnp.float32),
                jnp.zeros((B_, H_, SB_BLOCK), jnp.float32))
        acc, _ = lax.fori_loop(0, i + 1, key_step, init)
        return acc

    out = lax.map(query_block, jnp.arange(n_blocks))
    out = jnp.transpose(out, (1, 2, 0, 3, 4)).reshape(B_, H_, S_, Dh)
    return out.astype(q.dtype)


def sb_mixer(h, w_qkv, q_gain, k_gain, w_o):
    B_, S_, D_ = h.shape
    qkv = jnp.einsum('bsd,de->bse', h, w_qkv)
    q, k, v = jnp.split(qkv, 3, axis=-1)
    q = rms_norm(q.reshape(B_, S_, SB_HEADS, SB_HEAD_DIM), q_gain)
    k = rms_norm(k.reshape(B_, S_, SB_HEADS, SB_HEAD_DIM), k_gain)
    v = v.reshape(B_, S_, SB_HEADS, SB_HEAD_DIM)
    o = stick_breaking_attention(q.transpose(0, 2, 1, 3), k.transpose(0, 2, 1, 3),
                                 v.transpose(0, 2, 1, 3))
    o = o.transpose(0, 2, 1, 3).reshape(B_, S_, D_)
    return jnp.einsum('bse,ed->bsd', o, w_o).astype(h.dtype)


def hierarchical_moe(h, w_group, b_group, w_router, b_router, w_gate, w_up, w_down):
    B_, S_, D_ = h.shape
    t = h.reshape(-1, D_)
    group_probs = jax.nn.softmax((t @ w_group + b_group).astype(jnp.float32), axis=-1)
    g_prob, g_idx = lax.top_k(group_probs, 1)
    expert_logits = (t @ w_router + b_router).astype(jnp.float32)
    expert_logits = expert_logits.reshape(-1, N_GROUPS, EXPERTS_PER_GROUP)
    in_group = jnp.take_along_axis(expert_logits, g_idx[:, :, None], axis=1)[:, 0]
    e_prob, e_idx = lax.top_k(jax.nn.softmax(in_group, axis=-1), TOP_K_IN_GROUP)
    e_prob = e_prob / jnp.sum(e_prob, axis=-1, keepdims=True)
    weights = g_prob * e_prob
    expert_ids = g_idx * EXPERTS_PER_GROUP + e_idx
    gate = jnp.sum(jax.nn.one_hot(expert_ids, N_EXPERTS, dtype=jnp.float32) * weights[..., None],
                   axis=1)
    hid = jax.nn.silu(jnp.einsum('td,edf->tef', t, w_gate)) * jnp.einsum('td,edf->tef', t, w_up)
    hid = hid * gate[:, :, None].astype(hid.dtype)
    y = jnp.einsum('tef,efd->td', hid, w_down)
    return y.reshape(B_, S_, D_).astype(h.dtype)


def setup_inputs(seed: int = 0) -> dict:
    key = jax.random.key(seed)
    ks = jax.random.split(key, 18)
    f32 = jnp.float32

    def normal(k, shape, scale):
        return jax.random.normal(k, shape, f32) * scale

    NP, NS, D = N_POOL_LAYERS, N_SB_LAYERS, D_MODEL
    return {
        'x': normal(ks[0], (BATCH, SEQ, D), 1.0),
        'pool_norm': 1.0 + normal(ks[1], (NP, D), 0.02),
        'pool_w': normal(ks[2], (NP, N_POOL_GROUPS, POOL_GROUP_DIM, POOL_GROUP_DIM), POOL_GROUP_DIM ** -0.5),
        'pool_b': normal(ks[3], (NP, N_POOL_GROUPS, POOL_GROUP_DIM), 0.01),
        'pool_scale': 0.5 + normal(ks[4], (NP, D), 0.05),
        'sb_norm': 1.0 + normal(ks[5], (NS, D), 0.02),
        'sb_w_qkv': normal(ks[6], (NS, D, 3 * D), D ** -0.5),
        'sb_q_gain': 1.0 + normal(ks[7], (NS, SB_HEAD_DIM), 0.02),
        'sb_k_gain': 1.0 + normal(ks[8], (NS, SB_HEAD_DIM), 0.02),
        'sb_w_o': normal(ks[9], (NS, D, D), D ** -0.5),
        'moe_norm': 1.0 + normal(ks[10], (DEPTH, D), 0.02),
        'moe_w_group': normal(ks[11], (DEPTH, D, N_GROUPS), D ** -0.5),
        'moe_b_group': normal(ks[12], (DEPTH, N_GROUPS), 0.01),
        'moe_w_router': normal(ks[13], (DEPTH, D, N_EXPERTS), D ** -0.5),
        'moe_b_router': normal(ks[14], (DEPTH, N_EXPERTS), 0.01),
        'moe_w_gate': normal(ks[15], (DEPTH, N_EXPERTS, D, D_EXPERT), D ** -0.5),
        'moe_w_up': normal(ks[16], (DEPTH, N_EXPERTS, D, D_EXPERT), D ** -0.5),
        'moe_w_down': normal(ks[17], (DEPTH, N_EXPERTS, D_EXPERT, D), D_EXPERT ** -0.5),
    }


def reference(x, pool_norm, pool_w, pool_b, pool_scale, sb_norm, sb_w_qkv, sb_q_gain, sb_k_gain,
              sb_w_o, moe_norm, moe_w_group, moe_b_group, moe_w_router, moe_b_router,
              moe_w_gate, moe_w_up, moe_w_down):
    for i in range(DEPTH):
        j = i // N_MIXERS
        if i % N_MIXERS == 0:
            h = rms_norm(x, pool_norm[j])
            x = x + pool_mixer(h, pool_w[j], pool_b[j], pool_scale[j]).astype(x.dtype)
        else:
            h = rms_norm(x, sb_norm[j])
            x = x + sb_mixer(h, sb_w_qkv[j], sb_q_gain[j], sb_k_gain[j], sb_w_o[j]).astype(x.dtype)
        h = rms_norm(x, moe_norm[i])
        x = x + hierarchical_moe(h, moe_w_group[i], moe_b_group[i], moe_w_router[i], moe_b_router[i],
                                 moe_w_gate[i], moe_w_up[i], moe_w_down[i]).astype(x.dtype)
    return x
```

```python
import functools

import jax
import jax.numpy as jnp
from jax import lax
from jax.experimental import pallas as pl
from jax.experimental.pallas import tpu as pltpu

F32 = jnp.float32
BF16 = jnp.bfloat16

D_MODEL = 1024
DEPTH = 4
POOL_WINDOWS = (2, 4, 8, 16)
POOL_GROUP_DIM = D_MODEL // len(POOL_WINDOWS)
POOL_HALO = 16
SB_HEADS = 16
SB_HEAD_DIM = D_MODEL // SB_HEADS
N_GROUPS = 4
EXPERTS_PER_GROUP = 8
N_EXPERTS = N_GROUPS * EXPERTS_PER_GROUP
D_EXPERT = D_MODEL // 4
RMS_EPS = 1e-6

LANES = 128
VMEM_LIMIT_BYTES = 56 * 1024 * 1024

POOL_TILE = 512
ROUTER_TILE = 512
MOE_TILE = 1024
QKV_TILE = 512
ATTN_Q_TILE = 256
ATTN_K_TILE = 128
OPROJ_TILE = 512


def _params(*sem):
    return pltpu.CompilerParams(dimension_semantics=sem, vmem_limit_bytes=VMEM_LIMIT_BYTES)


def _rms(v, gain):
    return v * lax.rsqrt(jnp.mean(v * v, axis=-1, keepdims=True) + RMS_EPS) * gain


def _dot(a, b):
    return jnp.dot(a, b, preferred_element_type=F32)


def _split_bf16(v):
    hi = v.astype(BF16)
    lo = (v - hi.astype(F32)).astype(BF16)
    return hi, lo


def _pool_kernel(xprev_ref, x_ref, norm_ref, w_ref, b_ref, scale_ref, o_ref, hbuf):
    i = pl.program_id(1)
    ts = x_ref.shape[1]
    x = x_ref[0]
    gain = norm_ref[...]
    h = _rms(x, gain)
    hprev = jnp.where(i > 0, _rms(xprev_ref[0], gain), 0.0)
    hbuf[0:POOL_HALO, :] = hprev
    hbuf[POOL_HALO:POOL_HALO + ts, :] = h
    pos = i * ts + lax.broadcasted_iota(jnp.int32, (ts, 1), 0)
    for g, win in enumerate(POOL_WINDOWS):
        c0 = g * POOL_GROUP_DIM
        cols = slice(c0, c0 + POOL_GROUP_DIM)
        hg = h[:, cols]
        acc = hg
        for j in range(1, win):
            acc = acc + hbuf[POOL_HALO - j:POOL_HALO - j + ts, cols]
        count = jnp.minimum(pos + 1, win).astype(F32)
        diff = acc / count - hg
        y = _dot(diff.astype(BF16), w_ref[g]) + b_ref[g]
        o_ref[0, :, cols] = x[:, cols] + y * scale_ref[:, cols]


def _pool_layer(x, norm, w, b, scale):
    B, S, D = x.shape
    ts = min(POOL_TILE, S)
    halo_blocks = ts // POOL_HALO
    return pl.pallas_call(
        _pool_kernel,
        grid=(B, S // ts),
        in_specs=[
            pl.BlockSpec((1, POOL_HALO, D), lambda bi, i: (bi, jnp.maximum(i * halo_blocks - 1, 0), 0)),
            pl.BlockSpec((1, ts, D), lambda bi, i: (bi, i, 0)),
            pl.BlockSpec((1, D), lambda bi, i: (0, 0)),
            pl.BlockSpec((len(POOL_WINDOWS), POOL_GROUP_DIM, POOL_GROUP_DIM), lambda bi, i: (0, 0, 0)),
            pl.BlockSpec((len(POOL_WINDOWS), 1, POOL_GROUP_DIM), lambda bi, i: (0, 0, 0)),
            pl.BlockSpec((1, D), lambda bi, i: (0, 0)),
        ],
        out_specs=pl.BlockSpec((1, ts, D), lambda bi, i: (bi, i, 0)),
        out_shape=jax.ShapeDtypeStruct((B, S, D), F32),
        scratch_shapes=[pltpu.VMEM((POOL_HALO + ts, D), F32)],
        compiler_params=_params("parallel", "parallel"),
        name="pool_layer",
    )(x, x, norm.reshape(1, D), w.astype(BF16), b.reshape(len(POOL_WINDOWS), 1, POOL_GROUP_DIM),
      scale.reshape(1, D))


ROUTER_GROUP_LANE0 = N_EXPERTS


def _router_kernel(x_ref, norm_ref, whi_ref, wlo_ref, bias_ref, h_ref, gate_ref):
    h = _rms(x_ref[...], norm_ref[...])
    h_hi, h_lo = _split_bf16(h)
    w_hi = whi_ref[...]
    logits = _dot(h_hi, w_hi) + _dot(h_lo, w_hi) + _dot(h_hi, wlo_ref[...]) + bias_ref[...]
    h_ref[...] = h_hi

    lane = lax.broadcasted_iota(jnp.int32, logits.shape, 1).astype(F32)
    neg_inf = F32(-jnp.inf)
    no_lane = F32(4 * LANES)

    def first_argmax(v):
        m = jnp.max(v, axis=1, keepdims=True)
        idx = jnp.min(jnp.where(v == m, lane, no_lane), axis=1, keepdims=True)
        return m, idx

    is_group = (lane >= ROUTER_GROUP_LANE0) & (lane < ROUTER_GROUP_LANE0 + N_GROUPS)
    gl = jnp.where(is_group, logits, neg_inf)
    gmax, glane = first_argmax(gl)
    g_prob = 1.0 / jnp.sum(jnp.exp(gl - gmax), axis=1, keepdims=True)
    e0 = (glane - ROUTER_GROUP_LANE0) * EXPERTS_PER_GROUP
    el = jnp.where((lane >= e0) & (lane < e0 + EXPERTS_PER_GROUP), logits, neg_inf)
    m1, i1 = first_argmax(el)
    m2, i2 = first_argmax(jnp.where(lane == i1, neg_inf, el))
    r = jnp.exp(m2 - m1)
    w1 = 1.0 / (1.0 + r)
    gate_ref[...] = jnp.where(lane == i1, g_prob * w1, 0.0) + jnp.where(lane == i2, g_prob * (r * w1), 0.0)


def _router(x2, norm, w_group, b_group, w_router, b_router):
    T, D = x2.shape
    tm = min(ROUTER_TILE, T)
    w = jnp.zeros((D, LANES), F32)
    w = w.at[:, :N_EXPERTS].set(w_router).at[:, ROUTER_GROUP_LANE0:ROUTER_GROUP_LANE0 + N_GROUPS].set(w_group)
    w_hi = w.astype(BF16)
    w_lo = (w - w_hi.astype(F32)).astype(BF16)
    bias = jnp.zeros((1, LANES), F32)
    bias = bias.at[0, :N_EXPERTS].set(b_router).at[0, ROUTER_GROUP_LANE0:ROUTER_GROUP_LANE0 + N_GROUPS].set(b_group)
    return pl.pallas_call(
        _router_kernel,
        grid=(T // tm,),
        in_specs=[
            pl.BlockSpec((tm, D), lambda i: (i, 0)),
            pl.BlockSpec((1, D), lambda i: (0, 0)),
            pl.BlockSpec((D, LANES), lambda i: (0, 0)),
            pl.BlockSpec((D, LANES), lambda i: (0, 0)),
            pl.BlockSpec((1, LANES), lambda i: (0, 0)),
        ],
        out_specs=[pl.BlockSpec((tm, D), lambda i: (i, 0)), pl.BlockSpec((tm, LANES), lambda i: (i, 0))],
        out_shape=[jax.ShapeDtypeStruct((T, D), BF16), jax.ShapeDtypeStruct((T, LANES), F32)],
        compiler_params=_params("parallel"),
        name="moe_router",
    )(x2, norm.reshape(1, D), w_hi, w_lo, bias)


def _moe_kernel(x_ref, h_ref, gate_ref, wg_ref, wu_ref, wd_ref, o_ref):
    e = pl.program_id(1)

    @pl.when(e == 0)
    def _():
        o_ref[...] = x_ref[...]

    h = h_ref[...]
    a = _dot(h, wg_ref[0])
    u = _dot(h, wu_ref[0])
    gate = gate_ref[...]
    lane = lax.broadcasted_iota(jnp.int32, gate.shape, 1)
    gcol = jnp.sum(jnp.where(lane == e, gate, 0.0), axis=1, keepdims=True)
    hid = (a * jax.nn.sigmoid(a)) * u * gcol
    o_ref[...] += _dot(hid.astype(BF16), wd_ref[0])


def _moe_experts(x2, h, gate, w_gate, w_up, w_down):
    T, D = x2.shape
    tm = min(MOE_TILE, T)
    return pl.pallas_call(
        _moe_kernel,
        grid=(T // tm, N_EXPERTS),
        in_specs=[
            pl.BlockSpec((tm, D), lambda i, e: (i, 0)),
            pl.BlockSpec((tm, D), lambda i, e: (i, 0)),
            pl.BlockSpec((tm, LANES), lambda i, e: (i, 0)),
            pl.BlockSpec((1, D, D_EXPERT), lambda i, e: (e, 0, 0)),
            pl.BlockSpec((1, D, D_EXPERT), lambda i, e: (e, 0, 0)),
            pl.BlockSpec((1, D_EXPERT, D), lambda i, e: (e, 0, 0)),
        ],
        out_specs=pl.BlockSpec((tm, D), lambda i, e: (i, 0)),
        out_shape=jax.ShapeDtypeStruct((T, D), F32),
        compiler_params=_params("parallel", "arbitrary"),
        name="moe_experts",
    )(x2, h, gate, w_gate.astype(BF16), w_up.astype(BF16), w_down.astype(BF16))


def _moe_layer(x, norm, w_group, b_group, w_router, b_router, w_gate, w_up, w_down):
    B, S, D = x.shape
    x2 = x.reshape(B * S, D)
    h, gate = _router(x2, norm, w_group, b_group, w_router, b_router)
    return _moe_experts(x2, h, gate, w_gate, w_up, w_down).reshape(B, S, D)


def _qkv_kernel(x_ref, norm_ref, w_ref, seg_ref, segt_ref, qg_ref, kg_ref, q_ref, k_ref, v_ref):
    D = x_ref.shape[1]
    h = _rms(x_ref[...], norm_ref[...]).astype(BF16)
    qkv = _dot(h, w_ref[...])
    seg = seg_ref[...]
    segt = segt_ref[...]

    def head_norm(t, gain):
        ms = _dot((t * t).astype(BF16), seg) * (1.0 / SB_HEAD_DIM)
        r_hi, r_lo = _split_bf16(lax.rsqrt(ms + RMS_EPS))
        return t * (_dot(r_hi, segt) + _dot(r_lo, segt)) * gain

    q_ref[...] = head_norm(qkv[:, :D], qg_ref[...]).astype(BF16)
    k_ref[...] = head_norm(qkv[:, D:2 * D], kg_ref[...]).astype(BF16)
    v_ref[...] = qkv[:, 2 * D:].astype(BF16)


def _qkv(x2, norm, w_qkv, q_gain, k_gain):
    T, D = x2.shape
    tm = min(QKV_TILE, T)
    head_of = jnp.arange(D) // SB_HEAD_DIM
    seg = (head_of[:, None] == jnp.arange(LANES)[None, :]).astype(BF16)
    segt = seg.T
    qg = (jnp.tile(q_gain, SB_HEADS) * (SB_HEAD_DIM ** -0.5)).reshape(1, D)
    kg = jnp.tile(k_gain, SB_HEADS).reshape(1, D)
    row = pl.BlockSpec((tm, D), lambda i: (i, 0))
    vec = pl.BlockSpec((1, D), lambda i: (0, 0))
    return pl.pallas_call(
        _qkv_kernel,
        grid=(T // tm,),
        in_specs=[row, vec, pl.BlockSpec((D, 3 * D), lambda i: (0, 0)),
                  pl.BlockSpec((D, LANES), lambda i: (0, 0)), pl.BlockSpec((LANES, D), lambda i: (0, 0)),
                  vec, vec],
        out_specs=[row, row, row],
        out_shape=[jax.ShapeDtypeStruct((T, D), BF16)] * 3,
        compiler_params=_params("parallel"),
        name="sb_qkv",
    )(x2, norm.reshape(1, D), w_qkv.astype(BF16), seg, segt, qg, kg)


def _attn_kernel(q_ref, k_ref, v_ref, u_ref, o_ref, acc_ref, rem_ref, *, tq, tk):
    qi = pl.program_id(2)
    q2 = q_ref[...]
    lane = lax.broadcasted_iota(jnp.int32, (tq, LANES), 1)
    first_head = lane < SB_HEAD_DIM
    zero = jnp.zeros_like(q2)
    q_heads = (jnp.where(first_head, q2, zero), jnp.where(first_head, zero, q2))
    u2 = u_ref[...]
    acc_ref[...] = jnp.zeros_like(acc_ref)
    rem_ref[...] = jnp.zeros_like(rem_ref)
    q_start = qi * tq
    tiles_per_q = tq // tk

    def key_tile(kj, masked):
        k_start = pl.multiple_of(kj * tk, tk)
        kt = k_ref[0, pl.ds(k_start, tk), :]
        vt = v_ref[0, pl.ds(k_start, tk), :]
        if masked:
            row = lax.broadcasted_iota(jnp.int32, (tq, tk), 0)
            col = lax.broadcasted_iota(jnp.int32, (tq, tk), 1)
            mask = (k_start + col) < (q_start + row)
        pv = []
        for hd in range(2):
            z = lax.dot_general(q_heads[hd], kt, (((1,), (1,)), ((), ())), preferred_element_type=F32)
            nz = -z
            softplus_tail = jnp.log(1.0 + jnp.exp(jnp.minimum(z, nz)))
            log_keep = jnp.minimum(nz, 0.0) - softplus_tail
            if masked:
                log_keep = jnp.where(mask, log_keep, 0.0)
            lk_hi, lk_lo = _split_bf16(log_keep)
            c = _dot(lk_hi, u2) + _dot(lk_lo, u2)
            rem = rem_ref[hd]
            a = jnp.exp((z + log_keep) + (c[:, :tk] + rem))
            if masked:
                a = jnp.where(mask, a, 0.0)
            pv.append(_dot(a.astype(BF16), vt))
            rem_ref[hd] = rem + c[:, tk:]
        acc_ref[...] += jnp.where(first_head, pv[0], pv[1])

    for t in range(tiles_per_q):
        key_tile((qi + 1) * tiles_per_q - 1 - t, True)

    def body(step, carry):
        key_tile(qi * tiles_per_q - 1 - step, False)
        return carry

    lax.fori_loop(0, qi * tiles_per_q, body, 0)
    o_ref[...] = acc_ref[...].astype(o_ref.dtype)


def _attention(q, k, v, B, S):
    T, D = q.shape
    tq = min(ATTN_Q_TILE, S)
    tk = min(ATTN_K_TILE, tq)
    nq = S // tq
    key_idx = jnp.arange(tk)
    later = (key_idx[:, None] > key_idx[None, :]).astype(BF16)
    u2 = jnp.concatenate([later, jnp.ones((tk, tk), BF16)], axis=1)
    k3 = k.reshape(B, S, D)
    v3 = v.reshape(B, S, D)
    kv_spec = pl.BlockSpec((1, S, LANES), lambda b, hp, i: (b, 0, hp))
    return pl.pallas_call(
        functools.partial(_attn_kernel, tq=tq, tk=tk),
        grid=(B, D // LANES, nq),
        in_specs=[pl.BlockSpec((tq, LANES), lambda b, hp, i: (b * nq + i, hp)), kv_spec, kv_spec,
                  pl.BlockSpec((tk, 2 * tk), lambda b, hp, i: (0, 0))],
        out_specs=pl.BlockSpec((tq, LANES), lambda b, hp, i: (b * nq + i, hp)),
        out_shape=jax.ShapeDtypeStruct((T, D), BF16),
        scratch_shapes=[pltpu.VMEM((tq, LANES), F32), pltpu.VMEM((2, tq, tk), F32)],
        compiler_params=_params("parallel", "parallel", "arbitrary"),
        name="sb_attention",
    )(q, k3, v3, u2)


def _oproj_kernel(x_ref, o_ref, w_ref, out_ref):
    out_ref[...] = x_ref[...] + _dot(o_ref[...], w_ref[...])


def _oproj(x2, o, w_o):
    T, D = x2.shape
    tm = min(OPROJ_TILE, T)
    row = pl.BlockSpec((tm, D), lambda i: (i, 0))
    return pl.pallas_call(
        _oproj_kernel,
        grid=(T // tm,),
        in_specs=[row, row, pl.BlockSpec((D, D), lambda i: (0, 0))],
        out_specs=row,
        out_shape=jax.ShapeDtypeStruct((T, D), F32),
        compiler_params=_params("parallel"),
        name="sb_oproj",
    )(x2, o, w_o.astype(BF16))


def _sb_layer(x, norm, w_qkv, q_gain, k_gain, w_o):
    B, S, D = x.shape
    x2 = x.reshape(B * S, D)
    q, k, v = _qkv(x2, norm, w_qkv, q_gain, k_gain)
    o = _attention(q, k, v, B, S)
    return _oproj(x2, o, w_o).reshape(B, S, D)


def kernel(x, pool_norm, pool_w, pool_b, pool_scale, sb_norm, sb_w_qkv, sb_q_gain, sb_k_gain, sb_w_o, moe_norm, moe_w_group, moe_b_group, moe_w_router, moe_b_router, moe_w_gate, moe_w_up, moe_w_down):
    for i in range(DEPTH):
        j = i // 2
        if i % 2 == 0:
            x = _pool_layer(x, pool_norm[j], pool_w[j], pool_b[j], pool_scale[j])
        else:
            x = _sb_layer(x, sb_norm[j], sb_w_qkv[j], sb_q_gain[j], sb_k_gain[j], sb_w_o[j])
        x = _moe_layer(x, moe_norm[i], moe_w_group[i], moe_b_group[i], moe_w_router[i], moe_b_router[i],
                       moe_w_gate[i], moe_w_up[i], moe_w_down[i])
    return x
```

```python
import functools

import jax
import jax.numpy as jnp
from jax import lax
from jax.experimental import pallas as pl
from jax.experimental.pallas import tpu as pltpu

F32 = jnp.float32
BF16 = jnp.bfloat16

D_MODEL = 1024
DEPTH = 4
POOL_WINDOWS = (2, 4, 8, 16)
POOL_GROUP_DIM = D_MODEL // len(POOL_WINDOWS)
POOL_HALO = 16
SB_HEADS = 16
SB_HEAD_DIM = D_MODEL // SB_HEADS
N_GROUPS = 4
EXPERTS_PER_GROUP = 8
N_EXPERTS = N_GROUPS * EXPERTS_PER_GROUP
D_EXPERT = D_MODEL // 4
RMS_EPS = 1e-6
LOG2_E = 1.4426950408889634
LN_2 = 0.6931471805599453
ZERO_WEIGHT_LOG2 = 150.0

LANES = 128
VMEM_LIMIT_BYTES = 56 * 1024 * 1024

POOL_TILE = 512
ROUTER_TILE = 512
MOE_TILE = 1024
QKV_TILE = 512
ATTN_TILE = 256
OPROJ_TILE = 512


def _params(*sem):
    return pltpu.CompilerParams(dimension_semantics=sem, vmem_limit_bytes=VMEM_LIMIT_BYTES)


def _rms(v, gain):
    return v * lax.rsqrt(jnp.mean(v * v, axis=-1, keepdims=True) + RMS_EPS) * gain


def _dot(a, b):
    return jnp.dot(a, b, preferred_element_type=F32)


def _split_bf16(v):
    hi = v.astype(BF16)
    lo = (v - hi.astype(F32)).astype(BF16)
    return hi, lo


def _pool_kernel(xprev_ref, x_ref, norm_ref, w_ref, b_ref, scale_ref, o_ref, hbuf):
    i = pl.program_id(1)
    ts = x_ref.shape[1]
    x = x_ref[0]
    gain = norm_ref[...]
    h = _rms(x, gain)
    hprev = jnp.where(i > 0, _rms(xprev_ref[0], gain), 0.0)
    hbuf[0:POOL_HALO, :] = hprev
    hbuf[POOL_HALO:POOL_HALO + ts, :] = h
    pos = i * ts + lax.broadcasted_iota(jnp.int32, (ts, 1), 0)
    for g, win in enumerate(POOL_WINDOWS):
        c0 = g * POOL_GROUP_DIM
        cols = slice(c0, c0 + POOL_GROUP_DIM)
        hg = h[:, cols]
        acc = hg
        for j in range(1, win):
            acc = acc + hbuf[POOL_HALO - j:POOL_HALO - j + ts, cols]
        count = jnp.minimum(pos + 1, win).astype(F32)
        diff = acc / count - hg
        y = _dot(diff.astype(BF16), w_ref[g]) + b_ref[g]
        o_ref[0, :, cols] = x[:, cols] + y * scale_ref[:, cols]


def _pool_layer(x, norm, w, b, scale):
    B, S, D = x.shape
    ts = min(POOL_TILE, S)
    halo_blocks = ts // POOL_HALO
    return pl.pallas_call(
        _pool_kernel,
        grid=(B, S // ts),
        in_specs=[
            pl.BlockSpec((1, POOL_HALO, D), lambda bi, i: (bi, jnp.maximum(i * halo_blocks - 1, 0), 0)),
            pl.BlockSpec((1, ts, D), lambda bi, i: (bi, i, 0)),
            pl.BlockSpec((1, D), lambda bi, i: (0, 0)),
            pl.BlockSpec((len(POOL_WINDOWS), POOL_GROUP_DIM, POOL_GROUP_DIM), lambda bi, i: (0, 0, 0)),
            pl.BlockSpec((len(POOL_WINDOWS), 1, POOL_GROUP_DIM), lambda bi, i: (0, 0, 0)),
            pl.BlockSpec((1, D), lambda bi, i: (0, 0)),
        ],
        out_specs=pl.BlockSpec((1, ts, D), lambda bi, i: (bi, i, 0)),
        out_shape=jax.ShapeDtypeStruct((B, S, D), F32),
        scratch_shapes=[pltpu.VMEM((POOL_HALO + ts, D), F32)],
        compiler_params=_params("parallel", "parallel"),
        name="pool_layer",
    )(x, x, norm.reshape(1, D), w.astype(BF16), b.reshape(len(POOL_WINDOWS), 1, POOL_GROUP_DIM),
      scale.reshape(1, D))


ROUTER_GROUP_LANE0 = N_EXPERTS


def _router_kernel(x_ref, norm_ref, whi_ref, wlo_ref, bias_ref, h_ref, gate_ref):
    h = _rms(x_ref[...], norm_ref[...])
    h_hi, h_lo = _split_bf16(h)
    w_hi = whi_ref[...]
    logits = _dot(h_hi, w_hi) + _dot(h_lo, w_hi) + _dot(h_hi, wlo_ref[...]) + bias_ref[...]
    h_ref[...] = h_hi

    lane = lax.broadcasted_iota(jnp.int32, logits.shape, 1).astype(F32)
    neg_inf = F32(-jnp.inf)
    no_lane = F32(4 * LANES)

    def first_argmax(v):
        m = jnp.max(v, axis=1, keepdims=True)
        idx = jnp.min(jnp.where(v == m, lane, no_lane), axis=1, keepdims=True)
        return m, idx

    is_group = (lane >= ROUTER_GROUP_LANE0) & (lane < ROUTER_GROUP_LANE0 + N_GROUPS)
    gl = jnp.where(is_group, logits, neg_inf)
    gmax, glane = first_argmax(gl)
    g_prob = 1.0 / jnp.sum(jnp.exp(gl - gmax), axis=1, keepdims=True)
    e0 = (glane - ROUTER_GROUP_LANE0) * EXPERTS_PER_GROUP
    el = jnp.where((lane >= e0) & (lane < e0 + EXPERTS_PER_GROUP), logits, neg_inf)
    m1, i1 = first_argmax(el)
    m2, i2 = first_argmax(jnp.where(lane == i1, neg_inf, el))
    r = jnp.exp(m2 - m1)
    w1 = 1.0 / (1.0 + r)
    gate_ref[...] = jnp.where(lane == i1, g_prob * w1, 0.0) + jnp.where(lane == i2, g_prob * (r * w1), 0.0)


def _router(x2, norm, w_group, b_group, w_router, b_router):
    T, D = x2.shape
    tm = min(ROUTER_TILE, T)
    w = jnp.zeros((D, LANES), F32)
    w = w.at[:, :N_EXPERTS].set(w_router).at[:, ROUTER_GROUP_LANE0:ROUTER_GROUP_LANE0 + N_GROUPS].set(w_group)
    w_hi = w.astype(BF16)
    w_lo = (w - w_hi.astype(F32)).astype(BF16)
    bias = jnp.zeros((1, LANES), F32)
    bias = bias.at[0, :N_EXPERTS].set(b_router).at[0, ROUTER_GROUP_LANE0:ROUTER_GROUP_LANE0 + N_GROUPS].set(b_group)
    return pl.pallas_call(
        _router_kernel,
        grid=(T // tm,),
        in_specs=[
            pl.BlockSpec((tm, D), lambda i: (i, 0)),
            pl.BlockSpec((1, D), lambda i: (0, 0)),
            pl.BlockSpec((D, LANES), lambda i: (0, 0)),
            pl.BlockSpec((D, LANES), lambda i: (0, 0)),
            pl.BlockSpec((1, LANES), lambda i: (0, 0)),
        ],
        out_specs=[pl.BlockSpec((tm, D), lambda i: (i, 0)), pl.BlockSpec((tm, LANES), lambda i: (i, 0))],
        out_shape=[jax.ShapeDtypeStruct((T, D), BF16), jax.ShapeDtypeStruct((T, LANES), F32)],
        compiler_params=_params("parallel"),
        name="moe_router",
    )(x2, norm.reshape(1, D), w_hi, w_lo, bias)


def _moe_kernel(x_ref, h_ref, gate_ref, wg_ref, wu_ref, wd_ref, o_ref):
    e = pl.program_id(1)

    @pl.when(e == 0)
    def _():
        o_ref[...] = x_ref[...]

    h = h_ref[...]
    a = _dot(h, wg_ref[0])
    u = _dot(h, wu_ref[0])
    gate = gate_ref[...]
    lane = lax.broadcasted_iota(jnp.int32, gate.shape, 1)
    gcol = jnp.sum(jnp.where(lane == e, gate, 0.0), axis=1, keepdims=True)
    hid = (a * jax.nn.sigmoid(a)) * u * gcol
    o_ref[...] += _dot(hid.astype(BF16), wd_ref[0])


def _moe_experts(x2, h, gate, w_gate, w_up, w_down):
    T, D = x2.shape
    tm = min(MOE_TILE, T)
    return pl.pallas_call(
        _moe_kernel,
        grid=(T // tm, N_EXPERTS),
        in_specs=[
            pl.BlockSpec((tm, D), lambda i, e: (i, 0)),
            pl.BlockSpec((tm, D), lambda i, e: (i, 0)),
            pl.BlockSpec((tm, LANES), lambda i, e: (i, 0)),
            pl.BlockSpec((1, D, D_EXPERT), lambda i, e: (e, 0, 0)),
            pl.BlockSpec((1, D, D_EXPERT), lambda i, e: (e, 0, 0)),
            pl.BlockSpec((1, D_EXPERT, D), lambda i, e: (e, 0, 0)),
        ],
        out_specs=pl.BlockSpec((tm, D), lambda i, e: (i, 0)),
        out_shape=jax.ShapeDtypeStruct((T, D), F32),
        compiler_params=_params("parallel", "arbitrary"),
        name="moe_experts",
    )(x2, h, gate, w_gate.astype(BF16), w_up.astype(BF16), w_down.astype(BF16))


def _moe_layer(x, norm, w_group, b_group, w_router, b_router, w_gate, w_up, w_down):
    B, S, D = x.shape
    x2 = x.reshape(B * S, D)
    h, gate = _router(x2, norm, w_group, b_group, w_router, b_router)
    return _moe_experts(x2, h, gate, w_gate, w_up, w_down).reshape(B, S, D)


def _qkv_kernel(x_ref, norm_ref, w_ref, seg_ref, segt_ref, qg_ref, kg_ref, q_ref, k_ref, v_ref):
    D = x_ref.shape[1]
    h = _rms(x_ref[...], norm_ref[...]).astype(BF16)
    qkv = _dot(h, w_ref[...])
    seg = seg_ref[...]
    segt = segt_ref[...]

    def head_norm(t, gain):
        ms = _dot((t * t).astype(BF16), seg) * (1.0 / SB_HEAD_DIM)
        r_hi, r_lo = _split_bf16(lax.rsqrt(ms + RMS_EPS))
        return t * (_dot(r_hi, segt) + _dot(r_lo, segt)) * gain

    q_ref[...] = head_norm(qkv[:, :D], qg_ref[...]).astype(BF16)
    k_ref[...] = head_norm(qkv[:, D:2 * D], kg_ref[...]).astype(BF16)
    v_ref[...] = qkv[:, 2 * D:].astype(BF16)


def _qkv(x2, norm, w_qkv, q_gain, k_gain):
    T, D = x2.shape
    tm = min(QKV_TILE, T)
    head_of = jnp.arange(D) // SB_HEAD_DIM
    seg = (head_of[:, None] == jnp.arange(LANES)[None, :]).astype(BF16)
    segt = seg.T
    qg = (jnp.tile(q_gain, SB_HEADS) * (SB_HEAD_DIM ** -0.5 * LOG2_E)).reshape(1, D)
    kg = jnp.tile(k_gain, SB_HEADS).reshape(1, D)
    row = pl.BlockSpec((tm, D), lambda i: (i, 0))
    vec = pl.BlockSpec((1, D), lambda i: (0, 0))
    return pl.pallas_call(
        _qkv_kernel,
        grid=(T // tm,),
        in_specs=[row, vec, pl.BlockSpec((D, 3 * D), lambda i: (0, 0)),
                  pl.BlockSpec((D, LANES), lambda i: (0, 0)), pl.BlockSpec((LANES, D), lambda i: (0, 0)),
                  vec, vec],
        out_specs=[row, row, row],
        out_shape=[jax.ShapeDtypeStruct((T, D), BF16)] * 3,
        compiler_params=_params("parallel"),
        name="sb_qkv",
    )(x2, norm.reshape(1, D), w_qkv.astype(BF16), seg, segt, qg, kg)


ATTN_DIAG, ATTN_PASS, ATTN_NONE = 0, 1, 2


def _attn_kernel(q_ref, k_ref, v_ref, u_ref, ceil_ref, bias_ref, o_ref, qh_buf, acc_ref, rem_ref, *, t):
    qi = pl.program_id(2)
    q2 = q_ref[...]
    lane = lax.broadcasted_iota(jnp.int32, (t, LANES), 1)
    first_head = lane < SB_HEAD_DIM
    zero = jnp.zeros_like(q2)
    qh_buf[0] = jnp.where(first_head, q2, zero)
    qh_buf[1] = jnp.where(first_head, zero, q2)

    def key_tile(j):
        start = pl.multiple_of(jnp.maximum(qi - j, 0) * t, t)
        return k_ref[0, pl.ds(start, t), :], v_ref[0, pl.ds(start, t), :]

    def scores(hd, kt):
        return lax.dot_general(qh_buf[hd], kt, (((1,), (1,)), ((), ())), preferred_element_type=F32)

    def drop_of(z, ceil=None):
        drop = jnp.maximum(z, 0.0) + jnp.log2(1.0 + jnp.exp(jnp.abs(z) * (-LN_2)))
        log_beta = z - drop
        if ceil is not None:
            drop = jnp.minimum(drop, ceil)
        return drop, log_beta

    def suffix_sums(drop):
        d_hi, d_lo = _split_bf16(drop)
        return _dot(jnp.concatenate([d_hi, d_lo], axis=1), u_ref[...])

    def weights(log_beta, later, bias=None):
        arg = log_beta - later
        if bias is not None:
            arg = arg + bias
        return jnp.exp2(arg).astype(BF16)

    kinds = (ATTN_DIAG, jnp.where(qi >= 1, ATTN_PASS, ATTN_NONE))
    kv = [key_tile(0), key_tile(1)]
    z = [[scores(hd, kv[j][0]) for j in range(2)] for hd in range(2)]
    dl = [[drop_of(z[hd][j], ceil_ref[kinds[j]]) for j in range(2)] for hd in range(2)]
    c = [[suffix_sums(dl[hd][j][0]) for j in range(2)] for hd in range(2)]
    pv = []
    for hd in range(2):
        (d0, lb0), (d1, lb1) = dl[hd]
        rem1 = c[hd][0][:, 0:1] + d0[:, 0:1]
        a0 = weights(lb0, c[hd][0], bias_ref[kinds[0]])
        a1 = weights(lb1, c[hd][1] + rem1, bias_ref[kinds[1]])
        pv.append(_dot(a0, kv[0][1]) + _dot(a1, kv[1][1]))
        rem_ref[hd] = rem1 + (c[hd][1][:, 0:1] + d1[:, 0:1])
    acc_ref[...] = jnp.where(first_head, pv[0], pv[1])

    def rem_min():
        return jnp.min(jnp.minimum(rem_ref[0], rem_ref[1]))

    def more(carry):
        j, smallest_rem = carry
        return (j <= qi) & (smallest_rem < ZERO_WEIGHT_LOG2)

    def sweep(carry):
        j, _ = carry
        kt, vt = key_tile(j)
        pv = []
        for hd in range(2):
            drop, log_beta = drop_of(scores(hd, kt))
            c = suffix_sums(drop)
            rem = rem_ref[hd]
            pv.append(_dot(weights(log_beta, c + rem), vt))
            rem_ref[hd] = rem + (c[:, 0:1] + drop[:, 0:1])
        acc_ref[...] += jnp.where(first_head, pv[0], pv[1])
        return j + 1, rem_min()

    lax.while_loop(more, sweep, (jnp.int32(2), rem_min()))
    o_ref[...] = acc_ref[...].astype(o_ref.dtype)


def _attention(q, k, v, B, S):
    T, D = q.shape
    t = min(ATTN_TILE, S)
    nq = S // t
    idx = jnp.arange(t)
    u = (idx[:, None] > idx[None, :]).astype(BF16)
    u2 = jnp.concatenate([u, u], axis=0)
    causal = idx[None, :] < idx[:, None]
    allowed = jnp.stack([causal, jnp.ones_like(causal), jnp.zeros_like(causal)])
    ceil = jnp.where(allowed, jnp.inf, 0.0).astype(F32)
    bias = jnp.where(allowed, 0.0, -jnp.inf).astype(F32)
    k3 = k.reshape(B, S, D)
    v3 = v.reshape(B, S, D)
    kv_spec = pl.BlockSpec((1, S, LANES), lambda b, hp, i: (b, 0, hp))
    table_spec = pl.BlockSpec((3, t, t), lambda b, hp, i: (0, 0, 0))
    return pl.pallas_call(
        functools.partial(_attn_kernel, t=t),
        grid=(B, D // LANES, nq),
        in_specs=[pl.BlockSpec((t, LANES), lambda b, hp, i: (b * nq + i, hp)), kv_spec, kv_spec,
                  pl.BlockSpec((2 * t, t), lambda b, hp, i: (0, 0)), table_spec, table_spec],
        out_specs=pl.BlockSpec((t, LANES), lambda b, hp, i: (b * nq + i, hp)),
        out_shape=jax.ShapeDtypeStruct((T, D), BF16),
        scratch_shapes=[pltpu.VMEM((2, t, LANES), BF16), pltpu.VMEM((t, LANES), F32),
                        pltpu.VMEM((2, t, 1), F32)],
        compiler_params=_params("parallel", "parallel", "arbitrary"),
        name="sb_attention",
    )(q, k3, v3, u2, ceil, bias)


def _oproj_kernel(x_ref, o_ref, w_ref, out_ref):
    out_ref[...] = x_ref[...] + _dot(o_ref[...], w_ref[...])


def _oproj(x2, o, w_o):
    T, D = x2.shape
    tm = min(OPROJ_TILE, T)
    row = pl.BlockSpec((tm, D), lambda i: (i, 0))
    return pl.pallas_call(
        _oproj_kernel,
        grid=(T // tm,),
        in_specs=[row, row, pl.BlockSpec((D, D), lambda i: (0, 0))],
        out_specs=row,
        out_shape=jax.ShapeDtypeStruct((T, D), F32),
        compiler_params=_params("parallel"),
        name="sb_oproj",
    )(x2, o, w_o.astype(BF16))


def _sb_layer(x, norm, w_qkv, q_gain, k_gain, w_o):
    B, S, D = x.shape
    x2 = x.reshape(B * S, D)
    q, k, v = _qkv(x2, norm, w_qkv, q_gain, k_gain)
    o = _attention(q, k, v, B, S)
    return _oproj(x2, o, w_o).reshape(B, S, D)


def kernel(x, pool_norm, pool_w, pool_b, pool_scale, sb_norm, sb_w_qkv, sb_q_gain, sb_k_gain, sb_w_o, moe_norm, moe_w_group, moe_b_group, moe_w_router, moe_b_router, moe_w_gate, moe_w_up, moe_w_down):
    for i in range(DEPTH):
        j = i // 2
        if i % 2 == 0:
            x = _pool_layer(x, pool_norm[j], pool_w[j], pool_b[j], pool_scale[j])
        else:
            x = _sb_layer(x, sb_norm[j], sb_w_qkv[j], sb_q_gain[j], sb_k_gain[j], sb_w_o[j])
        x = _moe_layer(x, moe_norm[i], moe_w_group[i], moe_b_group[i], moe_w_router[i], moe_b_router[i],
                       moe_w_gate[i], moe_w_up[i], moe_w_down[i])
    return x
```

```python
import functools

import jax
import jax.numpy as jnp
from jax import lax
from jax.experimental import pallas as pl
from jax.experimental.pallas import tpu as pltpu

F32 = jnp.float32
BF16 = jnp.bfloat16

D_MODEL = 1024
DEPTH = 4
POOL_WINDOWS = (2, 4, 8, 16)
POOL_GROUP_DIM = D_MODEL // len(POOL_WINDOWS)
POOL_HALO = 16
SB_HEADS = 16
SB_HEAD_DIM = D_MODEL // SB_HEADS
N_GROUPS = 4
EXPERTS_PER_GROUP = 8
N_EXPERTS = N_GROUPS * EXPERTS_PER_GROUP
D_EXPERT = D_MODEL // 4
RMS_EPS = 1e-6
LOG2_E = 1.4426950408889634
LN_2 = 0.6931471805599453
ZERO_WEIGHT_LOG2 = 150.0

LANES = 128
VMEM_LIMIT_BYTES = 56 * 1024 * 1024

POOL_TILE = 512
ROUTER_TILE = 512
QKV_TILE = 512
ATTN_TILE = 256
OPROJ_TILE = 512


def _params(*sem):
    return pltpu.CompilerParams(dimension_semantics=sem, vmem_limit_bytes=VMEM_LIMIT_BYTES)


def _rms(v, gain):
    return v * lax.rsqrt(jnp.mean(v * v, axis=-1, keepdims=True) + RMS_EPS) * gain


def _dot(a, b):
    return jnp.dot(a, b, preferred_element_type=F32)


def _split_bf16(v):
    hi = v.astype(BF16)
    lo = (v - hi.astype(F32)).astype(BF16)
    return hi, lo


def _pool_kernel(xprev_ref, x_ref, norm_ref, w_ref, b_ref, scale_ref, o_ref, hbuf):
    i = pl.program_id(1)
    ts = x_ref.shape[1]
    x = x_ref[0]
    gain = norm_ref[...]
    h = _rms(x, gain)
    hprev = jnp.where(i > 0, _rms(xprev_ref[0], gain), 0.0)
    hbuf[0:POOL_HALO, :] = hprev
    hbuf[POOL_HALO:POOL_HALO + ts, :] = h
    pos = i * ts + lax.broadcasted_iota(jnp.int32, (ts, 1), 0)
    for g, win in enumerate(POOL_WINDOWS):
        c0 = g * POOL_GROUP_DIM
        cols = slice(c0, c0 + POOL_GROUP_DIM)
        hg = h[:, cols]
        acc = hg
        for j in range(1, win):
            acc = acc + hbuf[POOL_HALO - j:POOL_HALO - j + ts, cols]
        count = jnp.minimum(pos + 1, win).astype(F32)
        diff = acc / count - hg
        y = _dot(diff.astype(BF16), w_ref[g]) + b_ref[g]
        o_ref[0, :, cols] = x[:, cols] + y * scale_ref[:, cols]


def _pool_layer(x, norm, w, b, scale):
    B, S, D = x.shape
    ts = min(POOL_TILE, S)
    halo_blocks = ts // POOL_HALO
    return pl.pallas_call(
        _pool_kernel,
        grid=(B, S // ts),
        in_specs=[
            pl.BlockSpec((1, POOL_HALO, D), lambda bi, i: (bi, jnp.maximum(i * halo_blocks - 1, 0), 0)),
            pl.BlockSpec((1, ts, D), lambda bi, i: (bi, i, 0)),
            pl.BlockSpec((1, D), lambda bi, i: (0, 0)),
            pl.BlockSpec((len(POOL_WINDOWS), POOL_GROUP_DIM, POOL_GROUP_DIM), lambda bi, i: (0, 0, 0)),
            pl.BlockSpec((len(POOL_WINDOWS), 1, POOL_GROUP_DIM), lambda bi, i: (0, 0, 0)),
            pl.BlockSpec((1, D), lambda bi, i: (0, 0)),
        ],
        out_specs=pl.BlockSpec((1, ts, D), lambda bi, i: (bi, i, 0)),
        out_shape=jax.ShapeDtypeStruct((B, S, D), F32),
        scratch_shapes=[pltpu.VMEM((POOL_HALO + ts, D), F32)],
        compiler_params=_params("parallel", "parallel"),
        name="pool_layer",
    )(x, x, norm.reshape(1, D), w.astype(BF16), b.reshape(len(POOL_WINDOWS), 1, POOL_GROUP_DIM),
      scale.reshape(1, D))


ROUTER_GROUP_LANE0 = N_EXPERTS
GATE_GROUP_LANE = 40
GATE_RANK_LANE = 41
ROW_WIDTH = D_MODEL + LANES
MOE_ROW_TILE = 256
MOE_PAD_ROWS = N_GROUPS * MOE_ROW_TILE


def _router_kernel(x_ref, norm_ref, whi_ref, wlo_ref, bias_ref, tri_ref, row_ref, where_ref, count_ref,
                   run_ref):
    @pl.when(pl.program_id(0) == 0)
    def _():
        run_ref[...] = jnp.zeros_like(run_ref)

    x = x_ref[...]
    h = _rms(x, norm_ref[...])
    h_hi, h_lo = _split_bf16(h)
    w_hi = whi_ref[...]
    logits = _dot(h_hi, w_hi) + _dot(h_lo, w_hi) + _dot(h_hi, wlo_ref[...]) + bias_ref[...]

    lane = lax.broadcasted_iota(jnp.int32, logits.shape, 1).astype(F32)
    neg_inf = F32(-jnp.inf)
    no_lane = F32(4 * LANES)

    def first_argmax(v):
        m = jnp.max(v, axis=1, keepdims=True)
        idx = jnp.min(jnp.where(v == m, lane, no_lane), axis=1, keepdims=True)
        return m, idx

    is_group = (lane >= ROUTER_GROUP_LANE0) & (lane < ROUTER_GROUP_LANE0 + N_GROUPS)
    gl = jnp.where(is_group, logits, neg_inf)
    gmax, glane = first_argmax(gl)
    g_prob = 1.0 / jnp.sum(jnp.exp(gl - gmax), axis=1, keepdims=True)
    group = glane - ROUTER_GROUP_LANE0
    e0 = group * EXPERTS_PER_GROUP
    el = jnp.where((lane >= e0) & (lane < e0 + EXPERTS_PER_GROUP), logits, neg_inf)
    m1, i1 = first_argmax(el)
    m2, i2 = first_argmax(jnp.where(lane == i1, neg_inf, el))
    r = jnp.exp(m2 - m1)
    w1 = 1.0 / (1.0 + r)
    gate = jnp.where(lane == i1, g_prob * w1, 0.0) + jnp.where(lane == i2, g_prob * (r * w1), 0.0)

    in_group = jnp.where(lane == group, 1.0, 0.0)
    earlier = _dot(tri_ref[...], in_group.astype(BF16))
    run = run_ref[...]
    rank = jnp.sum(in_group * (earlier + run), axis=1, keepdims=True)
    run = run + jnp.sum(in_group, axis=0, keepdims=True)
    run_ref[...] = run
    count_ref[...] = run

    gate = jnp.where(lane == GATE_GROUP_LANE, group, jnp.where(lane == GATE_RANK_LANE, rank, gate))
    row_ref[:, :D_MODEL] = x
    row_ref[:, D_MODEL:] = gate
    where_ref[0] = gate.T[GATE_GROUP_LANE:GATE_GROUP_LANE + 8, :].astype(jnp.int32)


def _router(x2, norm, w_group, b_group, w_router, b_router):
    T, D = x2.shape
    tm = min(ROUTER_TILE, T)
    w = jnp.zeros((D, LANES), F32)
    w = w.at[:, :N_EXPERTS].set(w_router).at[:, ROUTER_GROUP_LANE0:ROUTER_GROUP_LANE0 + N_GROUPS].set(w_group)
    w_hi = w.astype(BF16)
    w_lo = (w - w_hi.astype(F32)).astype(BF16)
    bias = jnp.zeros((1, LANES), F32)
    bias = bias.at[0, :N_EXPERTS].set(b_router).at[0, ROUTER_GROUP_LANE0:ROUTER_GROUP_LANE0 + N_GROUPS].set(b_group)
    idx = jnp.arange(tm)
    tri = (idx[None, :] < idx[:, None]).astype(BF16)
    return pl.pallas_call(
        _router_kernel,
        grid=(T // tm,),
        in_specs=[
            pl.BlockSpec((tm, D), lambda i: (i, 0)),
            pl.BlockSpec((1, D), lambda i: (0, 0)),
            pl.BlockSpec((D, LANES), lambda i: (0, 0)),
            pl.BlockSpec((D, LANES), lambda i: (0, 0)),
            pl.BlockSpec((1, LANES), lambda i: (0, 0)),
            pl.BlockSpec((tm, tm), lambda i: (0, 0)),
        ],
        out_specs=[pl.BlockSpec((tm, ROW_WIDTH), lambda i: (i, 0)),
                   pl.BlockSpec((1, 8, tm), lambda i: (i, 0, 0)),
                   pl.BlockSpec((1, LANES), lambda i: (0, 0))],
        out_shape=[jax.ShapeDtypeStruct((T, ROW_WIDTH), F32),
                   jax.ShapeDtypeStruct((T // tm, 8, tm), jnp.int32),
                   jax.ShapeDtypeStruct((1, LANES), F32)],
        scratch_shapes=[pltpu.VMEM((1, LANES), F32)],
        compiler_params=_params("arbitrary"),
        name="moe_router",
    )(x2, norm.reshape(1, D), w_hi, w_lo, bias, tri)


def _dispatch_kernel(pad_lo_ref, pad_hi_ref, dest_ref, row_ref, sorted_ref, zero_buf, sem, zero_sem):
    tm = row_ref.shape[0]

    @pl.when(pl.program_id(0) == 0)
    def _():
        zero_buf[...] = jnp.zeros_like(zero_buf)
        for seg in range(N_GROUPS + 1):
            def fill(r, carry):
                pltpu.make_async_copy(zero_buf.at[pl.ds(0, 1)], sorted_ref.at[pl.ds(r, 1)], zero_sem).start()
                return carry
            lax.fori_loop(pad_lo_ref[seg], pad_hi_ref[seg], fill, 0)
        for _ in range(MOE_PAD_ROWS // zero_buf.shape[0]):
            pltpu.make_async_copy(zero_buf, zero_buf, zero_sem).wait()

    def send(t, carry):
        pltpu.make_async_copy(row_ref.at[pl.ds(t, 1)], sorted_ref.at[pl.ds(dest_ref[t], 1)], sem).start()
        return carry

    lax.fori_loop(0, tm, send, 0, unroll=8)
    pltpu.make_async_copy(row_ref, row_ref, sem).wait()


def _dispatch(rows, dest, pad_lo, pad_hi):
    T = rows.shape[0]
    tm = min(ROUTER_TILE, T)
    return pl.pallas_call(
        _dispatch_kernel,
        grid_spec=pltpu.PrefetchScalarGridSpec(
            num_scalar_prefetch=2,
            grid=(T // tm,),
            in_specs=[pl.BlockSpec((tm,), lambda i, lo, hi: (i,), memory_space=pltpu.SMEM),
                      pl.BlockSpec((tm, ROW_WIDTH), lambda i, lo, hi: (i, 0))],
            out_specs=pl.BlockSpec(memory_space=pl.ANY),
            scratch_shapes=[pltpu.VMEM((MOE_ROW_TILE, ROW_WIDTH), F32), pltpu.SemaphoreType.DMA(()),
                            pltpu.SemaphoreType.DMA(())],
        ),
        out_shape=jax.ShapeDtypeStruct((T + MOE_PAD_ROWS, ROW_WIDTH), F32),
        compiler_params=_params("arbitrary"),
        name="moe_dispatch",
    )(pad_lo, pad_hi, dest, rows)


def _experts_kernel(tile_group_ref, n_tiles_ref, row_ref, norm_ref, wg_ref, wu_ref, wd_ref, o_ref):
    r = pl.program_id(0)

    @pl.when(r < n_tiles_ref[0])
    def _():
        x = row_ref[:, :D_MODEL]
        gate = row_ref[:, D_MODEL:]
        h = _rms(x, norm_ref[...]).astype(BF16)
        lane = lax.broadcasted_iota(jnp.int32, gate.shape, 1)
        e0 = tile_group_ref[r] * EXPERTS_PER_GROUP
        acc = x
        for e in range(EXPERTS_PER_GROUP):
            a = _dot(h, wg_ref[e])
            u = _dot(h, wu_ref[e])
            gcol = jnp.sum(jnp.where(lane == e0 + e, gate, 0.0), axis=1, keepdims=True)
            hid = (a * jax.nn.sigmoid(a)) * u * gcol
            acc = acc + _dot(hid.astype(BF16), wd_ref[e])
        o_ref[...] = acc

    @pl.when(r >= n_tiles_ref[0])
    def _():
        o_ref[...] = jnp.zeros_like(o_ref)


def _experts(sorted_rows, tile_group, n_tiles, norm, w_gate, w_up, w_down):
    P = sorted_rows.shape[0]
    D = D_MODEL
    group_w = lambda r, tg, nt: (tg[r], 0, 0)
    return pl.pallas_call(
        _experts_kernel,
        grid_spec=pltpu.PrefetchScalarGridSpec(
            num_scalar_prefetch=2,
            grid=(P // MOE_ROW_TILE,),
            in_specs=[pl.BlockSpec((MOE_ROW_TILE, ROW_WIDTH), lambda r, tg, nt: (r, 0)),
                      pl.BlockSpec((1, D), lambda r, tg, nt: (0, 0)),
                      pl.BlockSpec((EXPERTS_PER_GROUP, D, D_EXPERT), group_w),
                      pl.BlockSpec((EXPERTS_PER_GROUP, D, D_EXPERT), group_w),
                      pl.BlockSpec((EXPERTS_PER_GROUP, D_EXPERT, D), group_w)],
            out_specs=pl.BlockSpec((MOE_ROW_TILE, D), lambda r, tg, nt: (r, 0)),
        ),
        out_shape=jax.ShapeDtypeStruct((P, D), F32),
        compiler_params=_params("arbitrary"),
        name="moe_experts",
    )(tile_group, n_tiles, sorted_rows, norm.reshape(1, D), w_gate.astype(BF16), w_up.astype(BF16),
      w_down.astype(BF16))


def _combine_kernel(dest_ref, sorted_ref, o_ref, sem):
    tm = o_ref.shape[0]

    def fetch(t, carry):
        pltpu.make_async_copy(sorted_ref.at[pl.ds(dest_ref[t], 1)], o_ref.at[pl.ds(t, 1)], sem).start()
        return carry

    lax.fori_loop(0, tm, fetch, 0, unroll=8)
    pltpu.make_async_copy(o_ref, o_ref, sem).wait()


def _combine(sorted_out, dest, T):
    D = sorted_out.shape[1]
    tm = min(ROUTER_TILE, T)
    return pl.pallas_call(
        _combine_kernel,
        grid=(T // tm,),
        in_specs=[pl.BlockSpec((tm,), lambda i: (i,), memory_space=pltpu.SMEM),
                  pl.BlockSpec(memory_space=pl.ANY)],
        out_specs=pl.BlockSpec((tm, D), lambda i: (i, 0)),
        out_shape=jax.ShapeDtypeStruct((T, D), F32),
        scratch_shapes=[pltpu.SemaphoreType.DMA(())],
        compiler_params=_params("arbitrary"),
        name="moe_combine",
    )(dest, sorted_out)


def _moe_layer(x, norm, w_group, b_group, w_router, b_router, w_gate, w_up, w_down):
    B, S, D = x.shape
    T = B * S
    rows, where, counts = _router(x.reshape(T, D), norm, w_group, b_group, w_router, b_router)
    where = where.transpose(1, 0, 2).reshape(8, T)
    group, rank = where[0], where[1]
    counts = counts[0, :N_GROUPS].astype(jnp.int32)
    tiles = (counts + MOE_ROW_TILE - 1) // MOE_ROW_TILE
    tile_end = jnp.cumsum(tiles)
    offsets = (tile_end - tiles) * MOE_ROW_TILE
    dest = offsets[group] + rank
    total_rows = T + MOE_PAD_ROWS
    pad_lo = jnp.concatenate([offsets + counts, tile_end[-1:] * MOE_ROW_TILE]).astype(jnp.int32)
    pad_hi = jnp.concatenate([tile_end * MOE_ROW_TILE, jnp.full((1,), total_rows)]).astype(jnp.int32)
    tile_idx = jnp.arange(total_rows // MOE_ROW_TILE)
    tile_group = jnp.minimum(jnp.sum(tile_idx[:, None] >= tile_end[None, :], axis=1), N_GROUPS - 1)
    sorted_rows = _dispatch(rows, dest, pad_lo, pad_hi)
    sorted_out = _experts(sorted_rows, tile_group.astype(jnp.int32), tile_end[-1:].astype(jnp.int32), norm,
                          w_gate, w_up, w_down)
    return _combine(sorted_out, dest, T).reshape(B, S, D)


def _qkv_kernel(x_ref, norm_ref, w_ref, seg_ref, segt_ref, qg_ref, kg_ref, q_ref, k_ref, v_ref):
    D = x_ref.shape[1]
    h = _rms(x_ref[...], norm_ref[...]).astype(BF16)
    qkv = _dot(h, w_ref[...])
    seg = seg_ref[...]
    segt = segt_ref[...]

    def head_norm(t, gain):
        ms = _dot((t * t).astype(BF16), seg) * (1.0 / SB_HEAD_DIM)
        r_hi, r_lo = _split_bf16(lax.rsqrt(ms + RMS_EPS))
        return t * (_dot(r_hi, segt) + _dot(r_lo, segt)) * gain

    q_ref[...] = head_norm(qkv[:, :D], qg_ref[...]).astype(BF16)
    k_ref[...] = head_norm(qkv[:, D:2 * D], kg_ref[...]).astype(BF16)
    v_ref[...] = qkv[:, 2 * D:].astype(BF16)


def _qkv(x2, norm, w_qkv, q_gain, k_gain):
    T, D = x2.shape
    tm = min(QKV_TILE, T)
    head_of = jnp.arange(D) // SB_HEAD_DIM
    seg = (head_of[:, None] == jnp.arange(LANES)[None, :]).astype(BF16)
    segt = seg.T
    qg = (jnp.tile(q_gain, SB_HEADS) * (SB_HEAD_DIM ** -0.5 * LOG2_E)).reshape(1, D)
    kg = jnp.tile(k_gain, SB_HEADS).reshape(1, D)
    row = pl.BlockSpec((tm, D), lambda i: (i, 0))
    vec = pl.BlockSpec((1, D), lambda i: (0, 0))
    return pl.pallas_call(
        _qkv_kernel,
        grid=(T // tm,),
        in_specs=[row, vec, pl.BlockSpec((D, 3 * D), lambda i: (0, 0)),
                  pl.BlockSpec((D, LANES), lambda i: (0, 0)), pl.BlockSpec((LANES, D), lambda i: (0, 0)),
                  vec, vec],
        out_specs=[row, row, row],
        out_shape=[jax.ShapeDtypeStruct((T, D), BF16)] * 3,
        compiler_params=_params("parallel"),
        name="sb_qkv",
    )(x2, norm.reshape(1, D), w_qkv.astype(BF16), seg, segt, qg, kg)


ATTN_DIAG, ATTN_PASS, ATTN_NONE = 0, 1, 2


def _attn_kernel(q_ref, k_ref, v_ref, u_ref, ceil_ref, bias_ref, o_ref, qh_buf, acc_ref, rem_ref, *, t):
    qi = pl.program_id(2)
    q2 = q_ref[...]
    lane = lax.broadcasted_iota(jnp.int32, (t, LANES), 1)
    first_head = lane < SB_HEAD_DIM
    zero = jnp.zeros_like(q2)
    qh_buf[0] = jnp.where(first_head, q2, zero)
    qh_buf[1] = jnp.where(first_head, zero, q2)

    def key_tile(j):
        start = pl.multiple_of(jnp.maximum(qi - j, 0) * t, t)
        return k_ref[0, pl.ds(start, t), :], v_ref[0, pl.ds(start, t), :]

    def scores(hd, kt):
        return lax.dot_general(qh_buf[hd], kt, (((1,), (1,)), ((), ())), preferred_element_type=F32)

    def drop_of(z, ceil=None):
        drop = jnp.maximum(z, 0.0) + jnp.log2(1.0 + jnp.exp(jnp.abs(z) * (-LN_2)))
        log_beta = z - drop
        if ceil is not None:
            drop = jnp.minimum(drop, ceil)
        return drop, log_beta

    def suffix_sums(drop):
        d_hi, d_lo = _split_bf16(drop)
        return _dot(jnp.concatenate([d_hi, d_lo], axis=1), u_ref[...])

    def weights(log_beta, later, bias=None):
        arg = log_beta - later
        if bias is not None:
            arg = arg + bias
        return jnp.exp2(arg).astype(BF16)

    kinds = (ATTN_DIAG, jnp.where(qi >= 1, ATTN_PASS, ATTN_NONE))
    kv = [key_tile(0), key_tile(1)]
    z = [[scores(hd, kv[j][0]) for j in range(2)] for hd in range(2)]
    dl = [[drop_of(z[hd][j], ceil_ref[kinds[j]]) for j in range(2)] for hd in range(2)]
    c = [[suffix_sums(dl[hd][j][0]) for j in range(2)] for hd in range(2)]
    pv = []
    for hd in range(2):
        (d0, lb0), (d1, lb1) = dl[hd]
        rem1 = c[hd][0][:, 0:1] + d0[:, 0:1]
        a0 = weights(lb0, c[hd][0], bias_ref[kinds[0]])
        a1 = weights(lb1, c[hd][1] + rem1, bias_ref[kinds[1]])
        pv.append(_dot(a0, kv[0][1]) + _dot(a1, kv[1][1]))
        rem_ref[hd] = rem1 + (c[hd][1][:, 0:1] + d1[:, 0:1])
    acc_ref[...] = jnp.where(first_head, pv[0], pv[1])

    def rem_min():
        return jnp.min(jnp.minimum(rem_ref[0], rem_ref[1]))

    def more(carry):
        j, smallest_rem = carry
        return (j <= qi) & (smallest_rem < ZERO_WEIGHT_LOG2)

    def sweep(carry):
        j, _ = carry
        kt, vt = key_tile(j)
        pv = []
        for hd in range(2):
            drop, log_beta = drop_of(scores(hd, kt))
            c = suffix_sums(drop)
            rem = rem_ref[hd]
            pv.append(_dot(weights(log_beta, c + rem), vt))
            rem_ref[hd] = rem + (c[:, 0:1] + drop[:, 0:1])
        acc_ref[...] += jnp.where(first_head, pv[0], pv[1])
        return j + 1, rem_min()

    lax.while_loop(more, sweep, (jnp.int32(2), rem_min()))
    o_ref[...] = acc_ref[...].astype(o_ref.dtype)


def _attention(q, k, v, B, S):
    T, D = q.shape
    t = min(ATTN_TILE, S)
    nq = S // t
    idx = jnp.arange(t)
    u = (idx[:, None] > idx[None, :]).astype(BF16)
    u2 = jnp.concatenate([u, u], axis=0)
    causal = idx[None, :] < idx[:, None]
    allowed = jnp.stack([causal, jnp.ones_like(causal), jnp.zeros_like(causal)])
    ceil = jnp.where(allowed, jnp.inf, 0.0).astype(F32)
    bias = jnp.where(allowed, 0.0, -jnp.inf).astype(F32)
    k3 = k.reshape(B, S, D)
    v3 = v.reshape(B, S, D)
    kv_spec = pl.BlockSpec((1, S, LANES), lambda b, hp, i: (b, 0, hp))
    table_spec = pl.BlockSpec((3, t, t), lambda b, hp, i: (0, 0, 0))
    return pl.pallas_call(
        functools.partial(_attn_kernel, t=t),
        grid=(B, D // LANES, nq),
        in_specs=[pl.BlockSpec((t, LANES), lambda b, hp, i: (b * nq + i, hp)), kv_spec, kv_spec,
                  pl.BlockSpec((2 * t, t), lambda b, hp, i: (0, 0)), table_spec, table_spec],
        out_specs=pl.BlockSpec((t, LANES), lambda b, hp, i: (b * nq + i, hp)),
        out_shape=jax.ShapeDtypeStruct((T, D), BF16),
        scratch_shapes=[pltpu.VMEM((2, t, LANES), BF16), pltpu.VMEM((t, LANES), F32),
                        pltpu.VMEM((2, t, 1), F32)],
        compiler_params=_params("parallel", "parallel", "arbitrary"),
        name="sb_attention",
    )(q, k3, v3, u2, ceil, bias)


def _oproj_kernel(x_ref, o_ref, w_ref, out_ref):
    out_ref[...] = x_ref[...] + _dot(o_ref[...], w_ref[...])


def _oproj(x2, o, w_o):
    T, D = x2.shape
    tm = min(OPROJ_TILE, T)
    row = pl.BlockSpec((tm, D), lambda i: (i, 0))
    return pl.pallas_call(
        _oproj_kernel,
        grid=(T // tm,),
        in_specs=[row, row, pl.BlockSpec((D, D), lambda i: (0, 0))],
        out_specs=row,
        out_shape=jax.ShapeDtypeStruct((T, D), F32),
        compiler_params=_params("parallel"),
        name="sb_oproj",
    )(x2, o, w_o.astype(BF16))


def _sb_layer(x, norm, w_qkv, q_gain, k_gain, w_o):
    B, S, D = x.shape
    x2 = x.reshape(B * S, D)
    q, k, v = _qkv(x2, norm, w_qkv, q_gain, k_gain)
    o = _attention(q, k, v, B, S)
    return _oproj(x2, o, w_o).reshape(B, S, D)


def kernel(x, pool_norm, pool_w, pool_b, pool_scale, sb_norm, sb_w_qkv, sb_q_gain, sb_k_gain, sb_w_o, moe_norm, moe_w_group, moe_b_group, moe_w_router, moe_b_router, moe_w_gate, moe_w_up, moe_w_down):
    for i in range(DEPTH):
        j = i // 2
        if i % 2 == 0:
            x = _pool_layer(x, pool_norm[j], pool_w[j], pool_b[j], pool_scale[j])
        else:
            x = _sb_layer(x, sb_norm[j], sb_w_qkv[j], sb_q_gain[j], sb_k_gain[j], sb_w_o[j])
        x = _moe_layer(x, moe_norm[i], moe_w_group[i], moe_b_group[i], moe_w_router[i], moe_b_router[i],
                       moe_w_gate[i], moe_w_up[i], moe_w_down[i])
    return x
```

```python
import functools

import jax
import jax.numpy as jnp
from jax import lax
from jax.experimental import pallas as pl
from jax.experimental.pallas import tpu as pltpu

F32 = jnp.float32
BF16 = jnp.bfloat16

D_MODEL = 1024
DEPTH = 4
POOL_WINDOWS = (2, 4, 8, 16)
POOL_GROUP_DIM = D_MODEL // len(POOL_WINDOWS)
POOL_HALO = 16
SB_HEADS = 16
SB_HEAD_DIM = D_MODEL // SB_HEADS
N_GROUPS = 4
EXPERTS_PER_GROUP = 8
N_EXPERTS = N_GROUPS * EXPERTS_PER_GROUP
D_EXPERT = D_MODEL // 4
RMS_EPS = 1e-6
LOG2_E = 1.4426950408889634
LN_2 = 0.6931471805599453
ZERO_WEIGHT_LOG2 = 150.0

LANES = 128
DMA_PRIORITIES = 2
VMEM_LIMIT_BYTES = 56 * 1024 * 1024

POOL_TILE = 512
ROUTER_TILE = 512
QKV_TILE = 512
ATTN_TILE = 256
OPROJ_TILE = 512


def _params(*sem):
    return pltpu.CompilerParams(dimension_semantics=sem, vmem_limit_bytes=VMEM_LIMIT_BYTES)


def _rms(v, gain):
    return v * lax.rsqrt(jnp.mean(v * v, axis=-1, keepdims=True) + RMS_EPS) * gain


def _dot(a, b):
    return jnp.dot(a, b, preferred_element_type=F32)


def _split_bf16(v):
    hi = v.astype(BF16)
    lo = (v - hi.astype(F32)).astype(BF16)
    return hi, lo


def _pool_kernel(xprev_ref, x_ref, norm_ref, w_ref, b_ref, scale_ref, o_ref, hbuf):
    i = pl.program_id(1)
    ts = x_ref.shape[1]
    x = x_ref[0]
    gain = norm_ref[...]
    h = _rms(x, gain)
    hprev = jnp.where(i > 0, _rms(xprev_ref[0], gain), 0.0)
    hbuf[0:POOL_HALO, :] = hprev
    hbuf[POOL_HALO:POOL_HALO + ts, :] = h
    pos = i * ts + lax.broadcasted_iota(jnp.int32, (ts, 1), 0)
    for g, win in enumerate(POOL_WINDOWS):
        c0 = g * POOL_GROUP_DIM
        cols = slice(c0, c0 + POOL_GROUP_DIM)
        hg = h[:, cols]
        acc = hg
        for j in range(1, win):
            acc = acc + hbuf[POOL_HALO - j:POOL_HALO - j + ts, cols]
        count = jnp.minimum(pos + 1, win).astype(F32)
        diff = acc / count - hg
        y = _dot(diff.astype(BF16), w_ref[g]) + b_ref[g]
        o_ref[0, :, cols] = x[:, cols] + y * scale_ref[:, cols]


def _pool_layer(x, norm, w, b, scale):
    B, S, D = x.shape
    ts = min(POOL_TILE, S)
    halo_blocks = ts // POOL_HALO
    return pl.pallas_call(
        _pool_kernel,
        grid=(B, S // ts),
        in_specs=[
            pl.BlockSpec((1, POOL_HALO, D), lambda bi, i: (bi, jnp.maximum(i * halo_blocks - 1, 0), 0)),
            pl.BlockSpec((1, ts, D), lambda bi, i: (bi, i, 0)),
            pl.BlockSpec((1, D), lambda bi, i: (0, 0)),
            pl.BlockSpec((len(POOL_WINDOWS), POOL_GROUP_DIM, POOL_GROUP_DIM), lambda bi, i: (0, 0, 0)),
            pl.BlockSpec((len(POOL_WINDOWS), 1, POOL_GROUP_DIM), lambda bi, i: (0, 0, 0)),
            pl.BlockSpec((1, D), lambda bi, i: (0, 0)),
        ],
        out_specs=pl.BlockSpec((1, ts, D), lambda bi, i: (bi, i, 0)),
        out_shape=jax.ShapeDtypeStruct((B, S, D), F32),
        scratch_shapes=[pltpu.VMEM((POOL_HALO + ts, D), F32)],
        compiler_params=_params("parallel", "parallel"),
        name="pool_layer",
    )(x, x, norm.reshape(1, D), w.astype(BF16), b.reshape(len(POOL_WINDOWS), 1, POOL_GROUP_DIM),
      scale.reshape(1, D))


ROUTER_GROUP_LANE0 = N_EXPERTS
GATE_GROUP_LANE = 40
GATE_RANK_LANE = 41
ROW_WIDTH = D_MODEL + LANES
MOE_ROW_TILE = 256
MOE_PAD_ROWS = N_GROUPS * MOE_ROW_TILE


def _router_kernel(x_ref, norm_ref, whi_ref, wlo_ref, bias_ref, tri_ref, row_ref, where_ref, count_ref,
                   run_ref):
    @pl.when(pl.program_id(0) == 0)
    def _():
        run_ref[...] = jnp.zeros_like(run_ref)

    x = x_ref[...]
    h = _rms(x, norm_ref[...])
    h_hi, h_lo = _split_bf16(h)
    w_hi = whi_ref[...]
    logits = _dot(h_hi, w_hi) + _dot(h_lo, w_hi) + _dot(h_hi, wlo_ref[...]) + bias_ref[...]

    lane = lax.broadcasted_iota(jnp.int32, logits.shape, 1).astype(F32)
    neg_inf = F32(-jnp.inf)
    no_lane = F32(4 * LANES)

    def first_argmax(v):
        m = jnp.max(v, axis=1, keepdims=True)
        idx = jnp.min(jnp.where(v == m, lane, no_lane), axis=1, keepdims=True)
        return m, idx

    is_group = (lane >= ROUTER_GROUP_LANE0) & (lane < ROUTER_GROUP_LANE0 + N_GROUPS)
    gl = jnp.where(is_group, logits, neg_inf)
    gmax, glane = first_argmax(gl)
    g_prob = 1.0 / jnp.sum(jnp.exp(gl - gmax), axis=1, keepdims=True)
    group = glane - ROUTER_GROUP_LANE0
    e0 = group * EXPERTS_PER_GROUP
    el = jnp.where((lane >= e0) & (lane < e0 + EXPERTS_PER_GROUP), logits, neg_inf)
    m1, i1 = first_argmax(el)
    m2, i2 = first_argmax(jnp.where(lane == i1, neg_inf, el))
    r = jnp.exp(m2 - m1)
    w1 = 1.0 / (1.0 + r)
    gate = jnp.where(lane == i1, g_prob * w1, 0.0) + jnp.where(lane == i2, g_prob * (r * w1), 0.0)

    in_group = jnp.where(lane == group, 1.0, 0.0)
    earlier = _dot(tri_ref[...], in_group.astype(BF16))
    run = run_ref[...]
    rank = jnp.sum(in_group * (earlier + run), axis=1, keepdims=True)
    run = run + jnp.sum(in_group, axis=0, keepdims=True)
    run_ref[...] = run
    count_ref[...] = run

    gate = jnp.where(lane == GATE_GROUP_LANE, group, jnp.where(lane == GATE_RANK_LANE, rank, gate))
    row_ref[:, :D_MODEL] = x
    row_ref[:, D_MODEL:] = gate
    where_ref[0] = gate.T[GATE_GROUP_LANE:GATE_GROUP_LANE + 8, :].astype(jnp.int32)


def _router(x2, norm, w_group, b_group, w_router, b_router):
    T, D = x2.shape
    tm = min(ROUTER_TILE, T)
    w = jnp.zeros((D, LANES), F32)
    w = w.at[:, :N_EXPERTS].set(w_router).at[:, ROUTER_GROUP_LANE0:ROUTER_GROUP_LANE0 + N_GROUPS].set(w_group)
    w_hi = w.astype(BF16)
    w_lo = (w - w_hi.astype(F32)).astype(BF16)
    bias = jnp.zeros((1, LANES), F32)
    bias = bias.at[0, :N_EXPERTS].set(b_router).at[0, ROUTER_GROUP_LANE0:ROUTER_GROUP_LANE0 + N_GROUPS].set(b_group)
    idx = jnp.arange(tm)
    tri = (idx[None, :] < idx[:, None]).astype(BF16)
    return pl.pallas_call(
        _router_kernel,
        grid=(T // tm,),
        in_specs=[
            pl.BlockSpec((tm, D), lambda i: (i, 0)),
            pl.BlockSpec((1, D), lambda i: (0, 0)),
            pl.BlockSpec((D, LANES), lambda i: (0, 0)),
            pl.BlockSpec((D, LANES), lambda i: (0, 0)),
            pl.BlockSpec((1, LANES), lambda i: (0, 0)),
            pl.BlockSpec((tm, tm), lambda i: (0, 0)),
        ],
        out_specs=[pl.BlockSpec((tm, ROW_WIDTH), lambda i: (i, 0)),
                   pl.BlockSpec((1, 8, tm), lambda i: (i, 0, 0)),
                   pl.BlockSpec((1, LANES), lambda i: (0, 0))],
        out_shape=[jax.ShapeDtypeStruct((T, ROW_WIDTH), F32),
                   jax.ShapeDtypeStruct((T // tm, 8, tm), jnp.int32),
                   jax.ShapeDtypeStruct((1, LANES), F32)],
        scratch_shapes=[pltpu.VMEM((1, LANES), F32)],
        compiler_params=_params("arbitrary"),
        name="moe_router",
    )(x2, norm.reshape(1, D), w_hi, w_lo, bias, tri)


def _dispatch_kernel(pad_lo_ref, pad_hi_ref, dest_ref, row_ref, sorted_ref, zero_buf, sem, zero_sem):
    tm = row_ref.shape[0]

    @pl.when(pl.program_id(0) == 0)
    def _():
        zero_buf[...] = jnp.zeros_like(zero_buf)
        for seg in range(N_GROUPS + 1):
            def fill(r, carry):
                pltpu.make_async_copy(zero_buf.at[pl.ds(0, 1)], sorted_ref.at[pl.ds(r, 1)], zero_sem).start()
                return carry
            lax.fori_loop(pad_lo_ref[seg], pad_hi_ref[seg], fill, 0)
        for _ in range(MOE_PAD_ROWS // zero_buf.shape[0]):
            pltpu.make_async_copy(zero_buf, zero_buf, zero_sem).wait()

    for t in range(tm):
        pltpu.make_async_copy(row_ref.at[pl.ds(t, 1)], sorted_ref.at[pl.ds(dest_ref[t], 1)], sem).start(
            priority=t % DMA_PRIORITIES)
    pltpu.make_async_copy(row_ref, row_ref, sem).wait()


def _dispatch(rows, dest, pad_lo, pad_hi):
    T = rows.shape[0]
    tm = min(ROUTER_TILE, T)
    return pl.pallas_call(
        _dispatch_kernel,
        grid_spec=pltpu.PrefetchScalarGridSpec(
            num_scalar_prefetch=2,
            grid=(T // tm,),
            in_specs=[pl.BlockSpec((tm,), lambda i, lo, hi: (i,), memory_space=pltpu.SMEM),
                      pl.BlockSpec((tm, ROW_WIDTH), lambda i, lo, hi: (i, 0))],
            out_specs=pl.BlockSpec(memory_space=pl.ANY),
            scratch_shapes=[pltpu.VMEM((MOE_ROW_TILE, ROW_WIDTH), F32), pltpu.SemaphoreType.DMA(()),
                            pltpu.SemaphoreType.DMA(())],
        ),
        out_shape=jax.ShapeDtypeStruct((T + MOE_PAD_ROWS, ROW_WIDTH), F32),
        compiler_params=_params("arbitrary"),
        name="moe_dispatch",
    )(pad_lo, pad_hi, dest, rows)


def _experts_kernel(tile_group_ref, n_tiles_ref, row_ref, norm_ref, wg_ref, wu_ref, wd_ref, o_ref):
    r = pl.program_id(0)

    @pl.when(r < n_tiles_ref[0])
    def _():
        x = row_ref[:, :D_MODEL]
        gate = row_ref[:, D_MODEL:]
        h = _rms(x, norm_ref[...]).astype(BF16)
        lane = lax.broadcasted_iota(jnp.int32, gate.shape, 1)
        e0 = tile_group_ref[r] * EXPERTS_PER_GROUP
        acc = x
        for e in range(EXPERTS_PER_GROUP):
            a = _dot(h, wg_ref[e])
            u = _dot(h, wu_ref[e])
            gcol = jnp.sum(jnp.where(lane == e0 + e, gate, 0.0), axis=1, keepdims=True)
            hid = (a * jax.nn.sigmoid(a)) * u * gcol
            acc = acc + _dot(hid.astype(BF16), wd_ref[e])
        o_ref[...] = acc

    @pl.when(r >= n_tiles_ref[0])
    def _():
        o_ref[...] = jnp.zeros_like(o_ref)


def _experts(sorted_rows, tile_group, n_tiles, norm, w_gate, w_up, w_down):
    P = sorted_rows.shape[0]
    D = D_MODEL
    group_w = lambda r, tg, nt: (tg[r], 0, 0)
    return pl.pallas_call(
        _experts_kernel,
        grid_spec=pltpu.PrefetchScalarGridSpec(
            num_scalar_prefetch=2,
            grid=(P // MOE_ROW_TILE,),
            in_specs=[pl.BlockSpec((MOE_ROW_TILE, ROW_WIDTH), lambda r, tg, nt: (r, 0)),
                      pl.BlockSpec((1, D), lambda r, tg, nt: (0, 0)),
                      pl.BlockSpec((EXPERTS_PER_GROUP, D, D_EXPERT), group_w),
                      pl.BlockSpec((EXPERTS_PER_GROUP, D, D_EXPERT), group_w),
                      pl.BlockSpec((EXPERTS_PER_GROUP, D_EXPERT, D), group_w)],
            out_specs=pl.BlockSpec((MOE_ROW_TILE, D), lambda r, tg, nt: (r, 0)),
        ),
        out_shape=jax.ShapeDtypeStruct((P, D), F32),
        compiler_params=_params("arbitrary"),
        name="moe_experts",
    )(tile_group, n_tiles, sorted_rows, norm.reshape(1, D), w_gate.astype(BF16), w_up.astype(BF16),
      w_down.astype(BF16))


def _combine_kernel(dest_ref, sorted_ref, o_ref, sem):
    tm = o_ref.shape[0]

    for t in range(tm):
        pltpu.make_async_copy(sorted_ref.at[pl.ds(dest_ref[t], 1)], o_ref.at[pl.ds(t, 1)], sem).start(
            priority=t % DMA_PRIORITIES)
    pltpu.make_async_copy(o_ref, o_ref, sem).wait()


def _combine(sorted_out, dest, T):
    D = sorted_out.shape[1]
    tm = min(ROUTER_TILE, T)
    return pl.pallas_call(
        _combine_kernel,
        grid=(T // tm,),
        in_specs=[pl.BlockSpec((tm,), lambda i: (i,), memory_space=pltpu.SMEM),
                  pl.BlockSpec(memory_space=pl.ANY)],
        out_specs=pl.BlockSpec((tm, D), lambda i: (i, 0)),
        out_shape=jax.ShapeDtypeStruct((T, D), F32),
        scratch_shapes=[pltpu.SemaphoreType.DMA(())],
        compiler_params=_params("arbitrary"),
        name="moe_combine",
    )(dest, sorted_out)


def _moe_layer(x, norm, w_group, b_group, w_router, b_router, w_gate, w_up, w_down):
    B, S, D = x.shape
    T = B * S
    rows, where, counts = _router(x.reshape(T, D), norm, w_group, b_group, w_router, b_router)
    where = where.transpose(1, 0, 2).reshape(8, T)
    group, rank = where[0], where[1]
    counts = counts[0, :N_GROUPS].astype(jnp.int32)
    tiles = (counts + MOE_ROW_TILE - 1) // MOE_ROW_TILE
    tile_end = jnp.cumsum(tiles)
    offsets = (tile_end - tiles) * MOE_ROW_TILE
    dest = offsets[group] + rank
    total_rows = T + MOE_PAD_ROWS
    pad_lo = jnp.concatenate([offsets + counts, tile_end[-1:] * MOE_ROW_TILE]).astype(jnp.int32)
    pad_hi = jnp.concatenate([tile_end * MOE_ROW_TILE, jnp.full((1,), total_rows)]).astype(jnp.int32)
    tile_idx = jnp.arange(total_rows // MOE_ROW_TILE)
    tile_group = jnp.minimum(jnp.sum(tile_idx[:, None] >= tile_end[None, :], axis=1), N_GROUPS - 1)
    sorted_rows = _dispatch(rows, dest, pad_lo, pad_hi)
    sorted_out = _experts(sorted_rows, tile_group.astype(jnp.int32), tile_end[-1:].astype(jnp.int32), norm,
                          w_gate, w_up, w_down)
    return _combine(sorted_out, dest, T).reshape(B, S, D)


def _qkv_kernel(x_ref, norm_ref, w_ref, seg_ref, segt_ref, qg_ref, kg_ref, q_ref, k_ref, v_ref):
    D = x_ref.shape[1]
    h = _rms(x_ref[...], norm_ref[...]).astype(BF16)
    qkv = _dot(h, w_ref[...])
    seg = seg_ref[...]
    segt = segt_ref[...]

    def head_norm(t, gain):
        ms = _dot((t * t).astype(BF16), seg) * (1.0 / SB_HEAD_DIM)
        r_hi, r_lo = _split_bf16(lax.rsqrt(ms + RMS_EPS))
        return t * (_dot(r_hi, segt) + _dot(r_lo, segt)) * gain

    q_ref[...] = head_norm(qkv[:, :D], qg_ref[...]).astype(BF16)
    k_ref[...] = head_norm(qkv[:, D:2 * D], kg_ref[...]).astype(BF16)
    v_ref[...] = qkv[:, 2 * D:].astype(BF16)


def _qkv(x2, norm, w_qkv, q_gain, k_gain):
    T, D = x2.shape
    tm = min(QKV_TILE, T)
    head_of = jnp.arange(D) // SB_HEAD_DIM
    seg = (head_of[:, None] == jnp.arange(LANES)[None, :]).astype(BF16)
    segt = seg.T
    qg = (jnp.tile(q_gain, SB_HEADS) * (SB_HEAD_DIM ** -0.5 * LOG2_E)).reshape(1, D)
    kg = jnp.tile(k_gain, SB_HEADS).reshape(1, D)
    row = pl.BlockSpec((tm, D), lambda i: (i, 0))
    vec = pl.BlockSpec((1, D), lambda i: (0, 0))
    return pl.pallas_call(
        _qkv_kernel,
        grid=(T // tm,),
        in_specs=[row, vec, pl.BlockSpec((D, 3 * D), lambda i: (0, 0)),
                  pl.BlockSpec((D, LANES), lambda i: (0, 0)), pl.BlockSpec((LANES, D), lambda i: (0, 0)),
                  vec, vec],
        out_specs=[row, row, row],
        out_shape=[jax.ShapeDtypeStruct((T, D), BF16)] * 3,
        compiler_params=_params("parallel"),
        name="sb_qkv",
    )(x2, norm.reshape(1, D), w_qkv.astype(BF16), seg, segt, qg, kg)


def _attn_kernel(q_ref, k_ref, v_ref, u_ref, ceil_ref, bias_ref, o_ref, qh_buf, acc_ref, rem_ref, *, t):
    qi = pl.program_id(2)
    q2 = q_ref[...]
    lane = lax.broadcasted_iota(jnp.int32, (t, LANES), 1)
    first_head = lane < SB_HEAD_DIM
    zero = jnp.zeros_like(q2)
    qh_buf[0] = jnp.where(first_head, q2, zero)
    qh_buf[1] = jnp.where(first_head, zero, q2)

    def key_tile(j):
        start = pl.multiple_of(jnp.maximum(qi - j, 0) * t, t)
        return k_ref[0, pl.ds(start, t), :], v_ref[0, pl.ds(start, t), :]

    def scores(hd, kt):
        return lax.dot_general(qh_buf[hd], kt, (((1,), (1,)), ((), ())), preferred_element_type=F32)

    def drop_of(z, ceil=None):
        drop = jnp.maximum(z, 0.0) + jnp.log2(1.0 + jnp.exp(jnp.abs(z) * (-LN_2)))
        log_beta = z - drop
        if ceil is not None:
            drop = jnp.minimum(drop, ceil)
        return drop, log_beta

    def suffix_sums(drop):
        return _dot(drop.astype(BF16), u_ref[...])

    def weights(log_beta, later, bias=None):
        arg = log_beta - later
        if bias is not None:
            arg = arg + bias
        return jnp.exp2(arg).astype(BF16)

    (k0, v0), (k1, v1) = key_tile(0), key_tile(1)
    v1 = jnp.where(qi >= 1, v1, jnp.zeros_like(v1))
    v01 = jnp.concatenate([v0, v1], axis=0)
    z = [[scores(hd, kt) for kt in (k0, k1)] for hd in range(2)]
    dl = [[drop_of(z[hd][0], ceil_ref[...]), drop_of(z[hd][1])] for hd in range(2)]
    c = [[suffix_sums(dl[hd][j][0]) for j in range(2)] for hd in range(2)]
    pv = []
    for hd in range(2):
        (d0, lb0), (d1, lb1) = dl[hd]
        rem1 = c[hd][0][:, 0:1] + d0[:, 0:1]
        a0 = weights(lb0, c[hd][0], bias_ref[...])
        a1 = weights(lb1, c[hd][1] + rem1)
        pv.append(_dot(jnp.concatenate([a0, a1], axis=1), v01))
        rem_ref[hd] = rem1 + (c[hd][1][:, 0:1] + d1[:, 0:1])
    acc_ref[...] = jnp.where(first_head, pv[0], pv[1])

    def rem_min():
        return jnp.min(jnp.minimum(rem_ref[0], rem_ref[1]))

    def more(carry):
        j, smallest_rem = carry
        return (j <= qi) & (smallest_rem < ZERO_WEIGHT_LOG2)

    def sweep(carry):
        j, _ = carry
        kt, vt = key_tile(j)
        pv = []
        for hd in range(2):
            drop, log_beta = drop_of(scores(hd, kt))
            c = suffix_sums(drop)
            rem = rem_ref[hd]
            pv.append(_dot(weights(log_beta, c + rem), vt))
            rem_ref[hd] = rem + (c[:, 0:1] + drop[:, 0:1])
        acc_ref[...] += jnp.where(first_head, pv[0], pv[1])
        return j + 1, rem_min()

    lax.while_loop(more, sweep, (jnp.int32(2), rem_min()))
    o_ref[...] = acc_ref[...].astype(o_ref.dtype)


def _attention(q, k, v, B, S):
    T, D = q.shape
    t = min(ATTN_TILE, S)
    nq = S // t
    idx = jnp.arange(t)
    u = (idx[:, None] > idx[None, :]).astype(BF16)
    causal = idx[None, :] < idx[:, None]
    ceil = jnp.where(causal, jnp.inf, 0.0).astype(F32)
    bias = jnp.where(causal, 0.0, -jnp.inf).astype(F32)
    k3 = k.reshape(B, S, D)
    v3 = v.reshape(B, S, D)
    kv_spec = pl.BlockSpec((1, S, LANES), lambda b, hp, i: (b, 0, hp))
    table_spec = pl.BlockSpec((t, t), lambda b, hp, i: (0, 0))
    return pl.pallas_call(
        functools.partial(_attn_kernel, t=t),
        grid=(B, D // LANES, nq),
        in_specs=[pl.BlockSpec((t, LANES), lambda b, hp, i: (b * nq + i, hp)), kv_spec, kv_spec,
                  table_spec, table_spec, table_spec],
        out_specs=pl.BlockSpec((t, LANES), lambda b, hp, i: (b * nq + i, hp)),
        out_shape=jax.ShapeDtypeStruct((T, D), BF16),
        scratch_shapes=[pltpu.VMEM((2, t, LANES), BF16), pltpu.VMEM((t, LANES), F32),
                        pltpu.VMEM((2, t, 1), F32)],
        compiler_params=_params("parallel", "parallel", "arbitrary"),
        name="sb_attention",
    )(q, k3, v3, u, ceil, bias)


def _oproj_kernel(x_ref, o_ref, w_ref, out_ref):
    out_ref[...] = x_ref[...] + _dot(o_ref[...], w_ref[...])


def _oproj(x2, o, w_o):
    T, D = x2.shape
    tm = min(OPROJ_TILE, T)
    row = pl.BlockSpec((tm, D), lambda i: (i, 0))
    return pl.pallas_call(
        _oproj_kernel,
        grid=(T // tm,),
        in_specs=[row, row, pl.BlockSpec((D, D), lambda i: (0, 0))],
        out_specs=row,
        out_shape=jax.ShapeDtypeStruct((T, D), F32),
        compiler_params=_params("parallel"),
        name="sb_oproj",
    )(x2, o, w_o.astype(BF16))


def _sb_layer(x, norm, w_qkv, q_gain, k_gain, w_o):
    B, S, D = x.shape
    x2 = x.reshape(B * S, D)
    q, k, v = _qkv(x2, norm, w_qkv, q_gain, k_gain)
    o = _attention(q, k, v, B, S)
    return _oproj(x2, o, w_o).reshape(B, S, D)


def kernel(x, pool_norm, pool_w, pool_b, pool_scale, sb_norm, sb_w_qkv, sb_q_gain, sb_k_gain, sb_w_o, moe_norm, moe_w_group, moe_b_group, moe_w_router, moe_b_router, moe_w_gate, moe_w_up, moe_w_down):
    for i in range(DEPTH):
        j = i // 2
        if i % 2 == 0:
            x = _pool_layer(x, pool_norm[j], pool_w[j], pool_b[j], pool_scale[j])
        else:
            x = _sb_layer(x, sb_norm[j], sb_w_qkv[j], sb_q_gain[j], sb_k_gain[j], sb_w_o[j])
        x = _moe_layer(x, moe_norm[i], moe_w_group[i], moe_b_group[i], moe_w_router[i], moe_b_router[i],
                       moe_w_gate[i], moe_w_up[i], moe_w_down[i])
    return x
```

```python
import functools

import jax
import jax.numpy as jnp
from jax import lax
from jax.experimental import pallas as pl
from jax.experimental.pallas import tpu as pltpu

F32 = jnp.float32
BF16 = jnp.bfloat16

D_MODEL = 1024
DEPTH = 4
POOL_WINDOWS = (2, 4, 8, 16)
POOL_GROUP_DIM = D_MODEL // len(POOL_WINDOWS)
POOL_HALO = 16
SB_HEADS = 16
SB_HEAD_DIM = D_MODEL // SB_HEADS
N_GROUPS = 4
EXPERTS_PER_GROUP = 8
N_EXPERTS = N_GROUPS * EXPERTS_PER_GROUP
D_EXPERT = D_MODEL // 4
RMS_EPS = 1e-6
LOG2_E = 1.4426950408889634
LN_2 = 0.6931471805599453
ZERO_WEIGHT_LOG2 = 150.0

LANES = 128
DMA_PRIORITIES = 2
VMEM_LIMIT_BYTES = 56 * 1024 * 1024

POOL_TILE = 512
ROUTER_TILE = 512
QKV_TILE = 512
ATTN_TILE = 256
OPROJ_TILE = 512


def _params(*sem):
    return pltpu.CompilerParams(dimension_semantics=sem, vmem_limit_bytes=VMEM_LIMIT_BYTES)


def _rms(v, gain):
    return v * lax.rsqrt(jnp.mean(v * v, axis=-1, keepdims=True) + RMS_EPS) * gain


def _dot(a, b):
    return jnp.dot(a, b, preferred_element_type=F32)


def _split_bf16(v):
    hi = v.astype(BF16)
    lo = (v - hi.astype(F32)).astype(BF16)
    return hi, lo


def _pool_kernel(xprev_ref, x_ref, norm_ref, w_ref, b_ref, scale_ref, o_ref, hbuf):
    i = pl.program_id(1)
    ts = x_ref.shape[1]
    x = x_ref[0]
    gain = norm_ref[...]
    h = _rms(x, gain)
    hprev = jnp.where(i > 0, _rms(xprev_ref[0], gain), 0.0)
    hbuf[0:POOL_HALO, :] = hprev
    hbuf[POOL_HALO:POOL_HALO + ts, :] = h
    pos = i * ts + lax.broadcasted_iota(jnp.int32, (ts, 1), 0)
    for g, win in enumerate(POOL_WINDOWS):
        c0 = g * POOL_GROUP_DIM
        cols = slice(c0, c0 + POOL_GROUP_DIM)
        hg = h[:, cols]
        acc = hg
        for j in range(1, win):
            acc = acc + hbuf[POOL_HALO - j:POOL_HALO - j + ts, cols]
        count = jnp.minimum(pos + 1, win).astype(F32)
        diff = acc / count - hg
        y = _dot(diff.astype(BF16), w_ref[g]) + b_ref[g]
        o_ref[0, :, cols] = x[:, cols] + y * scale_ref[:, cols]


def _pool_layer(x, norm, w, b, scale):
    B, S, D = x.shape
    ts = min(POOL_TILE, S)
    halo_blocks = ts // POOL_HALO
    return pl.pallas_call(
        _pool_kernel,
        grid=(B, S // ts),
        in_specs=[
            pl.BlockSpec((1, POOL_HALO, D), lambda bi, i: (bi, jnp.maximum(i * halo_blocks - 1, 0), 0)),
            pl.BlockSpec((1, ts, D), lambda bi, i: (bi, i, 0)),
            pl.BlockSpec((1, D), lambda bi, i: (0, 0)),
            pl.BlockSpec((len(POOL_WINDOWS), POOL_GROUP_DIM, POOL_GROUP_DIM), lambda bi, i: (0, 0, 0)),
            pl.BlockSpec((len(POOL_WINDOWS), 1, POOL_GROUP_DIM), lambda bi, i: (0, 0, 0)),
            pl.BlockSpec((1, D), lambda bi, i: (0, 0)),
        ],
        out_specs=pl.BlockSpec((1, ts, D), lambda bi, i: (bi, i, 0)),
        out_shape=jax.ShapeDtypeStruct((B, S, D), F32),
        scratch_shapes=[pltpu.VMEM((POOL_HALO + ts, D), F32)],
        compiler_params=_params("parallel", "parallel"),
        name="pool_layer",
    )(x, x, norm.reshape(1, D), w.astype(BF16), b.reshape(len(POOL_WINDOWS), 1, POOL_GROUP_DIM),
      scale.reshape(1, D))


ROUTER_GROUP_LANE0 = N_EXPERTS
GATE_GROUP_LANE = 40
GATE_RANK_LANE = 41
ROW_WIDTH = D_MODEL + LANES
MOE_ROW_TILE = 512
MOE_PAD_ROWS = N_GROUPS * MOE_ROW_TILE


def _router_kernel(x_ref, norm_ref, whi_ref, wlo_ref, bias_ref, tri_ref, row_ref, where_ref, count_ref,
                   run_ref):
    @pl.when(pl.program_id(0) == 0)
    def _():
        run_ref[...] = jnp.zeros_like(run_ref)

    x = x_ref[...]
    h = _rms(x, norm_ref[...])
    h_hi, h_lo = _split_bf16(h)
    w_hi = whi_ref[...]
    logits = _dot(h_hi, w_hi) + _dot(h_lo, w_hi) + _dot(h_hi, wlo_ref[...]) + bias_ref[...]

    lane = lax.broadcasted_iota(jnp.int32, logits.shape, 1).astype(F32)
    neg_inf = F32(-jnp.inf)
    no_lane = F32(4 * LANES)

    def first_argmax(v):
        m = jnp.max(v, axis=1, keepdims=True)
        idx = jnp.min(jnp.where(v == m, lane, no_lane), axis=1, keepdims=True)
        return m, idx

    is_group = (lane >= ROUTER_GROUP_LANE0) & (lane < ROUTER_GROUP_LANE0 + N_GROUPS)
    gl = jnp.where(is_group, logits, neg_inf)
    gmax, glane = first_argmax(gl)
    g_prob = 1.0 / jnp.sum(jnp.exp(gl - gmax), axis=1, keepdims=True)
    group = glane - ROUTER_GROUP_LANE0
    e0 = group * EXPERTS_PER_GROUP
    el = jnp.where((lane >= e0) & (lane < e0 + EXPERTS_PER_GROUP), logits, neg_inf)
    m1, i1 = first_argmax(el)
    m2, i2 = first_argmax(jnp.where(lane == i1, neg_inf, el))
    r = jnp.exp(m2 - m1)
    w1 = 1.0 / (1.0 + r)
    gate = jnp.where(lane == i1, g_prob * w1, 0.0) + jnp.where(lane == i2, g_prob * (r * w1), 0.0)

    in_group = jnp.where(lane == group, 1.0, 0.0)
    earlier = _dot(tri_ref[...], in_group.astype(BF16))
    run = run_ref[...]
    rank = jnp.sum(in_group * (earlier + run), axis=1, keepdims=True)
    run = run + jnp.sum(in_group, axis=0, keepdims=True)
    run_ref[...] = run
    count_ref[...] = run

    gate = jnp.where(lane == GATE_GROUP_LANE, group, jnp.where(lane == GATE_RANK_LANE, rank, gate))
    row_ref[:, :D_MODEL] = x
    row_ref[:, D_MODEL:] = gate
    where_ref[0] = gate.T[GATE_GROUP_LANE:GATE_GROUP_LANE + 8, :].astype(jnp.int32)


def _router(x2, norm, w_group, b_group, w_router, b_router):
    T, D = x2.shape
    tm = min(ROUTER_TILE, T)
    w = jnp.zeros((D, LANES), F32)
    w = w.at[:, :N_EXPERTS].set(w_router).at[:, ROUTER_GROUP_LANE0:ROUTER_GROUP_LANE0 + N_GROUPS].set(w_group)
    w_hi = w.astype(BF16)
    w_lo = (w - w_hi.astype(F32)).astype(BF16)
    bias = jnp.zeros((1, LANES), F32)
    bias = bias.at[0, :N_EXPERTS].set(b_router).at[0, ROUTER_GROUP_LANE0:ROUTER_GROUP_LANE0 + N_GROUPS].set(b_group)
    idx = jnp.arange(tm)
    tri = (idx[None, :] < idx[:, None]).astype(BF16)
    return pl.pallas_call(
        _router_kernel,
        grid=(T // tm,),
        in_specs=[
            pl.BlockSpec((tm, D), lambda i: (i, 0)),
            pl.BlockSpec((1, D), lambda i: (0, 0)),
            pl.BlockSpec((D, LANES), lambda i: (0, 0)),
            pl.BlockSpec((D, LANES), lambda i: (0, 0)),
            pl.BlockSpec((1, LANES), lambda i: (0, 0)),
            pl.BlockSpec((tm, tm), lambda i: (0, 0)),
        ],
        out_specs=[pl.BlockSpec((tm, ROW_WIDTH), lambda i: (i, 0)),
                   pl.BlockSpec((1, 8, tm), lambda i: (i, 0, 0)),
                   pl.BlockSpec((1, LANES), lambda i: (0, 0))],
        out_shape=[jax.ShapeDtypeStruct((T, ROW_WIDTH), F32),
                   jax.ShapeDtypeStruct((T // tm, 8, tm), jnp.int32),
                   jax.ShapeDtypeStruct((1, LANES), F32)],
        scratch_shapes=[pltpu.VMEM((1, LANES), F32)],
        compiler_params=_params("arbitrary"),
        name="moe_router",
    )(x2, norm.reshape(1, D), w_hi, w_lo, bias, tri)


def _experts_kernel(tile_group_ref, tile_rows_ref, tok_prev_ref, tok_ref, tok_next_ref, rows_ref, norm_ref,
                    wg_ref, wu_ref, wd_ref, out_ref, rbuf, obuf, in_sem, out_sem):
    r = pl.program_id(0)
    last = pl.num_programs(0) - 1
    slot = r % 2

    def gather(dst_slot, tok):
        for t in range(MOE_ROW_TILE):
            pltpu.make_async_copy(rows_ref.at[pl.ds(tok[t], 1)], rbuf.at[dst_slot, pl.ds(t, 1)],
                                  in_sem.at[dst_slot]).start(priority=t % DMA_PRIORITIES)

    def row_copy(src_slot, t, tok):
        return pltpu.make_async_copy(obuf.at[src_slot, pl.ds(t, 1)], out_ref.at[pl.ds(tok[t], 1)],
                                     out_sem.at[src_slot])

    def send_rows(n, src_slot, tok):
        def body(t, carry):
            row_copy(src_slot, t, tok).start()
            return carry
        lax.fori_loop(0, n, body, 0)

    def wait_rows(n, src_slot):
        @pl.when(n == MOE_ROW_TILE)
        def _():
            pltpu.make_async_copy(obuf.at[src_slot], obuf.at[src_slot], out_sem.at[src_slot]).wait()

        @pl.when(n < MOE_ROW_TILE)
        def _():
            def body(t, carry):
                pltpu.make_async_copy(obuf.at[src_slot, pl.ds(0, 1)], obuf.at[src_slot, pl.ds(0, 1)],
                                      out_sem.at[src_slot]).wait()
                return carry
            lax.fori_loop(0, n, body, 0)

    rows_now = tile_rows_ref[r]
    rows_prev = jnp.where(r >= 1, tile_rows_ref[jnp.maximum(r - 1, 0)], 0)

    @pl.when(r == 0)
    def _():
        gather(slot, tok_ref)

    @pl.when(r >= 2)
    def _():
        wait_rows(tile_rows_ref[jnp.maximum(r - 2, 0)], slot)

    @pl.when(r < last)
    def _():
        gather(1 - slot, tok_next_ref)

    pltpu.make_async_copy(rbuf.at[slot], rbuf.at[slot], in_sem.at[slot]).wait()

    @pl.when(rows_prev == MOE_ROW_TILE)
    def _():
        for t in range(MOE_ROW_TILE):
            row_copy(1 - slot, t, tok_prev_ref).start(priority=t % DMA_PRIORITIES)

    @pl.when(rows_prev < MOE_ROW_TILE)
    def _():
        send_rows(rows_prev, 1 - slot, tok_prev_ref)

    @pl.when(rows_now > 0)
    def _():
        x = rbuf[slot, :, :D_MODEL]
        gate = rbuf[slot, :, D_MODEL:]
        h = _rms(x, norm_ref[...]).astype(BF16)
        lane = lax.broadcasted_iota(jnp.int32, gate.shape, 1)
        e0 = tile_group_ref[r] * EXPERTS_PER_GROUP
        acc = x
        for e in range(EXPERTS_PER_GROUP):
            a = _dot(h, wg_ref[e])
            u = _dot(h, wu_ref[e])
            gcol = jnp.sum(jnp.where(lane == e0 + e, gate, 0.0), axis=1, keepdims=True)
            hid = (a * jax.nn.sigmoid(a)) * u * gcol
            acc = acc + _dot(hid.astype(BF16), wd_ref[e])
        obuf[slot] = acc

    @pl.when(r == last)
    def _():
        send_rows(rows_now, slot, tok_ref)
        wait_rows(rows_prev, 1 - slot)
        wait_rows(rows_now, slot)


def _experts(rows, tile_group, tile_rows, tok, norm, w_gate, w_up, w_down, layer):
    T = rows.shape[0]
    D = D_MODEL
    n_tiles = tok.shape[0] // MOE_ROW_TILE
    group_w = lambda r, tg, tr: (layer * N_GROUPS + tg[r], 0, 0)
    smem_tile = lambda index: pl.BlockSpec((MOE_ROW_TILE,), index, memory_space=pltpu.SMEM)
    return pl.pallas_call(
        _experts_kernel,
        grid_spec=pltpu.PrefetchScalarGridSpec(
            num_scalar_prefetch=2,
            grid=(n_tiles,),
            in_specs=[smem_tile(lambda r, tg, tr: (jnp.maximum(r - 1, 0),)),
                      smem_tile(lambda r, tg, tr: (r,)),
                      smem_tile(lambda r, tg, tr: (jnp.minimum(r + 1, n_tiles - 1),)),
                      pl.BlockSpec(memory_space=pl.ANY),
                      pl.BlockSpec((1, D), lambda r, tg, tr: (0, 0)),
                      pl.BlockSpec((EXPERTS_PER_GROUP, D, D_EXPERT), group_w),
                      pl.BlockSpec((EXPERTS_PER_GROUP, D, D_EXPERT), group_w),
                      pl.BlockSpec((EXPERTS_PER_GROUP, D_EXPERT, D), group_w)],
            out_specs=pl.BlockSpec(memory_space=pl.ANY),
            scratch_shapes=[pltpu.VMEM((2, MOE_ROW_TILE, ROW_WIDTH), F32), pltpu.VMEM((2, MOE_ROW_TILE, D), F32),
                            pltpu.SemaphoreType.DMA((2,)), pltpu.SemaphoreType.DMA((2,))],
        ),
        out_shape=jax.ShapeDtypeStruct((T, D), F32),
        compiler_params=_params("arbitrary"),
        name="moe_experts",
    )(tile_group, tile_rows, tok, tok, tok, rows, norm.reshape(1, D), w_gate.astype(BF16), w_up.astype(BF16),
      w_down.astype(BF16))


def _moe_layer(x, norm, w_group, b_group, w_router, b_router, w_gate, w_up, w_down, layer=0):
    B, S, D = x.shape
    T = B * S
    rows, where, counts = _router(x.reshape(T, D), norm, w_group, b_group, w_router, b_router)
    where = where.transpose(1, 0, 2).reshape(8, T)
    group, rank = where[0], where[1]
    counts = counts[0, :N_GROUPS].astype(jnp.int32)
    tiles = (counts + MOE_ROW_TILE - 1) // MOE_ROW_TILE
    tile_end = jnp.cumsum(tiles)
    tile_first = tile_end - tiles
    dest = tile_first[group] * MOE_ROW_TILE + rank
    total_rows = T + MOE_PAD_ROWS
    tok = jnp.zeros((total_rows,), jnp.int32).at[dest].set(jnp.arange(T, dtype=jnp.int32))
    tile_idx = jnp.arange(total_rows // MOE_ROW_TILE)
    tile_group = jnp.minimum(jnp.sum(tile_idx[:, None] >= tile_end[None, :], axis=1), N_GROUPS - 1)
    tile_rows = jnp.clip(counts[tile_group] - (tile_idx - tile_first[tile_group]) * MOE_ROW_TILE, 0, MOE_ROW_TILE)
    tile_rows = jnp.where(tile_idx < tile_end[-1], tile_rows, 0)
    return _experts(rows, tile_group.astype(jnp.int32), tile_rows.astype(jnp.int32), tok, norm,
                    w_gate, w_up, w_down, layer).reshape(B, S, D)


def _qkv_kernel(x_ref, norm_ref, w_ref, seg_ref, segt_ref, qg_ref, kg_ref, q_ref, k_ref, v_ref):
    D = x_ref.shape[1]
    h = _rms(x_ref[...], norm_ref[...]).astype(BF16)
    qkv = _dot(h, w_ref[...])
    seg = seg_ref[...]
    segt = segt_ref[...]

    def head_norm(t, gain):
        ms = _dot((t * t).astype(BF16), seg) * (1.0 / SB_HEAD_DIM)
        r_hi, r_lo = _split_bf16(lax.rsqrt(ms + RMS_EPS))
        return t * (_dot(r_hi, segt) + _dot(r_lo, segt)) * gain

    q_ref[...] = head_norm(qkv[:, :D], qg_ref[...]).astype(BF16)
    k_ref[...] = head_norm(qkv[:, D:2 * D], kg_ref[...]).astype(BF16)
    v_ref[...] = qkv[:, 2 * D:].astype(BF16)


def _qkv(x2, norm, w_qkv, q_gain, k_gain):
    T, D = x2.shape
    tm = min(QKV_TILE, T)
    head_of = jnp.arange(D) // SB_HEAD_DIM
    seg = (head_of[:, None] == jnp.arange(LANES)[None, :]).astype(BF16)
    segt = seg.T
    qg = (jnp.tile(q_gain, SB_HEADS) * (SB_HEAD_DIM ** -0.5 * LOG2_E)).reshape(1, D)
    kg = jnp.tile(k_gain, SB_HEADS).reshape(1, D)
    row = pl.BlockSpec((tm, D), lambda i: (i, 0))
    vec = pl.BlockSpec((1, D), lambda i: (0, 0))
    return pl.pallas_call(
        _qkv_kernel,
        grid=(T // tm,),
        in_specs=[row, vec, pl.BlockSpec((D, 3 * D), lambda i: (0, 0)),
                  pl.BlockSpec((D, LANES), lambda i: (0, 0)), pl.BlockSpec((LANES, D), lambda i: (0, 0)),
                  vec, vec],
        out_specs=[row, row, row],
        out_shape=[jax.ShapeDtypeStruct((T, D), BF16)] * 3,
        compiler_params=_params("parallel"),
        name="sb_qkv",
    )(x2, norm.reshape(1, D), w_qkv.astype(BF16), seg, segt, qg, kg)


def _attn_kernel(q_ref, k_ref, v_ref, u_ref, ceil_ref, bias_ref, o_ref, qh_buf, acc_ref, rem_ref, *, t):
    qi = pl.program_id(2)
    q2 = q_ref[...]
    lane = lax.broadcasted_iota(jnp.int32, (t, LANES), 1)
    first_head = lane < SB_HEAD_DIM
    zero = jnp.zeros_like(q2)
    qh_buf[0] = jnp.where(first_head, q2, zero)
    qh_buf[1] = jnp.where(first_head, zero, q2)

    def key_tile(j):
        start = pl.multiple_of(jnp.maximum(qi - j, 0) * t, t)
        return k_ref[0, pl.ds(start, t), :], v_ref[0, pl.ds(start, t), :]

    def scores(hd, kt):
        return lax.dot_general(qh_buf[hd], kt, (((1,), (1,)), ((), ())), preferred_element_type=F32)

    def drop_of(z, ceil=None):
        drop = jnp.maximum(z, 0.0) + jnp.log2(1.0 + jnp.exp(jnp.abs(z) * (-LN_2)))
        log_beta = z - drop
        if ceil is not None:
            drop = jnp.minimum(drop, ceil)
        return drop, log_beta

    def suffix_sums(drop):
        return _dot(drop.astype(BF16), u_ref[...])

    def weights(log_beta, later, bias=None):
        arg = log_beta - later
        if bias is not None:
            arg = arg + bias
        return jnp.exp2(arg).astype(BF16)

    (k0, v0), (k1, v1) = key_tile(0), key_tile(1)
    v1 = jnp.where(qi >= 1, v1, jnp.zeros_like(v1))
    v01 = jnp.concatenate([v0, v1], axis=0)
    z = [[scores(hd, kt) for kt in (k0, k1)] for hd in range(2)]
    dl = [[drop_of(z[hd][0], ceil_ref[...]), drop_of(z[hd][1])] for hd in range(2)]
    c = [[suffix_sums(dl[hd][j][0]) for j in range(2)] for hd in range(2)]
    pv = []
    for hd in range(2):
        (d0, lb0), (d1, lb1) = dl[hd]
        rem1 = c[hd][0][:, 0:1] + d0[:, 0:1]
        a0 = weights(lb0, c[hd][0], bias_ref[...])
        a1 = weights(lb1, c[hd][1] + rem1)
        pv.append(_dot(jnp.concatenate([a0, a1], axis=1), v01))
        rem_ref[hd] = rem1 + (c[hd][1][:, 0:1] + d1[:, 0:1])
    acc_ref[...] = jnp.where(first_head, pv[0], pv[1])

    def rem_min():
        return jnp.min(jnp.minimum(rem_ref[0], rem_ref[1]))

    def more(carry):
        j, smallest_rem = carry
        return (j <= qi) & (smallest_rem < ZERO_WEIGHT_LOG2)

    def sweep(carry):
        j, _ = carry
        kt, vt = key_tile(j)
        pv = []
        for hd in range(2):
            drop, log_beta = drop_of(scores(hd, kt))
            c = suffix_sums(drop)
            rem = rem_ref[hd]
            pv.append(_dot(weights(log_beta, c + rem), vt))
            rem_ref[hd] = rem + (c[:, 0:1] + drop[:, 0:1])
        acc_ref[...] += jnp.where(first_head, pv[0], pv[1])
        return j + 1, rem_min()

    lax.while_loop(more, sweep, (jnp.int32(2), rem_min()))
    o_ref[...] = acc_ref[...].astype(o_ref.dtype)


def _attention(q, k, v, B, S):
    T, D = q.shape
    t = min(ATTN_TILE, S)
    nq = S // t
    idx = jnp.arange(t)
    u = (idx[:, None] > idx[None, :]).astype(BF16)
    causal = idx[None, :] < idx[:, None]
    ceil = jnp.where(causal, jnp.inf, 0.0).astype(F32)
    bias = jnp.where(causal, 0.0, -jnp.inf).astype(F32)
    k3 = k.reshape(B, S, D)
    v3 = v.reshape(B, S, D)
    kv_spec = pl.BlockSpec((1, S, LANES), lambda b, hp, i: (b, 0, hp))
    table_spec = pl.BlockSpec((t, t), lambda b, hp, i: (0, 0))
    return pl.pallas_call(
        functools.partial(_attn_kernel, t=t),
        grid=(B, D // LANES, nq),
        in_specs=[pl.BlockSpec((t, LANES), lambda b, hp, i: (b * nq + i, hp)), kv_spec, kv_spec,
                  table_spec, table_spec, table_spec],
        out_specs=pl.BlockSpec((t, LANES), lambda b, hp, i: (b * nq + i, hp)),
        out_shape=jax.ShapeDtypeStruct((T, D), BF16),
        scratch_shapes=[pltpu.VMEM((2, t, LANES), BF16), pltpu.VMEM((t, LANES), F32),
                        pltpu.VMEM((2, t, 1), F32)],
        compiler_params=_params("parallel", "parallel", "arbitrary"),
        name="sb_attention",
    )(q, k3, v3, u, ceil, bias)


def _oproj_kernel(x_ref, o_ref, w_ref, out_ref):
    out_ref[...] = x_ref[...] + _dot(o_ref[...], w_ref[...])


def _oproj(x2, o, w_o):
    T, D = x2.shape
    tm = min(OPROJ_TILE, T)
    row = pl.BlockSpec((tm, D), lambda i: (i, 0))
    return pl.pallas_call(
        _oproj_kernel,
        grid=(T // tm,),
        in_specs=[row, row, pl.BlockSpec((D, D), lambda i: (0, 0))],
        out_specs=row,
        out_shape=jax.ShapeDtypeStruct((T, D), F32),
        compiler_params=_params("parallel"),
        name="sb_oproj",
    )(x2, o, w_o.astype(BF16))


def _sb_layer(x, norm, w_qkv, q_gain, k_gain, w_o):
    B, S, D = x.shape
    x2 = x.reshape(B * S, D)
    q, k, v = _qkv(x2, norm, w_qkv, q_gain, k_gain)
    o = _attention(q, k, v, B, S)
    return _oproj(x2, o, w_o).reshape(B, S, D)


def kernel(x, pool_norm, pool_w, pool_b, pool_scale, sb_norm, sb_w_qkv, sb_q_gain, sb_k_gain, sb_w_o, moe_norm, moe_w_group, moe_b_group, moe_w_router, moe_b_router, moe_w_gate, moe_w_up, moe_w_down):
    stacked = lambda w: w.astype(BF16).reshape((DEPTH * N_EXPERTS,) + w.shape[2:])
    w_gate, w_up, w_down = stacked(moe_w_gate), stacked(moe_w_up), stacked(moe_w_down)
    for i in range(DEPTH):
        j = i // 2
        if i % 2 == 0:
            x = _pool_layer(x, pool_norm[j], pool_w[j], pool_b[j], pool_scale[j])
        else:
            x = _sb_layer(x, sb_norm[j], sb_w_qkv[j], sb_q_gain[j], sb_k_gain[j], sb_w_o[j])
        x = _moe_layer(x, moe_norm[i], moe_w_group[i], moe_b_group[i], moe_w_router[i], moe_b_router[i],
                       w_gate, w_up, w_down, layer=i)
    return x
```

```python
import functools

import jax
import jax.numpy as jnp
from jax import lax
from jax.experimental import pallas as pl
from jax.experimental.pallas import tpu as pltpu

F32 = jnp.float32
BF16 = jnp.bfloat16

D_MODEL = 1024
DEPTH = 4
POOL_WINDOWS = (2, 4, 8, 16)
POOL_GROUP_DIM = D_MODEL // len(POOL_WINDOWS)
POOL_HALO = 16
SB_HEADS = 16
SB_HEAD_DIM = D_MODEL // SB_HEADS
N_GROUPS = 4
EXPERTS_PER_GROUP = 8
N_EXPERTS = N_GROUPS * EXPERTS_PER_GROUP
D_EXPERT = D_MODEL // 4
RMS_EPS = 1e-6
LOG2_E = 1.4426950408889634
LN_2 = 0.6931471805599453
ZERO_WEIGHT_LOG2 = 150.0

LANES = 128
DMA_PRIORITIES = 2
VMEM_LIMIT_BYTES = 56 * 1024 * 1024

POOL_TILE = 512
ROUTER_TILE = 512
QKV_TILE = 512
ATTN_TILE = 256
ATTN_TILES_PER_STEP = 2
OPROJ_TILE = 512


def _params(*sem):
    return pltpu.CompilerParams(dimension_semantics=sem, vmem_limit_bytes=VMEM_LIMIT_BYTES)


def _rms(v, gain):
    return v * lax.rsqrt(jnp.mean(v * v, axis=-1, keepdims=True) + RMS_EPS) * gain


def _dot(a, b):
    return jnp.dot(a, b, preferred_element_type=F32)


def _split_bf16(v):
    hi = v.astype(BF16)
    lo = (v - hi.astype(F32)).astype(BF16)
    return hi, lo


def _pool_kernel(xprev_ref, x_ref, norm_ref, w_ref, b_ref, scale_ref, o_ref, hbuf):
    i = pl.program_id(1)
    ts = x_ref.shape[1]
    x = x_ref[0]
    gain = norm_ref[...]
    h = _rms(x, gain)
    hprev = jnp.where(i > 0, _rms(xprev_ref[0], gain), 0.0)
    hbuf[0:POOL_HALO, :] = hprev
    hbuf[POOL_HALO:POOL_HALO + ts, :] = h
    pos = i * ts + lax.broadcasted_iota(jnp.int32, (ts, 1), 0)
    for g, win in enumerate(POOL_WINDOWS):
        c0 = g * POOL_GROUP_DIM
        cols = slice(c0, c0 + POOL_GROUP_DIM)
        hg = h[:, cols]
        acc = hg
        for j in range(1, win):
            acc = acc + hbuf[POOL_HALO - j:POOL_HALO - j + ts, cols]
        count = jnp.minimum(pos + 1, win).astype(F32)
        diff = acc / count - hg
        y = _dot(diff.astype(BF16), w_ref[g]) + b_ref[g]
        o_ref[0, :, cols] = x[:, cols] + y * scale_ref[:, cols]


def _pool_layer(x, norm, w, b, scale):
    B, S, D = x.shape
    ts = min(POOL_TILE, S)
    halo_blocks = ts // POOL_HALO
    return pl.pallas_call(
        _pool_kernel,
        grid=(B, S // ts),
        in_specs=[
            pl.BlockSpec((1, POOL_HALO, D), lambda bi, i: (bi, jnp.maximum(i * halo_blocks - 1, 0), 0)),
            pl.BlockSpec((1, ts, D), lambda bi, i: (bi, i, 0)),
            pl.BlockSpec((1, D), lambda bi, i: (0, 0)),
            pl.BlockSpec((len(POOL_WINDOWS), POOL_GROUP_DIM, POOL_GROUP_DIM), lambda bi, i: (0, 0, 0)),
            pl.BlockSpec((len(POOL_WINDOWS), 1, POOL_GROUP_DIM), lambda bi, i: (0, 0, 0)),
            pl.BlockSpec((1, D), lambda bi, i: (0, 0)),
        ],
        out_specs=pl.BlockSpec((1, ts, D), lambda bi, i: (bi, i, 0)),
        out_shape=jax.ShapeDtypeStruct((B, S, D), F32),
        scratch_shapes=[pltpu.VMEM((POOL_HALO + ts, D), F32)],
        compiler_params=_params("parallel", "parallel"),
        name="pool_layer",
    )(x, x, norm.reshape(1, D), w.astype(BF16), b.reshape(len(POOL_WINDOWS), 1, POOL_GROUP_DIM),
      scale.reshape(1, D))


ROUTER_GROUP_LANE0 = N_EXPERTS
GATE_GROUP_LANE = 40
GATE_RANK_LANE = 41
ROW_WIDTH = D_MODEL + LANES
MOE_ROW_TILE = 512
MOE_PAD_ROWS = N_GROUPS * MOE_ROW_TILE


def _router_kernel(x_ref, norm_ref, whi_ref, wlo_ref, bias_ref, tri_ref, row_ref, where_ref, count_ref,
                   run_ref):
    @pl.when(pl.program_id(0) == 0)
    def _():
        run_ref[...] = jnp.zeros_like(run_ref)

    x = x_ref[...]
    h = _rms(x, norm_ref[...])
    h_hi, h_lo = _split_bf16(h)
    w_hi = whi_ref[...]
    logits = _dot(h_hi, w_hi) + _dot(h_lo, w_hi) + _dot(h_hi, wlo_ref[...]) + bias_ref[...]

    lane = lax.broadcasted_iota(jnp.int32, logits.shape, 1).astype(F32)
    neg_inf = F32(-jnp.inf)
    no_lane = F32(4 * LANES)

    def first_argmax(v):
        m = jnp.max(v, axis=1, keepdims=True)
        idx = jnp.min(jnp.where(v == m, lane, no_lane), axis=1, keepdims=True)
        return m, idx

    is_group = (lane >= ROUTER_GROUP_LANE0) & (lane < ROUTER_GROUP_LANE0 + N_GROUPS)
    gl = jnp.where(is_group, logits, neg_inf)
    gmax, glane = first_argmax(gl)
    g_prob = 1.0 / jnp.sum(jnp.exp(gl - gmax), axis=1, keepdims=True)
    group = glane - ROUTER_GROUP_LANE0
    e0 = group * EXPERTS_PER_GROUP
    el = jnp.where((lane >= e0) & (lane < e0 + EXPERTS_PER_GROUP), logits, neg_inf)
    m1, i1 = first_argmax(el)
    m2, i2 = first_argmax(jnp.where(lane == i1, neg_inf, el))
    r = jnp.exp(m2 - m1)
    w1 = 1.0 / (1.0 + r)
    gate = jnp.where(lane == i1, g_prob * w1, 0.0) + jnp.where(lane == i2, g_prob * (r * w1), 0.0)

    in_group = jnp.where(lane == group, 1.0, 0.0)
    earlier = _dot(tri_ref[...], in_group.astype(BF16))
    run = run_ref[...]
    rank = jnp.sum(in_group * (earlier + run), axis=1, keepdims=True)
    run = run + jnp.sum(in_group, axis=0, keepdims=True)
    run_ref[...] = run
    count_ref[...] = run

    gate = jnp.where(lane == GATE_GROUP_LANE, group, jnp.where(lane == GATE_RANK_LANE, rank, gate))
    row_ref[:, :D_MODEL] = x
    row_ref[:, D_MODEL:] = gate
    where_ref[0] = gate.T[GATE_GROUP_LANE:GATE_GROUP_LANE + 8, :].astype(jnp.int32)


def _router(x2, norm, w_group, b_group, w_router, b_router):
    T, D = x2.shape
    tm = min(ROUTER_TILE, T)
    w = jnp.zeros((D, LANES), F32)
    w = w.at[:, :N_EXPERTS].set(w_router).at[:, ROUTER_GROUP_LANE0:ROUTER_GROUP_LANE0 + N_GROUPS].set(w_group)
    w_hi = w.astype(BF16)
    w_lo = (w - w_hi.astype(F32)).astype(BF16)
    bias = jnp.zeros((1, LANES), F32)
    bias = bias.at[0, :N_EXPERTS].set(b_router).at[0, ROUTER_GROUP_LANE0:ROUTER_GROUP_LANE0 + N_GROUPS].set(b_group)
    idx = jnp.arange(tm)
    tri = (idx[None, :] < idx[:, None]).astype(BF16)
    return pl.pallas_call(
        _router_kernel,
        grid=(T // tm,),
        in_specs=[
            pl.BlockSpec((tm, D), lambda i: (i, 0)),
            pl.BlockSpec((1, D), lambda i: (0, 0)),
            pl.BlockSpec((D, LANES), lambda i: (0, 0)),
            pl.BlockSpec((D, LANES), lambda i: (0, 0)),
            pl.BlockSpec((1, LANES), lambda i: (0, 0)),
            pl.BlockSpec((tm, tm), lambda i: (0, 0)),
        ],
        out_specs=[pl.BlockSpec((tm, ROW_WIDTH), lambda i: (i, 0)),
                   pl.BlockSpec((1, 8, tm), lambda i: (i, 0, 0)),
                   pl.BlockSpec((1, LANES), lambda i: (0, 0))],
        out_shape=[jax.ShapeDtypeStruct((T, ROW_WIDTH), F32),
                   jax.ShapeDtypeStruct((T // tm, 8, tm), jnp.int32),
                   jax.ShapeDtypeStruct((1, LANES), F32)],
        scratch_shapes=[pltpu.VMEM((1, LANES), F32)],
        compiler_params=_params("arbitrary"),
        name="moe_router",
    )(x2, norm.reshape(1, D), w_hi, w_lo, bias, tri)


def _experts_kernel(tile_group_ref, tile_rows_ref, tok_prev_ref, tok_ref, tok_next_ref, rows_ref, norm_ref,
                    wg_ref, wu_ref, wd_ref, out_ref, rbuf, obuf, in_sem, out_sem):
    r = pl.program_id(0)
    last = pl.num_programs(0) - 1
    slot = r % 2

    def gather(dst_slot, tok):
        for t in range(MOE_ROW_TILE):
            pltpu.make_async_copy(rows_ref.at[pl.ds(tok[t], 1)], rbuf.at[dst_slot, pl.ds(t, 1)],
                                  in_sem.at[dst_slot]).start(priority=t % DMA_PRIORITIES)

    def row_copy(src_slot, t, tok):
        return pltpu.make_async_copy(obuf.at[src_slot, pl.ds(t, 1)], out_ref.at[pl.ds(tok[t], 1)],
                                     out_sem.at[src_slot])

    def send_rows(n, src_slot, tok):
        def body(t, carry):
            row_copy(src_slot, t, tok).start()
            return carry
        lax.fori_loop(0, n, body, 0)

    def wait_rows(n, src_slot):
        @pl.when(n == MOE_ROW_TILE)
        def _():
            pltpu.make_async_copy(obuf.at[src_slot], obuf.at[src_slot], out_sem.at[src_slot]).wait()

        @pl.when(n < MOE_ROW_TILE)
        def _():
            def body(t, carry):
                pltpu.make_async_copy(obuf.at[src_slot, pl.ds(0, 1)], obuf.at[src_slot, pl.ds(0, 1)],
                                      out_sem.at[src_slot]).wait()
                return carry
            lax.fori_loop(0, n, body, 0)

    rows_now = tile_rows_ref[r]
    rows_prev = jnp.where(r >= 1, tile_rows_ref[jnp.maximum(r - 1, 0)], 0)

    @pl.when((r == 0) & (rows_now > 0))
    def _():
        gather(slot, tok_ref)

    @pl.when(r >= 2)
    def _():
        wait_rows(tile_rows_ref[jnp.maximum(r - 2, 0)], slot)

    rows_next = jnp.where(r < last, tile_rows_ref[jnp.minimum(r + 1, last)], 0)

    @pl.when(rows_next > 0)
    def _():
        gather(1 - slot, tok_next_ref)

    @pl.when(rows_now > 0)
    def _():
        pltpu.make_async_copy(rbuf.at[slot], rbuf.at[slot], in_sem.at[slot]).wait()

    @pl.when(rows_prev == MOE_ROW_TILE)
    def _():
        for t in range(MOE_ROW_TILE):
            row_copy(1 - slot, t, tok_prev_ref).start(priority=t % DMA_PRIORITIES)

    @pl.when(rows_prev < MOE_ROW_TILE)
    def _():
        send_rows(rows_prev, 1 - slot, tok_prev_ref)

    @pl.when(rows_now > 0)
    def _():
        x = rbuf[slot, :, :D_MODEL]
        gate = rbuf[slot, :, D_MODEL:]
        h = _rms(x, norm_ref[...]).astype(BF16)
        lane = lax.broadcasted_iota(jnp.int32, gate.shape, 1)
        e0 = tile_group_ref[r] * EXPERTS_PER_GROUP
        acc = x
        for e in range(EXPERTS_PER_GROUP):
            a = _dot(h, wg_ref[e])
            u = _dot(h, wu_ref[e])
            gcol = jnp.sum(jnp.where(lane == e0 + e, gate, 0.0), axis=1, keepdims=True)
            hid = (a * jax.nn.sigmoid(a)) * u * gcol
            acc = acc + _dot(hid.astype(BF16), wd_ref[e])
        obuf[slot] = acc

    @pl.when(r == last)
    def _():
        send_rows(rows_now, slot, tok_ref)
        wait_rows(rows_prev, 1 - slot)
        wait_rows(rows_now, slot)


def _experts(rows, tile_group, tile_rows, tok, norm, w_gate, w_up, w_down, layer):
    T = rows.shape[0]
    D = D_MODEL
    n_tiles = tok.shape[0] // MOE_ROW_TILE
    group_w = lambda r, tg, tr: (layer * N_GROUPS + tg[r], 0, 0)
    smem_tile = lambda index: pl.BlockSpec((MOE_ROW_TILE,), index, memory_space=pltpu.SMEM)
    return pl.pallas_call(
        _experts_kernel,
        grid_spec=pltpu.PrefetchScalarGridSpec(
            num_scalar_prefetch=2,
            grid=(n_tiles,),
            in_specs=[smem_tile(lambda r, tg, tr: (jnp.maximum(r - 1, 0),)),
                      smem_tile(lambda r, tg, tr: (r,)),
                      smem_tile(lambda r, tg, tr: (jnp.minimum(r + 1, n_tiles - 1),)),
                      pl.BlockSpec(memory_space=pl.ANY),
                      pl.BlockSpec((1, D), lambda r, tg, tr: (0, 0)),
                      pl.BlockSpec((EXPERTS_PER_GROUP, D, D_EXPERT), group_w),
                      pl.BlockSpec((EXPERTS_PER_GROUP, D, D_EXPERT), group_w),
                      pl.BlockSpec((EXPERTS_PER_GROUP, D_EXPERT, D), group_w)],
            out_specs=pl.BlockSpec(memory_space=pl.ANY),
            scratch_shapes=[pltpu.VMEM((2, MOE_ROW_TILE, ROW_WIDTH), F32), pltpu.VMEM((2, MOE_ROW_TILE, D), F32),
                            pltpu.SemaphoreType.DMA((2,)), pltpu.SemaphoreType.DMA((2,))],
        ),
        out_shape=jax.ShapeDtypeStruct((T, D), F32),
        compiler_params=_params("arbitrary"),
        name="moe_experts",
    )(tile_group, tile_rows, tok, tok, tok, rows, norm.reshape(1, D), w_gate.astype(BF16), w_up.astype(BF16),
      w_down.astype(BF16))


def _invert_kernel(dest_ref, tok_ref):
    i = pl.program_id(0)
    tm = dest_ref.shape[0]

    @pl.when(i == 0)
    def _():
        def clear(p, carry):
            tok_ref[p] = 0
            return carry
        lax.fori_loop(0, tok_ref.shape[0], clear, 0, unroll=8)

    for t in range(tm):
        tok_ref[dest_ref[t]] = i * tm + t


def _invert(dest, total_rows):
    T = dest.shape[0]
    tm = min(ROUTER_TILE, T)
    return pl.pallas_call(
        _invert_kernel,
        grid=(T // tm,),
        in_specs=[pl.BlockSpec((tm,), lambda i: (i,), memory_space=pltpu.SMEM)],
        out_specs=pl.BlockSpec(memory_space=pltpu.SMEM),
        out_shape=jax.ShapeDtypeStruct((total_rows,), jnp.int32),
        compiler_params=_params("arbitrary"),
        name="moe_invert",
    )(dest)


def _moe_layer(x, norm, w_group, b_group, w_router, b_router, w_gate, w_up, w_down, layer=0):
    B, S, D = x.shape
    T = B * S
    rows, where, counts = _router(x.reshape(T, D), norm, w_group, b_group, w_router, b_router)
    where = where.transpose(1, 0, 2).reshape(8, T)
    group, rank = where[0], where[1]
    counts = counts[0, :N_GROUPS].astype(jnp.int32)
    tiles = (counts + MOE_ROW_TILE - 1) // MOE_ROW_TILE
    tile_end = jnp.cumsum(tiles)
    tile_first = tile_end - tiles
    dest = tile_first[group] * MOE_ROW_TILE + rank
    total_rows = T + MOE_PAD_ROWS
    tok = _invert(dest.astype(jnp.int32), total_rows)
    tile_idx = jnp.arange(total_rows // MOE_ROW_TILE)
    tile_group = jnp.minimum(jnp.sum(tile_idx[:, None] >= tile_end[None, :], axis=1), N_GROUPS - 1)
    tile_rows = jnp.clip(counts[tile_group] - (tile_idx - tile_first[tile_group]) * MOE_ROW_TILE, 0, MOE_ROW_TILE)
    tile_rows = jnp.where(tile_idx < tile_end[-1], tile_rows, 0)
    return _experts(rows, tile_group.astype(jnp.int32), tile_rows.astype(jnp.int32), tok, norm,
                    w_gate, w_up, w_down, layer).reshape(B, S, D)


def _qkv_kernel(x_ref, norm_ref, w_ref, seg_ref, segt_ref, qg_ref, kg_ref, q_ref, k_ref, v_ref):
    D = x_ref.shape[1]
    h = _rms(x_ref[...], norm_ref[...]).astype(BF16)
    qkv = _dot(h, w_ref[...])
    seg = seg_ref[...]
    segt = segt_ref[...]

    def head_norm(t, gain):
        ms = _dot((t * t).astype(BF16), seg) * (1.0 / SB_HEAD_DIM)
        r_hi, r_lo = _split_bf16(lax.rsqrt(ms + RMS_EPS))
        return t * (_dot(r_hi, segt) + _dot(r_lo, segt)) * gain

    q_ref[...] = head_norm(qkv[:, :D], qg_ref[...]).astype(BF16)
    k_ref[...] = head_norm(qkv[:, D:2 * D], kg_ref[...]).astype(BF16)
    v_ref[...] = qkv[:, 2 * D:].astype(BF16)


def _qkv(x2, norm, w_qkv, q_gain, k_gain):
    T, D = x2.shape
    tm = min(QKV_TILE, T)
    head_of = jnp.arange(D) // SB_HEAD_DIM
    seg = (head_of[:, None] == jnp.arange(LANES)[None, :]).astype(BF16)
    segt = seg.T
    qg = (jnp.tile(q_gain, SB_HEADS) * (SB_HEAD_DIM ** -0.5 * LOG2_E)).reshape(1, D)
    kg = jnp.tile(k_gain, SB_HEADS).reshape(1, D)
    row = pl.BlockSpec((tm, D), lambda i: (i, 0))
    vec = pl.BlockSpec((1, D), lambda i: (0, 0))
    return pl.pallas_call(
        _qkv_kernel,
        grid=(T // tm,),
        in_specs=[row, vec, pl.BlockSpec((D, 3 * D), lambda i: (0, 0)),
                  pl.BlockSpec((D, LANES), lambda i: (0, 0)), pl.BlockSpec((LANES, D), lambda i: (0, 0)),
                  vec, vec],
        out_specs=[row, row, row],
        out_shape=[jax.ShapeDtypeStruct((T, D), BF16)] * 3,
        compiler_params=_params("parallel"),
        name="sb_qkv",
    )(x2, norm.reshape(1, D), w_qkv.astype(BF16), seg, segt, qg, kg)


def _attn_kernel(q_ref, k_ref, v_ref, u_ref, ceil_ref, bias_ref, o_ref, qh_buf, acc_ref, rem_ref, *, t, subs):
    step = pl.program_id(2)
    lane = lax.broadcasted_iota(jnp.int32, (t, LANES), 1)
    first_head = lane < SB_HEAD_DIM
    for s in range(subs):
        q2 = q_ref[s * t:(s + 1) * t, :]
        zero = jnp.zeros_like(q2)
        qh_buf[s, 0] = jnp.where(first_head, q2, zero)
        qh_buf[s, 1] = jnp.where(first_head, zero, q2)

    def key_tile(qi, j):
        start = pl.multiple_of(jnp.maximum(qi - j, 0) * t, t)
        return k_ref[0, pl.ds(start, t), :], v_ref[0, pl.ds(start, t), :]

    def scores(s, hd, kt):
        return lax.dot_general(qh_buf[s, hd], kt, (((1,), (1,)), ((), ())), preferred_element_type=F32)

    def drop_of(z, ceil=None):
        drop = jnp.maximum(z, 0.0) + jnp.log2(1.0 + jnp.exp(jnp.abs(z) * (-LN_2)))
        log_beta = z - drop
        if ceil is not None:
            drop = jnp.minimum(drop, ceil)
        return drop, log_beta

    def suffix_sums(drop):
        return _dot(drop.astype(BF16), u_ref[...])

    def weights(log_beta, later, bias=None):
        arg = log_beta - later
        if bias is not None:
            arg = arg + bias
        return jnp.exp2(arg).astype(BF16)

    tiles = [step * subs + s for s in range(subs)]
    units = [(s, hd) for s in range(subs) for hd in range(2)]
    keys, values = [], []
    for s, qi in enumerate(tiles):
        (k0, v0), (k1, v1) = key_tile(qi, 0), key_tile(qi, 1)
        keys.append((k0, k1))
        values.append(jnp.concatenate([v0, jnp.where(qi >= 1, v1, jnp.zeros_like(v1))], axis=0))
    z = {(s, hd): [scores(s, hd, kt) for kt in keys[s]] for s, hd in units}
    dl = {un: [drop_of(z[un][0], ceil_ref[...]), drop_of(z[un][1])] for un in units}
    c = {un: [suffix_sums(dl[un][j][0]) for j in range(2)] for un in units}
    pv = {}
    for s, hd in units:
        (d0, lb0), (d1, lb1) = dl[s, hd]
        c0, c1 = c[s, hd]
        rem1 = c0[:, 0:1] + d0[:, 0:1]
        a0 = weights(lb0, c0, bias_ref[...])
        a1 = weights(lb1, c1 + rem1)
        pv[s, hd] = _dot(jnp.concatenate([a0, a1], axis=1), values[s])
        rem_ref[s, hd] = rem1 + (c1[:, 0:1] + d1[:, 0:1])
    for s in range(subs):
        acc_ref[s] = jnp.where(first_head, pv[s, 0], pv[s, 1])

    for s, qi in enumerate(tiles):
        def rem_min():
            return jnp.min(jnp.minimum(rem_ref[s, 0], rem_ref[s, 1]))

        def more(carry):
            j, smallest_rem = carry
            return (j <= qi) & (smallest_rem < ZERO_WEIGHT_LOG2)

        def sweep(carry):
            j, _ = carry
            kt, vt = key_tile(qi, j)
            pv = []
            for hd in range(2):
                drop, log_beta = drop_of(scores(s, hd, kt))
                c = suffix_sums(drop)
                rem = rem_ref[s, hd]
                pv.append(_dot(weights(log_beta, c + rem), vt))
                rem_ref[s, hd] = rem + (c[:, 0:1] + drop[:, 0:1])
            acc_ref[s] += jnp.where(first_head, pv[0], pv[1])
            return j + 1, rem_min()

        lax.while_loop(more, sweep, (jnp.int32(2), rem_min()))
        o_ref[s * t:(s + 1) * t, :] = acc_ref[s].astype(o_ref.dtype)


def _attention(q, k, v, B, S):
    T, D = q.shape
    t = min(ATTN_TILE, S)
    subs = ATTN_TILES_PER_STEP
    steps = S // (t * subs)
    idx = jnp.arange(t)
    u = (idx[:, None] > idx[None, :]).astype(BF16)
    causal = idx[None, :] < idx[:, None]
    ceil = jnp.where(causal, jnp.inf, 0.0).astype(F32)
    bias = jnp.where(causal, 0.0, -jnp.inf).astype(F32)
    k3 = k.reshape(B, S, D)
    v3 = v.reshape(B, S, D)
    kv_spec = pl.BlockSpec((1, S, LANES), lambda b, hp, i: (b, 0, hp))
    table_spec = pl.BlockSpec((t, t), lambda b, hp, i: (0, 0))
    q_spec = pl.BlockSpec((subs * t, LANES), lambda b, hp, i: (b * steps + i, hp))
    return pl.pallas_call(
        functools.partial(_attn_kernel, t=t, subs=subs),
        grid=(B, D // LANES, steps),
        in_specs=[q_spec, kv_spec, kv_spec, table_spec, table_spec, table_spec],
        out_specs=q_spec,
        out_shape=jax.ShapeDtypeStruct((T, D), BF16),
        scratch_shapes=[pltpu.VMEM((subs, 2, t, LANES), BF16), pltpu.VMEM((subs, t, LANES), F32),
                        pltpu.VMEM((subs, 2, t, 1), F32)],
        compiler_params=_params("parallel", "parallel", "arbitrary"),
        name="sb_attention",
    )(q, k3, v3, u, ceil, bias)


def _oproj_kernel(x_ref, o_ref, w_ref, out_ref):
    out_ref[...] = x_ref[...] + _dot(o_ref[...], w_ref[...])


def _oproj(x2, o, w_o):
    T, D = x2.shape
    tm = min(OPROJ_TILE, T)
    row = pl.BlockSpec((tm, D), lambda i: (i, 0))
    return pl.pallas_call(
        _oproj_kernel,
        grid=(T // tm,),
        in_specs=[row, row, pl.BlockSpec((D, D), lambda i: (0, 0))],
        out_specs=row,
        out_shape=jax.ShapeDtypeStruct((T, D), F32),
        compiler_params=_params("parallel"),
        name="sb_oproj",
    )(x2, o, w_o.astype(BF16))


def _sb_layer(x, norm, w_qkv, q_gain, k_gain, w_o):
    B, S, D = x.shape
    x2 = x.reshape(B * S, D)
    q, k, v = _qkv(x2, norm, w_qkv, q_gain, k_gain)
    o = _attention(q, k, v, B, S)
    return _oproj(x2, o, w_o).reshape(B, S, D)


def kernel(x, pool_norm, pool_w, pool_b, pool_scale, sb_norm, sb_w_qkv, sb_q_gain, sb_k_gain, sb_w_o, moe_norm, moe_w_group, moe_b_group, moe_w_router, moe_b_router, moe_w_gate, moe_w_up, moe_w_down):
    stacked = lambda w: w.astype(BF16).reshape((DEPTH * N_EXPERTS,) + w.shape[2:])
    w_gate, w_up, w_down = stacked(moe_w_gate), stacked(moe_w_up), stacked(moe_w_down)
    for i in range(DEPTH):
        j = i // 2
        if i % 2 == 0:
            x = _pool_layer(x, pool_norm[j], pool_w[j], pool_b[j], pool_scale[j])
        else:
            x = _sb_layer(x, sb_norm[j], sb_w_qkv[j], sb_q_gain[j], sb_k_gain[j], sb_w_o[j])
        x = _moe_layer(x, moe_norm[i], moe_w_group[i], moe_b_group[i], moe_w_router[i], moe_b_router[i],
                       w_gate, w_up, w_down, layer=i)
    return x
```

```python
import functools

import jax
import jax.numpy as jnp
from jax import lax
from jax.experimental import pallas as pl
from jax.experimental.pallas import tpu as pltpu

F32 = jnp.float32
BF16 = jnp.bfloat16

D_MODEL = 1024
DEPTH = 4
POOL_WINDOWS = (2, 4, 8, 16)
POOL_GROUP_DIM = D_MODEL // len(POOL_WINDOWS)
POOL_HALO = 16
POOL_LEAD = 8
SB_HEADS = 16
SB_HEAD_DIM = D_MODEL // SB_HEADS
N_GROUPS = 4
EXPERTS_PER_GROUP = 8
N_EXPERTS = N_GROUPS * EXPERTS_PER_GROUP
D_EXPERT = D_MODEL // 4
RMS_EPS = 1e-6
LOG2_E = 1.4426950408889634
LN_2 = 0.6931471805599453
ZERO_WEIGHT_LOG2 = 150.0

LANES = 128
DMA_PRIORITIES = 2
VMEM_LIMIT_BYTES = 56 * 1024 * 1024

POOL_TILE = 512
ROUTER_TILE = 512
INVERT_TILE = 2048
QKV_TILE = 512
ATTN_TILE = 256
ATTN_TILES_PER_STEP = 2


def _params(*sem):
    return pltpu.CompilerParams(dimension_semantics=sem, vmem_limit_bytes=VMEM_LIMIT_BYTES)


def _rms(v, gain):
    return v * lax.rsqrt(jnp.mean(v * v, axis=-1, keepdims=True) + RMS_EPS) * gain


def _dot(a, b):
    return jnp.dot(a, b, preferred_element_type=F32)


def _split_bf16(v):
    hi = v.astype(BF16)
    lo = (v - hi.astype(F32)).astype(BF16)
    return hi, lo


def _pool_kernel(xprev_ref, x_ref, norm_ref, w_ref, b_ref, scale_ref, o_ref, sums):
    i = pl.program_id(1)
    ts = x_ref.shape[1]
    x = x_ref[0]
    gain = norm_ref[...]
    h = _rms(x, gain)
    hprev = jnp.where(i > 0, _rms(xprev_ref[0], gain), 0.0)
    top, lo = POOL_LEAD, POOL_LEAD + POOL_HALO
    n = lo + ts
    sums[:, 0:top, :] = jnp.zeros((sums.shape[0], top, sums.shape[2]), F32)
    sums[0, top:lo, :] = hprev
    sums[0, lo:n, :] = h
    levels = len(POOL_WINDOWS)
    for k in range(1, levels):
        shift, c0 = 2 ** (k - 1), (k - 1) * POOL_GROUP_DIM
        sums[k, top:n, c0:] = sums[k - 1, top:n, c0:] + sums[k - 1, top - shift:n - shift, c0:]
    pos = i * ts + lax.broadcasted_iota(jnp.int32, (ts, 1), 0)
    for g, win in enumerate(POOL_WINDOWS):
        c0 = g * POOL_GROUP_DIM
        cols = slice(c0, c0 + POOL_GROUP_DIM)
        hg = h[:, cols]
        if g + 1 < levels:
            acc = sums[g + 1, lo:n, cols]
        else:
            half = win // 2
            acc = sums[g, lo:n, cols] + sums[g, lo - half:n - half, cols]
        count = jnp.minimum(pos + 1, win).astype(F32)
        diff = acc / count - hg
        y = _dot(diff.astype(BF16), w_ref[g]) + b_ref[g]
        o_ref[0, :, cols] = x[:, cols] + y * scale_ref[:, cols]


def _pool_layer(x, norm, w, b, scale):
    B, S, D = x.shape
    ts = min(POOL_TILE, S)
    halo_blocks = ts // POOL_HALO
    return pl.pallas_call(
        _pool_kernel,
        grid=(B, S // ts),
        in_specs=[
            pl.BlockSpec((1, POOL_HALO, D), lambda bi, i: (bi, jnp.maximum(i * halo_blocks - 1, 0), 0)),
            pl.BlockSpec((1, ts, D), lambda bi, i: (bi, i, 0)),
            pl.BlockSpec((1, D), lambda bi, i: (0, 0)),
            pl.BlockSpec((len(POOL_WINDOWS), POOL_GROUP_DIM, POOL_GROUP_DIM), lambda bi, i: (0, 0, 0)),
            pl.BlockSpec((len(POOL_WINDOWS), 1, POOL_GROUP_DIM), lambda bi, i: (0, 0, 0)),
            pl.BlockSpec((1, D), lambda bi, i: (0, 0)),
        ],
        out_specs=pl.BlockSpec((1, ts, D), lambda bi, i: (bi, i, 0)),
        out_shape=jax.ShapeDtypeStruct((B, S, D), F32),
        scratch_shapes=[pltpu.VMEM((len(POOL_WINDOWS), POOL_LEAD + POOL_HALO + ts, D), F32)],
        compiler_params=_params("parallel", "parallel"),
        name="pool_layer",
    )(x, x, norm.reshape(1, D), w.astype(BF16), b.reshape(len(POOL_WINDOWS), 1, POOL_GROUP_DIM),
      scale.reshape(1, D))


ROUTER_GROUP_LANE0 = N_EXPERTS
GATE_GROUP_LANE = 40
GATE_RANK_LANE = 41
ROW_WIDTH = D_MODEL + LANES
MOE_ROW_TILE = 512
MOE_PAD_ROWS = N_GROUPS * MOE_ROW_TILE


def _router_kernel(*refs, mixer_out):
    if mixer_out:
        o_ref, wo_ref, *refs = refs
    x_ref, norm_ref, whi_ref, wlo_ref, bias_ref, tri_ref, row_ref, where_ref, count_ref, run_ref = refs

    @pl.when(pl.program_id(0) == 0)
    def _():
        run_ref[...] = jnp.zeros_like(run_ref)

    x = x_ref[...]
    if mixer_out:
        x = x + _dot(o_ref[...], wo_ref[...])
    h = _rms(x, norm_ref[...])
    h_hi, h_lo = _split_bf16(h)
    w_hi = whi_ref[...]
    logits = _dot(h_hi, w_hi) + _dot(h_lo, w_hi) + _dot(h_hi, wlo_ref[...]) + bias_ref[...]

    lane = lax.broadcasted_iota(jnp.int32, logits.shape, 1).astype(F32)
    neg_inf = F32(-jnp.inf)
    no_lane = F32(4 * LANES)

    def first_argmax(v):
        m = jnp.max(v, axis=1, keepdims=True)
        idx = jnp.min(jnp.where(v == m, lane, no_lane), axis=1, keepdims=True)
        return m, idx

    is_group = (lane >= ROUTER_GROUP_LANE0) & (lane < ROUTER_GROUP_LANE0 + N_GROUPS)
    gl = jnp.where(is_group, logits, neg_inf)
    gmax, glane = first_argmax(gl)
    g_prob = 1.0 / jnp.sum(jnp.exp(gl - gmax), axis=1, keepdims=True)
    group = glane - ROUTER_GROUP_LANE0
    e0 = group * EXPERTS_PER_GROUP
    el = jnp.where((lane >= e0) & (lane < e0 + EXPERTS_PER_GROUP), logits, neg_inf)
    m1, i1 = first_argmax(el)
    m2, i2 = first_argmax(jnp.where(lane == i1, neg_inf, el))
    r = jnp.exp(m2 - m1)
    w1 = 1.0 / (1.0 + r)
    gate = jnp.where(lane == i1, g_prob * w1, 0.0) + jnp.where(lane == i2, g_prob * (r * w1), 0.0)

    in_group = jnp.where(lane == group, 1.0, 0.0)
    earlier = _dot(tri_ref[...], in_group.astype(BF16))
    run = run_ref[...]
    rank = jnp.sum(in_group * (earlier + run), axis=1, keepdims=True)
    run = run + jnp.sum(in_group, axis=0, keepdims=True)
    run_ref[...] = run
    count_ref[...] = run

    gate = jnp.where(lane == GATE_GROUP_LANE, group, jnp.where(lane == GATE_RANK_LANE, rank, gate))
    row_ref[:, :D_MODEL] = x
    row_ref[:, D_MODEL:] = gate
    where_ref[0] = gate.T[GATE_GROUP_LANE:GATE_GROUP_LANE + 8, :].astype(jnp.int32)


def _router(x2, norm, w_group, b_group, w_router, b_router, attn=None):
    T, D = x2.shape
    tm = min(ROUTER_TILE, T)
    w = jnp.zeros((D, LANES), F32)
    w = w.at[:, :N_EXPERTS].set(w_router).at[:, ROUTER_GROUP_LANE0:ROUTER_GROUP_LANE0 + N_GROUPS].set(w_group)
    w_hi = w.astype(BF16)
    w_lo = (w - w_hi.astype(F32)).astype(BF16)
    bias = jnp.zeros((1, LANES), F32)
    bias = bias.at[0, :N_EXPERTS].set(b_router).at[0, ROUTER_GROUP_LANE0:ROUTER_GROUP_LANE0 + N_GROUPS].set(b_group)
    idx = jnp.arange(tm)
    tri = (idx[None, :] < idx[:, None]).astype(BF16)
    mixer_specs, mixer_args = [], []
    if attn is not None:
        mixer_specs = [pl.BlockSpec((tm, D), lambda i: (i, 0)), pl.BlockSpec((D, D), lambda i: (0, 0))]
        mixer_args = [attn[0], attn[1].astype(BF16)]
    return pl.pallas_call(
        functools.partial(_router_kernel, mixer_out=attn is not None),
        grid=(T // tm,),
        in_specs=mixer_specs + [
            pl.BlockSpec((tm, D), lambda i: (i, 0)),
            pl.BlockSpec((1, D), lambda i: (0, 0)),
            pl.BlockSpec((D, LANES), lambda i: (0, 0)),
            pl.BlockSpec((D, LANES), lambda i: (0, 0)),
            pl.BlockSpec((1, LANES), lambda i: (0, 0)),
            pl.BlockSpec((tm, tm), lambda i: (0, 0)),
        ],
        out_specs=[pl.BlockSpec((tm, ROW_WIDTH), lambda i: (i, 0)),
                   pl.BlockSpec((1, 8, tm), lambda i: (i, 0, 0)),
                   pl.BlockSpec((1, LANES), lambda i: (0, 0))],
        out_shape=[jax.ShapeDtypeStruct((T, ROW_WIDTH), F32),
                   jax.ShapeDtypeStruct((T // tm, 8, tm), jnp.int32),
                   jax.ShapeDtypeStruct((1, LANES), F32)],
        scratch_shapes=[pltpu.VMEM((1, LANES), F32)],
        compiler_params=_params("arbitrary"),
        name="moe_router",
    )(*mixer_args, x2, norm.reshape(1, D), w_hi, w_lo, bias, tri)


def _experts_kernel(tile_group_ref, tile_rows_ref, tok_prev_ref, tok_ref, tok_next_ref, rows_ref, norm_ref,
                    wg_ref, wu_ref, wd_ref, out_ref, rbuf, obuf, in_sem, out_sem):
    r = pl.program_id(0)
    last = pl.num_programs(0) - 1
    slot = r % 2

    def gather(dst_slot, tok):
        for t in range(MOE_ROW_TILE):
            pltpu.make_async_copy(rows_ref.at[pl.ds(tok[t], 1)], rbuf.at[dst_slot, pl.ds(t, 1)],
                                  in_sem.at[dst_slot]).start(priority=t % DMA_PRIORITIES)

    def row_copy(src_slot, t, tok):
        return pltpu.make_async_copy(obuf.at[src_slot, pl.ds(t, 1)], out_ref.at[pl.ds(tok[t], 1)],
                                     out_sem.at[src_slot])

    def send_rows(n, src_slot, tok):
        def body(t, carry):
            row_copy(src_slot, t, tok).start()
            return carry
        lax.fori_loop(0, n, body, 0)

    def wait_rows(n, src_slot):
        @pl.when(n == MOE_ROW_TILE)
        def _():
            pltpu.make_async_copy(obuf.at[src_slot], obuf.at[src_slot], out_sem.at[src_slot]).wait()

        @pl.when(n < MOE_ROW_TILE)
        def _():
            def body(t, carry):
                pltpu.make_async_copy(obuf.at[src_slot, pl.ds(0, 1)], obuf.at[src_slot, pl.ds(0, 1)],
                                      out_sem.at[src_slot]).wait()
                return carry
            lax.fori_loop(0, n, body, 0)

    rows_now = tile_rows_ref[r]
    rows_prev = jnp.where(r >= 1, tile_rows_ref[jnp.maximum(r - 1, 0)], 0)

    @pl.when((r == 0) & (rows_now > 0))
    def _():
        gather(slot, tok_ref)

    @pl.when(r >= 2)
    def _():
        wait_rows(tile_rows_ref[jnp.maximum(r - 2, 0)], slot)

    rows_next = jnp.where(r < last, tile_rows_ref[jnp.minimum(r + 1, last)], 0)

    @pl.when(rows_next > 0)
    def _():
        gather(1 - slot, tok_next_ref)

    @pl.when(rows_now > 0)
    def _():
        pltpu.make_async_copy(rbuf.at[slot], rbuf.at[slot], in_sem.at[slot]).wait()

    @pl.when(rows_prev == MOE_ROW_TILE)
    def _():
        for t in range(MOE_ROW_TILE):
            row_copy(1 - slot, t, tok_prev_ref).start(priority=t % DMA_PRIORITIES)

    @pl.when(rows_prev < MOE_ROW_TILE)
    def _():
        send_rows(rows_prev, 1 - slot, tok_prev_ref)

    @pl.when(rows_now > 0)
    def _():
        x = rbuf[slot, :, :D_MODEL]
        gate = rbuf[slot, :, D_MODEL:]
        h = _rms(x, norm_ref[...]).astype(BF16)
        lane = lax.broadcasted_iota(jnp.int32, gate.shape, 1)
        e0 = tile_group_ref[r] * EXPERTS_PER_GROUP
        acc = x
        for e in range(EXPERTS_PER_GROUP):
            a = _dot(h, wg_ref[e])
            u = _dot(h, wu_ref[e])
            gcol = jnp.sum(jnp.where(lane == e0 + e, gate, 0.0), axis=1, keepdims=True)
            hid = (a * jax.nn.sigmoid(a)) * u * gcol
            acc = acc + _dot(hid.astype(BF16), wd_ref[e])
        obuf[slot] = acc

    @pl.when(r == last)
    def _():
        send_rows(rows_now, slot, tok_ref)
        wait_rows(rows_prev, 1 - slot)
        wait_rows(rows_now, slot)


def _experts(rows, tile_group, tile_rows, tok, norm, w_gate, w_up, w_down, layer):
    T = rows.shape[0]
    D = D_MODEL
    n_tiles = tok.shape[0] // MOE_ROW_TILE
    group_w = lambda r, tg, tr: (layer * N_GROUPS + tg[r], 0, 0)
    smem_tile = lambda index: pl.BlockSpec((MOE_ROW_TILE,), index, memory_space=pltpu.SMEM)
    return pl.pallas_call(
        _experts_kernel,
        grid_spec=pltpu.PrefetchScalarGridSpec(
            num_scalar_prefetch=2,
            grid=(n_tiles,),
            in_specs=[smem_tile(lambda r, tg, tr: (jnp.maximum(r - 1, 0),)),
                      smem_tile(lambda r, tg, tr: (r,)),
                      smem_tile(lambda r, tg, tr: (jnp.minimum(r + 1, n_tiles - 1),)),
                      pl.BlockSpec(memory_space=pl.ANY),
                      pl.BlockSpec((1, D), lambda r, tg, tr: (0, 0)),
                      pl.BlockSpec((EXPERTS_PER_GROUP, D, D_EXPERT), group_w),
                      pl.BlockSpec((EXPERTS_PER_GROUP, D, D_EXPERT), group_w),
                      pl.BlockSpec((EXPERTS_PER_GROUP, D_EXPERT, D), group_w)],
            out_specs=pl.BlockSpec(memory_space=pl.ANY),
            scratch_shapes=[pltpu.VMEM((2, MOE_ROW_TILE, ROW_WIDTH), F32), pltpu.VMEM((2, MOE_ROW_TILE, D), F32),
                            pltpu.SemaphoreType.DMA((2,)), pltpu.SemaphoreType.DMA((2,))],
        ),
        out_shape=jax.ShapeDtypeStruct((T, D), F32),
        compiler_params=_params("arbitrary"),
        name="moe_experts",
    )(tile_group, tile_rows, tok, tok, tok, rows, norm.reshape(1, D), w_gate.astype(BF16), w_up.astype(BF16),
      w_down.astype(BF16))


def _invert_kernel(dest_ref, tok_ref):
    i = pl.program_id(0)
    tm = dest_ref.shape[0]

    @pl.when(i == 0)
    def _():
        def clear(p, carry):
            tok_ref[p] = 0
            return carry
        lax.fori_loop(0, tok_ref.shape[0], clear, 0, unroll=8)

    for t in range(tm):
        tok_ref[dest_ref[t]] = i * tm + t


def _invert(dest, total_rows):
    T = dest.shape[0]
    tm = min(INVERT_TILE, T)
    return pl.pallas_call(
        _invert_kernel,
        grid=(T // tm,),
        in_specs=[pl.BlockSpec((tm,), lambda i: (i,), memory_space=pltpu.SMEM)],
        out_specs=pl.BlockSpec(memory_space=pltpu.SMEM),
        out_shape=jax.ShapeDtypeStruct((total_rows,), jnp.int32),
        compiler_params=_params("arbitrary"),
        name="moe_invert",
    )(dest)


def _moe_layer(x, norm, w_group, b_group, w_router, b_router, w_gate, w_up, w_down, layer=0, attn=None):
    B, S, D = x.shape
    T = B * S
    rows, where, counts = _router(x.reshape(T, D), norm, w_group, b_group, w_router, b_router, attn)
    where = where.transpose(1, 0, 2).reshape(8, T)
    group, rank = where[0], where[1]
    counts = counts[0, :N_GROUPS].astype(jnp.int32)
    tiles = (counts + MOE_ROW_TILE - 1) // MOE_ROW_TILE
    tile_end = jnp.cumsum(tiles)
    tile_first = tile_end - tiles
    dest = tile_first[group] * MOE_ROW_TILE + rank
    total_rows = T + MOE_PAD_ROWS
    tok = _invert(dest.astype(jnp.int32), total_rows)
    tile_idx = jnp.arange(total_rows // MOE_ROW_TILE)
    tile_group = jnp.minimum(jnp.sum(tile_idx[:, None] >= tile_end[None, :], axis=1), N_GROUPS - 1)
    tile_rows = jnp.clip(counts[tile_group] - (tile_idx - tile_first[tile_group]) * MOE_ROW_TILE, 0, MOE_ROW_TILE)
    tile_rows = jnp.where(tile_idx < tile_end[-1], tile_rows, 0)
    return _experts(rows, tile_group.astype(jnp.int32), tile_rows.astype(jnp.int32), tok, norm,
                    w_gate, w_up, w_down, layer).reshape(B, S, D)


def _qkv_kernel(x_ref, norm_ref, w_ref, seg_ref, segt_ref, qg_ref, kg_ref, q_ref, k_ref, v_ref):
    D = x_ref.shape[1]
    h = _rms(x_ref[...], norm_ref[...]).astype(BF16)
    qkv = _dot(h, w_ref[...])
    seg = seg_ref[...]
    segt = segt_ref[...]

    def head_norm(t, gain):
        ms = _dot((t * t).astype(BF16), seg) * (1.0 / SB_HEAD_DIM)
        r_hi, r_lo = _split_bf16(lax.rsqrt(ms + RMS_EPS))
        return t * (_dot(r_hi, segt) + _dot(r_lo, segt)) * gain

    q_ref[...] = head_norm(qkv[:, :D], qg_ref[...]).astype(BF16)
    k_ref[...] = head_norm(qkv[:, D:2 * D], kg_ref[...]).astype(BF16)
    v_ref[...] = qkv[:, 2 * D:].astype(BF16)


def _qkv(x2, norm, w_qkv, q_gain, k_gain):
    T, D = x2.shape
    tm = min(QKV_TILE, T)
    head_of = jnp.arange(D) // SB_HEAD_DIM
    seg = (head_of[:, None] == jnp.arange(LANES)[None, :]).astype(BF16)
    segt = seg.T
    qg = (jnp.tile(q_gain, SB_HEADS) * (SB_HEAD_DIM ** -0.5 * LOG2_E)).reshape(1, D)
    kg = jnp.tile(k_gain, SB_HEADS).reshape(1, D)
    row = pl.BlockSpec((tm, D), lambda i: (i, 0))
    vec = pl.BlockSpec((1, D), lambda i: (0, 0))
    return pl.pallas_call(
        _qkv_kernel,
        grid=(T // tm,),
        in_specs=[row, vec, pl.BlockSpec((D, 3 * D), lambda i: (0, 0)),
                  pl.BlockSpec((D, LANES), lambda i: (0, 0)), pl.BlockSpec((LANES, D), lambda i: (0, 0)),
                  vec, vec],
        out_specs=[row, row, row],
        out_shape=[jax.ShapeDtypeStruct((T, D), BF16)] * 3,
        compiler_params=_params("parallel"),
        name="sb_qkv",
    )(x2, norm.reshape(1, D), w_qkv.astype(BF16), seg, segt, qg, kg)


def _attn_kernel(q_ref, k_ref, v_ref, u_ref, ceil_ref, bias_ref, o_ref, qh_buf, acc_ref, rem_ref, *, t, subs):
    step = pl.program_id(2)
    lane = lax.broadcasted_iota(jnp.int32, (t, LANES), 1)
    first_head = lane < SB_HEAD_DIM
    for s in range(subs):
        q2 = q_ref[s * t:(s + 1) * t, :]
        zero = jnp.zeros_like(q2)
        qh_buf[s, 0] = jnp.where(first_head, q2, zero)
        qh_buf[s, 1] = jnp.where(first_head, zero, q2)

    def key_tile(qi, j):
        start = pl.multiple_of(jnp.maximum(qi - j, 0) * t, t)
        return k_ref[0, pl.ds(start, t), :], v_ref[0, pl.ds(start, t), :]

    def scores(s, hd, kt):
        return lax.dot_general(qh_buf[s, hd], kt, (((1,), (1,)), ((), ())), preferred_element_type=F32)

    def drop_of(z, ceil=None):
        drop = jnp.maximum(z, 0.0) + jnp.log2(1.0 + jnp.exp(jnp.abs(z) * (-LN_2)))
        log_beta = z - drop
        if ceil is not None:
            drop = jnp.minimum(drop, ceil)
        return drop, log_beta

    def suffix_sums(drop):
        return _dot(drop.astype(BF16), u_ref[...])

    def weights(log_beta, later, bias=None):
        arg = log_beta - later
        if bias is not None:
            arg = arg + bias
        return jnp.exp2(arg).astype(BF16)

    tiles = [step * subs + s for s in range(subs)]
    units = [(s, hd) for s in range(subs) for hd in range(2)]
    keys, values = [], []
    for s, qi in enumerate(tiles):
        (k0, v0), (k1, v1) = key_tile(qi, 0), key_tile(qi, 1)
        keys.append((k0, k1))
        values.append(jnp.concatenate([v0, jnp.where(qi >= 1, v1, jnp.zeros_like(v1))], axis=0))
    z = {(s, hd): [scores(s, hd, kt) for kt in keys[s]] for s, hd in units}
    dl = {un: [drop_of(z[un][0], ceil_ref[...]), drop_of(z[un][1])] for un in units}
    c = {un: [suffix_sums(dl[un][j][0]) for j in range(2)] for un in units}
    pv = {}
    for s, hd in units:
        (d0, lb0), (d1, lb1) = dl[s, hd]
        c0, c1 = c[s, hd]
        rem1 = c0[:, 0:1] + d0[:, 0:1]
        a0 = weights(lb0, c0, bias_ref[...])
        a1 = weights(lb1, c1 + rem1)
        pv[s, hd] = _dot(jnp.concatenate([a0, a1], axis=1), values[s])
        rem_ref[s, hd] = rem1 + (c1[:, 0:1] + d1[:, 0:1])
    for s in range(subs):
        acc_ref[s] = jnp.where(first_head, pv[s, 0], pv[s, 1])

    for s, qi in enumerate(tiles):
        def rem_min():
            return jnp.min(jnp.minimum(rem_ref[s, 0], rem_ref[s, 1]))

        def more(carry):
            j, smallest_rem = carry
            return (j <= qi) & (smallest_rem < ZERO_WEIGHT_LOG2)

        def sweep(carry):
            j, _ = carry
            kt, vt = key_tile(qi, j)
            pv = []
            for hd in range(2):
                drop, log_beta = drop_of(scores(s, hd, kt))
                c = suffix_sums(drop)
                rem = rem_ref[s, hd]
                pv.append(_dot(weights(log_beta, c + rem), vt))
                rem_ref[s, hd] = rem + (c[:, 0:1] + drop[:, 0:1])
            acc_ref[s] += jnp.where(first_head, pv[0], pv[1])
            return j + 1, rem_min()

        lax.while_loop(more, sweep, (jnp.int32(2), rem_min()))
        o_ref[s * t:(s + 1) * t, :] = acc_ref[s].astype(o_ref.dtype)


def _attention(q, k, v, B, S):
    T, D = q.shape
    t = min(ATTN_TILE, S)
    subs = ATTN_TILES_PER_STEP
    steps = S // (t * subs)
    idx = jnp.arange(t)
    u = (idx[:, None] > idx[None, :]).astype(BF16)
    causal = idx[None, :] < idx[:, None]
    ceil = jnp.where(causal, jnp.inf, 0.0).astype(F32)
    bias = jnp.where(causal, 0.0, -jnp.inf).astype(F32)
    k3 = k.reshape(B, S, D)
    v3 = v.reshape(B, S, D)
    kv_spec = pl.BlockSpec((1, S, LANES), lambda b, hp, i: (b, 0, hp))
    table_spec = pl.BlockSpec((t, t), lambda b, hp, i: (0, 0))
    q_spec = pl.BlockSpec((subs * t, LANES), lambda b, hp, i: (b * steps + i, hp))
    return pl.pallas_call(
        functools.partial(_attn_kernel, t=t, subs=subs),
        grid=(B, D // LANES, steps),
        in_specs=[q_spec, kv_spec, kv_spec, table_spec, table_spec, table_spec],
        out_specs=q_spec,
        out_shape=jax.ShapeDtypeStruct((T, D), BF16),
        scratch_shapes=[pltpu.VMEM((subs, 2, t, LANES), BF16), pltpu.VMEM((subs, t, LANES), F32),
                        pltpu.VMEM((subs, 2, t, 1), F32)],
        compiler_params=_params("parallel", "parallel", "arbitrary"),
        name="sb_attention",
    )(q, k3, v3, u, ceil, bias)


def _sb_mixer(x, norm, w_qkv, q_gain, k_gain):
    B, S, D = x.shape
    q, k, v = _qkv(x.reshape(B * S, D), norm, w_qkv, q_gain, k_gain)
    return _attention(q, k, v, B, S)


def kernel(x, pool_norm, pool_w, pool_b, pool_scale, sb_norm, sb_w_qkv, sb_q_gain, sb_k_gain, sb_w_o, moe_norm, moe_w_group, moe_b_group, moe_w_router, moe_b_router, moe_w_gate, moe_w_up, moe_w_down):
    stacked = lambda w: w.astype(BF16).reshape((DEPTH * N_EXPERTS,) + w.shape[2:])
    w_gate, w_up, w_down = stacked(moe_w_gate), stacked(moe_w_up), stacked(moe_w_down)
    for i in range(DEPTH):
        j = i // 2
        attn = None
        if i % 2 == 0:
            x = _pool_layer(x, pool_norm[j], pool_w[j], pool_b[j], pool_scale[j])
        else:
            attn = (_sb_mixer(x, sb_norm[j], sb_w_qkv[j], sb_q_gain[j], sb_k_gain[j]), sb_w_o[j])
        x = _moe_layer(x, moe_norm[i], moe_w_group[i], moe_b_group[i], moe_w_router[i], moe_b_router[i],
                       w_gate, w_up, w_down, layer=i, attn=attn)
    return x
```

```python
import functools

import jax
import jax.numpy as jnp
from jax import lax
from jax.experimental import pallas as pl
from jax.experimental.pallas import tpu as pltpu

F32 = jnp.float32
BF16 = jnp.bfloat16

D_MODEL = 1024
DEPTH = 4
POOL_WINDOWS = (2, 4, 8, 16)
POOL_GROUP_DIM = D_MODEL // len(POOL_WINDOWS)
POOL_HALO = 16
POOL_LEAD = 8
SB_HEADS = 16
SB_HEAD_DIM = D_MODEL // SB_HEADS
N_GROUPS = 4
EXPERTS_PER_GROUP = 8
N_EXPERTS = N_GROUPS * EXPERTS_PER_GROUP
D_EXPERT = D_MODEL // 4
RMS_EPS = 1e-6
LOG2_E = 1.4426950408889634
LN_2 = 0.6931471805599453
ZERO_WEIGHT_LOG2 = 150.0

LANES = 128
MXU_DEPTH = 256
DMA_PRIORITIES = 2
VMEM_LIMIT_BYTES = 56 * 1024 * 1024

POOL_TILE = 512
ROUTER_TILE = 512
INVERT_TILE = 2048
QKV_TILE = 512
ATTN_TILE = 256
ATTN_TILES_PER_STEP = 2


def _params(*sem):
    return pltpu.CompilerParams(dimension_semantics=sem, vmem_limit_bytes=VMEM_LIMIT_BYTES)


def _rms(v, gain):
    return v * lax.rsqrt(jnp.mean(v * v, axis=-1, keepdims=True) + RMS_EPS) * gain


def _dot(a, b):
    return jnp.dot(a, b, preferred_element_type=F32)


def _split_bf16(v):
    hi = v.astype(BF16)
    lo = (v - hi.astype(F32)).astype(BF16)
    return hi, lo


def _pool_kernel(xprev_ref, x_ref, norm_ref, w_ref, b_ref, scale_ref, o_ref, sums):
    i = pl.program_id(1)
    ts = x_ref.shape[1]
    x = x_ref[0]
    gain = norm_ref[...]
    h = _rms(x, gain)
    hprev = jnp.where(i > 0, _rms(xprev_ref[0], gain), 0.0)
    top, lo = POOL_LEAD, POOL_LEAD + POOL_HALO
    n = lo + ts
    sums[:, 0:top, :] = jnp.zeros((sums.shape[0], top, sums.shape[2]), F32)
    sums[0, top:lo, :] = hprev
    sums[0, lo:n, :] = h
    levels = len(POOL_WINDOWS)
    for k in range(1, levels):
        shift, c0 = 2 ** (k - 1), (k - 1) * POOL_GROUP_DIM
        sums[k, top:n, c0:] = sums[k - 1, top:n, c0:] + sums[k - 1, top - shift:n - shift, c0:]
    pos = i * ts + lax.broadcasted_iota(jnp.int32, (ts, 1), 0)
    for g, win in enumerate(POOL_WINDOWS):
        c0 = g * POOL_GROUP_DIM
        cols = slice(c0, c0 + POOL_GROUP_DIM)
        hg = h[:, cols]
        if g + 1 < levels:
            acc = sums[g + 1, lo:n, cols]
        else:
            half = win // 2
            acc = sums[g, lo:n, cols] + sums[g, lo - half:n - half, cols]
        count = jnp.minimum(pos + 1, win).astype(F32)
        diff = acc / count - hg
        y = _dot(diff.astype(BF16), w_ref[g]) + b_ref[g]
        o_ref[0, :, cols] = x[:, cols] + y * scale_ref[:, cols]


def _pool_layer(x, norm, w, b, scale):
    B, S, D = x.shape
    ts = min(POOL_TILE, S)
    halo_blocks = ts // POOL_HALO
    return pl.pallas_call(
        _pool_kernel,
        grid=(B, S // ts),
        in_specs=[
            pl.BlockSpec((1, POOL_HALO, D), lambda bi, i: (bi, jnp.maximum(i * halo_blocks - 1, 0), 0)),
            pl.BlockSpec((1, ts, D), lambda bi, i: (bi, i, 0)),
            pl.BlockSpec((1, D), lambda bi, i: (0, 0)),
            pl.BlockSpec((len(POOL_WINDOWS), POOL_GROUP_DIM, POOL_GROUP_DIM), lambda bi, i: (0, 0, 0)),
            pl.BlockSpec((len(POOL_WINDOWS), 1, POOL_GROUP_DIM), lambda bi, i: (0, 0, 0)),
            pl.BlockSpec((1, D), lambda bi, i: (0, 0)),
        ],
        out_specs=pl.BlockSpec((1, ts, D), lambda bi, i: (bi, i, 0)),
        out_shape=jax.ShapeDtypeStruct((B, S, D), F32),
        scratch_shapes=[pltpu.VMEM((len(POOL_WINDOWS), POOL_LEAD + POOL_HALO + ts, D), F32)],
        compiler_params=_params("parallel", "parallel"),
        name="pool_layer",
    )(x, x, norm.reshape(1, D), w.astype(BF16), b.reshape(len(POOL_WINDOWS), 1, POOL_GROUP_DIM),
      scale.reshape(1, D))


ROUTER_GROUP_ROW0 = N_EXPERTS
ROUTER_ROWS = LANES
ROUTER_USED_ROWS = 40
ROUTER_COUNT_ROWS = 16
ROW_WIDTH = D_MODEL + LANES
MOE_ROW_TILE = 512
MOE_PAD_ROWS = N_GROUPS * MOE_ROW_TILE


def _router_kernel(*refs, mixer_out):
    if mixer_out:
        o_ref, wo_ref, *refs = refs
    x_ref, norm_ref, whi_ref, wlo_ref, bias_ref, tri_ref, row_ref, where_ref, count_ref, run_ref = refs

    @pl.when(pl.program_id(0) == 0)
    def _():
        run_ref[...] = jnp.zeros_like(run_ref)

    x = x_ref[...]
    if mixer_out:
        x = x + _dot(o_ref[...], wo_ref[...])
    h = _rms(x, norm_ref[...])
    h_hi, h_lo = _split_bf16(h)
    w_hi = whi_ref[...]

    def per_token(w, t):
        return sum(lax.dot_general(w[:, c:c + MXU_DEPTH], t[:, c:c + MXU_DEPTH], (((1,), (1,)), ((), ())),
                                   preferred_element_type=F32) for c in range(0, w.shape[1], MXU_DEPTH))

    all_logits = per_token(w_hi, h_hi) + per_token(w_hi, h_lo) + per_token(wlo_ref[...], h_hi) + bias_ref[...]
    tokens = all_logits.shape[1]
    shape = (ROUTER_USED_ROWS, LANES)
    row = lax.broadcasted_iota(jnp.int32, shape, 0).astype(F32)
    all_rows = lax.broadcasted_iota(jnp.int32, (ROUTER_ROWS, LANES), 0).astype(F32)
    few = row[:ROUTER_COUNT_ROWS]
    neg_inf = F32(-jnp.inf)
    no_row = F32(4 * ROUTER_ROWS)

    def first_argmax(v):
        m = jnp.max(v, axis=0, keepdims=True)
        idx = jnp.min(jnp.where(v == m, row, no_row), axis=0, keepdims=True)
        return m, idx

    gates, groups, members = [], [], []
    for c in range(0, tokens, LANES):
        logits = all_logits[:ROUTER_USED_ROWS, c:c + LANES]
        is_group = (row >= ROUTER_GROUP_ROW0) & (row < ROUTER_GROUP_ROW0 + N_GROUPS)
        gl = jnp.where(is_group, logits, neg_inf)
        gmax, grow = first_argmax(gl)
        g_prob = 1.0 / jnp.sum(jnp.exp(gl - gmax), axis=0, keepdims=True)
        group = grow - ROUTER_GROUP_ROW0
        e0 = group * EXPERTS_PER_GROUP
        el = jnp.where((row >= e0) & (row < e0 + EXPERTS_PER_GROUP), logits, neg_inf)
        m1, i1 = first_argmax(el)
        m2, i2 = first_argmax(jnp.where(row == i1, neg_inf, el))
        r = jnp.exp(m2 - m1)
        w1 = 1.0 / (1.0 + r)
        gates.append(jnp.where(all_rows == i1, g_prob * w1, 0.0) + jnp.where(all_rows == i2, g_prob * (r * w1), 0.0))
        groups.append(group)
        members.append(jnp.where(few == group, 1.0, 0.0))

    in_group = jnp.concatenate(members, axis=1).astype(BF16)
    run = run_ref[...]
    earlier = _dot(in_group, tri_ref[...]) + run
    run = run + _dot(in_group, jnp.ones(tri_ref.shape, BF16))
    run_ref[...] = run
    count_ref[...] = run[:, :LANES]

    row_ref[:, :D_MODEL] = x
    for j, c in enumerate(range(0, tokens, LANES)):
        rank = jnp.sum(members[j] * earlier[:, c:c + LANES], axis=0, keepdims=True)
        row_ref[c:c + LANES, D_MODEL:] = gates[j].T
        where_ref[0, :, c:c + LANES] = jnp.where(few[:8] == 0.0, groups[j],
                                                 jnp.where(few[:8] == 1.0, rank, 0.0)).astype(jnp.int32)


def _router(x2, norm, w_group, b_group, w_router, b_router, attn=None):
    T, D = x2.shape
    tm = min(ROUTER_TILE, T)
    groups = slice(ROUTER_GROUP_ROW0, ROUTER_GROUP_ROW0 + N_GROUPS)
    w = jnp.zeros((ROUTER_ROWS, D), F32).at[:N_EXPERTS].set(w_router.T).at[groups].set(w_group.T)
    w_hi = w.astype(BF16)
    w_lo = (w - w_hi.astype(F32)).astype(BF16)
    bias = jnp.zeros((ROUTER_ROWS,), F32).at[:N_EXPERTS].set(b_router).at[groups].set(b_group)
    bias = jnp.broadcast_to(bias[:, None], (ROUTER_ROWS, tm))
    idx = jnp.arange(tm)
    tri = (idx[:, None] < idx[None, :]).astype(BF16)
    mixer_specs, mixer_args = [], []
    if attn is not None:
        mixer_specs = [pl.BlockSpec((tm, D), lambda i: (i, 0)), pl.BlockSpec((D, D), lambda i: (0, 0))]
        mixer_args = [attn[0], attn[1].astype(BF16)]
    return pl.pallas_call(
        functools.partial(_router_kernel, mixer_out=attn is not None),
        grid=(T // tm,),
        in_specs=mixer_specs + [
            pl.BlockSpec((tm, D), lambda i: (i, 0)),
            pl.BlockSpec((1, D), lambda i: (0, 0)),
            pl.BlockSpec((ROUTER_ROWS, D), lambda i: (0, 0)),
            pl.BlockSpec((ROUTER_ROWS, D), lambda i: (0, 0)),
            pl.BlockSpec((ROUTER_ROWS, tm), lambda i: (0, 0)),
            pl.BlockSpec((tm, tm), lambda i: (0, 0)),
        ],
        out_specs=[pl.BlockSpec((tm, ROW_WIDTH), lambda i: (i, 0)),
                   pl.BlockSpec((1, 8, tm), lambda i: (i, 0, 0)),
                   pl.BlockSpec((ROUTER_COUNT_ROWS, LANES), lambda i: (0, 0))],
        out_shape=[jax.ShapeDtypeStruct((T, ROW_WIDTH), F32),
                   jax.ShapeDtypeStruct((T // tm, 8, tm), jnp.int32),
                   jax.ShapeDtypeStruct((ROUTER_COUNT_ROWS, LANES), F32)],
        scratch_shapes=[pltpu.VMEM((ROUTER_COUNT_ROWS, tm), F32)],
        compiler_params=_params("arbitrary"),
        name="moe_router",
    )(*mixer_args, x2, norm.reshape(1, D), w_hi, w_lo, bias, tri)


def _experts_kernel(tile_group_ref, tile_rows_ref, tok_prev_ref, tok_ref, tok_next_ref, rows_ref, norm_ref,
                    wg_ref, wu_ref, wd_ref, out_ref, rbuf, obuf, in_sem, out_sem):
    r = pl.program_id(0)
    last = pl.num_programs(0) - 1
    slot = r % 2

    def gather(dst_slot, tok):
        for t in range(MOE_ROW_TILE):
            pltpu.make_async_copy(rows_ref.at[pl.ds(tok[t], 1)], rbuf.at[dst_slot, pl.ds(t, 1)],
                                  in_sem.at[dst_slot]).start(priority=t % DMA_PRIORITIES)

    def row_copy(src_slot, t, tok):
        return pltpu.make_async_copy(obuf.at[src_slot, pl.ds(t, 1)], out_ref.at[pl.ds(tok[t], 1)],
                                     out_sem.at[src_slot])

    def send_rows(n, src_slot, tok):
        def body(t, carry):
            row_copy(src_slot, t, tok).start()
            return carry
        lax.fori_loop(0, n, body, 0)

    def wait_rows(n, src_slot):
        @pl.when(n == MOE_ROW_TILE)
        def _():
            pltpu.make_async_copy(obuf.at[src_slot], obuf.at[src_slot], out_sem.at[src_slot]).wait()

        @pl.when(n < MOE_ROW_TILE)
        def _():
            def body(t, carry):
                pltpu.make_async_copy(obuf.at[src_slot, pl.ds(0, 1)], obuf.at[src_slot, pl.ds(0, 1)],
                                      out_sem.at[src_slot]).wait()
                return carry
            lax.fori_loop(0, n, body, 0)

    rows_now = tile_rows_ref[r]
    rows_prev = jnp.where(r >= 1, tile_rows_ref[jnp.maximum(r - 1, 0)], 0)

    @pl.when((r == 0) & (rows_now > 0))
    def _():
        gather(slot, tok_ref)

    @pl.when(r >= 2)
    def _():
        wait_rows(tile_rows_ref[jnp.maximum(r - 2, 0)], slot)

    rows_next = jnp.where(r < last, tile_rows_ref[jnp.minimum(r + 1, last)], 0)

    @pl.when(rows_next > 0)
    def _():
        gather(1 - slot, tok_next_ref)

    @pl.when(rows_now > 0)
    def _():
        pltpu.make_async_copy(rbuf.at[slot], rbuf.at[slot], in_sem.at[slot]).wait()

    @pl.when(rows_prev == MOE_ROW_TILE)
    def _():
        for t in range(MOE_ROW_TILE):
            row_copy(1 - slot, t, tok_prev_ref).start(priority=t % DMA_PRIORITIES)

    @pl.when(rows_prev < MOE_ROW_TILE)
    def _():
        send_rows(rows_prev, 1 - slot, tok_prev_ref)

    @pl.when(rows_now > 0)
    def _():
        x = rbuf[slot, :, :D_MODEL]
        gate = rbuf[slot, :, D_MODEL:]
        h = _rms(x, norm_ref[...]).astype(BF16)
        lane = lax.broadcasted_iota(jnp.int32, gate.shape, 1)
        e0 = tile_group_ref[r] * EXPERTS_PER_GROUP
        acc = x
        for e in range(EXPERTS_PER_GROUP):
            a = _dot(h, wg_ref[e])
            u = _dot(h, wu_ref[e])
            gcol = jnp.sum(jnp.where(lane == e0 + e, gate, 0.0), axis=1, keepdims=True)
            hid = (a * jax.nn.sigmoid(a)) * u * gcol
            acc = acc + _dot(hid.astype(BF16), wd_ref[e])
        obuf[slot] = acc

    @pl.when(r == last)
    def _():
        send_rows(rows_now, slot, tok_ref)
        wait_rows(rows_prev, 1 - slot)
        wait_rows(rows_now, slot)


def _experts(rows, tile_group, tile_rows, tok, norm, w_gate, w_up, w_down, layer):
    T = rows.shape[0]
    D = D_MODEL
    n_tiles = tok.shape[0] // MOE_ROW_TILE
    group_w = lambda r, tg, tr: (layer * N_GROUPS + tg[r], 0, 0)
    smem_tile = lambda index: pl.BlockSpec((MOE_ROW_TILE,), index, memory_space=pltpu.SMEM)
    return pl.pallas_call(
        _experts_kernel,
        grid_spec=pltpu.PrefetchScalarGridSpec(
            num_scalar_prefetch=2,
            grid=(n_tiles,),
            in_specs=[smem_tile(lambda r, tg, tr: (jnp.maximum(r - 1, 0),)),
                      smem_tile(lambda r, tg, tr: (r,)),
                      smem_tile(lambda r, tg, tr: (jnp.minimum(r + 1, n_tiles - 1),)),
                      pl.BlockSpec(memory_space=pl.ANY),
                      pl.BlockSpec((1, D), lambda r, tg, tr: (0, 0)),
                      pl.BlockSpec((EXPERTS_PER_GROUP, D, D_EXPERT), group_w),
                      pl.BlockSpec((EXPERTS_PER_GROUP, D, D_EXPERT), group_w),
                      pl.BlockSpec((EXPERTS_PER_GROUP, D_EXPERT, D), group_w)],
            out_specs=pl.BlockSpec(memory_space=pl.ANY),
            scratch_shapes=[pltpu.VMEM((2, MOE_ROW_TILE, ROW_WIDTH), F32), pltpu.VMEM((2, MOE_ROW_TILE, D), F32),
                            pltpu.SemaphoreType.DMA((2,)), pltpu.SemaphoreType.DMA((2,))],
        ),
        out_shape=jax.ShapeDtypeStruct((T, D), F32),
        compiler_params=_params("arbitrary"),
        name="moe_experts",
    )(tile_group, tile_rows, tok, tok, tok, rows, norm.reshape(1, D), w_gate.astype(BF16), w_up.astype(BF16),
      w_down.astype(BF16))


def _invert_kernel(dest_ref, tok_ref):
    i = pl.program_id(0)
    tm = dest_ref.shape[0]

    @pl.when(i == 0)
    def _():
        def clear(p, carry):
            tok_ref[p] = 0
            return carry
        lax.fori_loop(0, tok_ref.shape[0], clear, 0, unroll=8)

    for t in range(tm):
        tok_ref[dest_ref[t]] = i * tm + t


def _invert(dest, total_rows):
    T = dest.shape[0]
    tm = min(INVERT_TILE, T)
    return pl.pallas_call(
        _invert_kernel,
        grid=(T // tm,),
        in_specs=[pl.BlockSpec((tm,), lambda i: (i,), memory_space=pltpu.SMEM)],
        out_specs=pl.BlockSpec(memory_space=pltpu.SMEM),
        out_shape=jax.ShapeDtypeStruct((total_rows,), jnp.int32),
        compiler_params=_params("arbitrary"),
        name="moe_invert",
    )(dest)


def _moe_layer(x, norm, w_group, b_group, w_router, b_router, w_gate, w_up, w_down, layer=0, attn=None):
    B, S, D = x.shape
    T = B * S
    rows, where, counts = _router(x.reshape(T, D), norm, w_group, b_group, w_router, b_router, attn)
    where = where.transpose(1, 0, 2).reshape(8, T)
    group, rank = where[0], where[1]
    counts = counts[:N_GROUPS, 0].astype(jnp.int32)
    tiles = (counts + MOE_ROW_TILE - 1) // MOE_ROW_TILE
    tile_end = jnp.cumsum(tiles)
    tile_first = tile_end - tiles
    dest = tile_first[group] * MOE_ROW_TILE + rank
    total_rows = T + MOE_PAD_ROWS
    tok = _invert(dest.astype(jnp.int32), total_rows)
    tile_idx = jnp.arange(total_rows // MOE_ROW_TILE)
    tile_group = jnp.minimum(jnp.sum(tile_idx[:, None] >= tile_end[None, :], axis=1), N_GROUPS - 1)
    tile_rows = jnp.clip(counts[tile_group] - (tile_idx - tile_first[tile_group]) * MOE_ROW_TILE, 0, MOE_ROW_TILE)
    tile_rows = jnp.where(tile_idx < tile_end[-1], tile_rows, 0)
    return _experts(rows, tile_group.astype(jnp.int32), tile_rows.astype(jnp.int32), tok, norm,
                    w_gate, w_up, w_down, layer).reshape(B, S, D)


def _qkv_kernel(x_ref, norm_ref, w_ref, seg_ref, segt_ref, qg_ref, kg_ref, q_ref, k_ref, v_ref):
    D = x_ref.shape[1]
    h = _rms(x_ref[...], norm_ref[...]).astype(BF16)
    qkv = _dot(h, w_ref[...])
    seg = seg_ref[...]
    segt = segt_ref[...]

    def head_norm(t, gain):
        ms = _dot((t * t).astype(BF16), seg) * (1.0 / SB_HEAD_DIM)
        r_hi, r_lo = _split_bf16(lax.rsqrt(ms + RMS_EPS))
        return t * (_dot(r_hi, segt) + _dot(r_lo, segt)) * gain

    q_ref[...] = head_norm(qkv[:, :D], qg_ref[...]).astype(BF16)
    k_ref[...] = head_norm(qkv[:, D:2 * D], kg_ref[...]).astype(BF16)
    v_ref[...] = qkv[:, 2 * D:].astype(BF16)


def _qkv(x2, norm, w_qkv, q_gain, k_gain):
    T, D = x2.shape
    tm = min(QKV_TILE, T)
    head_of = jnp.arange(D) // SB_HEAD_DIM
    seg = (head_of[:, None] == jnp.arange(LANES)[None, :]).astype(BF16)
    segt = seg.T
    qg = (jnp.tile(q_gain, SB_HEADS) * (SB_HEAD_DIM ** -0.5 * LOG2_E)).reshape(1, D)
    kg = jnp.tile(k_gain, SB_HEADS).reshape(1, D)
    row = pl.BlockSpec((tm, D), lambda i: (i, 0))
    vec = pl.BlockSpec((1, D), lambda i: (0, 0))
    return pl.pallas_call(
        _qkv_kernel,
        grid=(T // tm,),
        in_specs=[row, vec, pl.BlockSpec((D, 3 * D), lambda i: (0, 0)),
                  pl.BlockSpec((D, LANES), lambda i: (0, 0)), pl.BlockSpec((LANES, D), lambda i: (0, 0)),
                  vec, vec],
        out_specs=[row, row, row],
        out_shape=[jax.ShapeDtypeStruct((T, D), BF16)] * 3,
        compiler_params=_params("parallel"),
        name="sb_qkv",
    )(x2, norm.reshape(1, D), w_qkv.astype(BF16), seg, segt, qg, kg)


def _attn_kernel(q_ref, k_ref, v_ref, u_ref, ceil_ref, bias_ref, o_ref, qh_buf, acc_ref, rem_ref, *, t, subs):
    step = pl.program_id(2)
    lane = lax.broadcasted_iota(jnp.int32, (t, LANES), 1)
    first_head = lane < SB_HEAD_DIM
    for s in range(subs):
        q2 = q_ref[s * t:(s + 1) * t, :]
        zero = jnp.zeros_like(q2)
        qh_buf[s, 0] = jnp.where(first_head, q2, zero)
        qh_buf[s, 1] = jnp.where(first_head, zero, q2)

    def key_tile(qi, j):
        start = pl.multiple_of(jnp.maximum(qi - j, 0) * t, t)
        return k_ref[0, pl.ds(start, t), :], v_ref[0, pl.ds(start, t), :]

    def scores(s, hd, kt):
        return lax.dot_general(qh_buf[s, hd], kt, (((1,), (1,)), ((), ())), preferred_element_type=F32)

    def drop_of(z, ceil=None):
        drop = jnp.maximum(z, 0.0) + jnp.log2(1.0 + jnp.exp(jnp.abs(z) * (-LN_2)))
        log_beta = z - drop
        if ceil is not None:
            drop = jnp.minimum(drop, ceil)
        return drop, log_beta

    def suffix_sums(drop):
        return _dot(drop.astype(BF16), u_ref[...])

    def weights(log_beta, later, bias=None):
        arg = log_beta - later
        if bias is not None:
            arg = arg + bias
        return jnp.exp2(arg).astype(BF16)

    tiles = [step * subs + s for s in range(subs)]
    units = [(s, hd) for s in range(subs) for hd in range(2)]
    keys, values = [], []
    for s, qi in enumerate(tiles):
        (k0, v0), (k1, v1) = key_tile(qi, 0), key_tile(qi, 1)
        keys.append((k0, k1))
        values.append(jnp.concatenate([v0, jnp.where(qi >= 1, v1, jnp.zeros_like(v1))], axis=0))
    z = {(s, hd): [scores(s, hd, kt) for kt in keys[s]] for s, hd in units}
    dl = {un: [drop_of(z[un][0], ceil_ref[...]), drop_of(z[un][1])] for un in units}
    c = {un: [suffix_sums(dl[un][j][0]) for j in range(2)] for un in units}
    pv = {}
    for s, hd in units:
        (d0, lb0), (d1, lb1) = dl[s, hd]
        c0, c1 = c[s, hd]
        rem1 = c0[:, 0:1] + d0[:, 0:1]
        a0 = weights(lb0, c0, bias_ref[...])
        a1 = weights(lb1, c1 + rem1)
        pv[s, hd] = _dot(jnp.concatenate([a0, a1], axis=1), values[s])
        rem_ref[s, hd] = rem1 + (c1[:, 0:1] + d1[:, 0:1])
    for s in range(subs):
        acc_ref[s] = jnp.where(first_head, pv[s, 0], pv[s, 1])

    for s, qi in enumerate(tiles):
        def rem_min():
            return jnp.min(jnp.minimum(rem_ref[s, 0], rem_ref[s, 1]))

        def more(carry):
            j, smallest_rem = carry
            return (j <= qi) & (smallest_rem < ZERO_WEIGHT_LOG2)

        def sweep(carry):
            j, _ = carry
            kt, vt = key_tile(qi, j)
            pv = []
            for hd in range(2):
                drop, log_beta = drop_of(scores(s, hd, kt))
                c = suffix_sums(drop)
                rem = rem_ref[s, hd]
                pv.append(_dot(weights(log_beta, c + rem), vt))
                rem_ref[s, hd] = rem + (c[:, 0:1] + drop[:, 0:1])
            acc_ref[s] += jnp.where(first_head, pv[0], pv[1])
            return j + 1, rem_min()

        lax.while_loop(more, sweep, (jnp.int32(2), rem_min()))
        o_ref[s * t:(s + 1) * t, :] = acc_ref[s].astype(o_ref.dtype)


def _attention(q, k, v, B, S):
    T, D = q.shape
    t = min(ATTN_TILE, S)
    subs = ATTN_TILES_PER_STEP
    steps = S // (t * subs)
    idx = jnp.arange(t)
    u = (idx[:, None] > idx[None, :]).astype(BF16)
    causal = idx[None, :] < idx[:, None]
    ceil = jnp.where(causal, jnp.inf, 0.0).astype(F32)
    bias = jnp.where(causal, 0.0, -jnp.inf).astype(F32)
    k3 = k.reshape(B, S, D)
    v3 = v.reshape(B, S, D)
    kv_spec = pl.BlockSpec((1, S, LANES), lambda b, hp, i: (b, 0, hp))
    table_spec = pl.BlockSpec((t, t), lambda b, hp, i: (0, 0))
    q_spec = pl.BlockSpec((subs * t, LANES), lambda b, hp, i: (b * steps + i, hp))
    return pl.pallas_call(
        functools.partial(_attn_kernel, t=t, subs=subs),
        grid=(B, D // LANES, steps),
        in_specs=[q_spec, kv_spec, kv_spec, table_spec, table_spec, table_spec],
        out_specs=q_spec,
        out_shape=jax.ShapeDtypeStruct((T, D), BF16),
        scratch_shapes=[pltpu.VMEM((subs, 2, t, LANES), BF16), pltpu.VMEM((subs, t, LANES), F32),
                        pltpu.VMEM((subs, 2, t, 1), F32)],
        compiler_params=_params("parallel", "parallel", "arbitrary"),
        name="sb_attention",
    )(q, k3, v3, u, ceil, bias)


def _sb_mixer(x, norm, w_qkv, q_gain, k_gain):
    B, S, D = x.shape
    q, k, v = _qkv(x.reshape(B * S, D), norm, w_qkv, q_gain, k_gain)
    return _attention(q, k, v, B, S)


def kernel(x, pool_norm, pool_w, pool_b, pool_scale, sb_norm, sb_w_qkv, sb_q_gain, sb_k_gain, sb_w_o, moe_norm, moe_w_group, moe_b_group, moe_w_router, moe_b_router, moe_w_gate, moe_w_up, moe_w_down):
    stacked = lambda w: w.astype(BF16).reshape((DEPTH * N_EXPERTS,) + w.shape[2:])
    w_gate, w_up, w_down = stacked(moe_w_gate), stacked(moe_w_up), stacked(moe_w_down)
    for i in range(DEPTH):
        j = i // 2
        attn = None
        if i % 2 == 0:
            x = _pool_layer(x, pool_norm[j], pool_w[j], pool_b[j], pool_scale[j])
        else:
            attn = (_sb_mixer(x, sb_norm[j], sb_w_qkv[j], sb_q_gain[j], sb_k_gain[j]), sb_w_o[j])
        x = _moe_layer(x, moe_norm[i], moe_w_group[i], moe_b_group[i], moe_w_router[i], moe_b_router[i],
                       w_gate, w_up, w_down, layer=i, attn=attn)
    return x
```

```python
import functools

import jax
import jax.numpy as jnp
from jax import lax
from jax.experimental import pallas as pl
from jax.experimental.pallas import tpu as pltpu

F32 = jnp.float32
BF16 = jnp.bfloat16

D_MODEL = 1024
DEPTH = 4
POOL_WINDOWS = (2, 4, 8, 16)
POOL_GROUP_DIM = D_MODEL // len(POOL_WINDOWS)
POOL_HALO = 16
POOL_LEAD = 8
SB_HEADS = 16
SB_HEAD_DIM = D_MODEL // SB_HEADS
N_GROUPS = 4
EXPERTS_PER_GROUP = 8
N_EXPERTS = N_GROUPS * EXPERTS_PER_GROUP
D_EXPERT = D_MODEL // 4
RMS_EPS = 1e-6
LOG2_E = 1.4426950408889634
LN_2 = 0.6931471805599453
ZERO_WEIGHT_LOG2 = 150.0

LANES = 128
MXU_DEPTH = 256
DMA_PRIORITIES = 2
VMEM_LIMIT_BYTES = 56 * 1024 * 1024

POOL_TILE = 512
ROUTER_TILE = 512
INVERT_TILE = 2048
QKV_TILE = 512
ATTN_TILE = 256
ATTN_TILES_PER_STEP = 2


def _params(*sem):
    return pltpu.CompilerParams(dimension_semantics=sem, vmem_limit_bytes=VMEM_LIMIT_BYTES)


def _rms(v, gain):
    return v * lax.rsqrt(jnp.mean(v * v, axis=-1, keepdims=True) + RMS_EPS) * gain


def _dot(a, b):
    return jnp.dot(a, b, preferred_element_type=F32)


def _split_bf16(v):
    hi = v.astype(BF16)
    lo = (v - hi.astype(F32)).astype(BF16)
    return hi, lo


def _pool_kernel(xprev_ref, x_ref, norm_ref, w_ref, b_ref, scale_ref, o_ref, sums):
    i = pl.program_id(1)
    ts = x_ref.shape[1]
    x = x_ref[0]
    gain = norm_ref[...]
    h = _rms(x, gain)
    hprev = jnp.where(i > 0, _rms(xprev_ref[0], gain), 0.0)
    top, lo = POOL_LEAD, POOL_LEAD + POOL_HALO
    n = lo + ts
    sums[:, 0:top, :] = jnp.zeros((sums.shape[0], top, sums.shape[2]), F32)
    sums[0, top:lo, :] = hprev
    sums[0, lo:n, :] = h
    levels = len(POOL_WINDOWS)
    for k in range(1, levels):
        shift, c0 = 2 ** (k - 1), (k - 1) * POOL_GROUP_DIM
        sums[k, top:n, c0:] = sums[k - 1, top:n, c0:] + sums[k - 1, top - shift:n - shift, c0:]
    pos = i * ts + lax.broadcasted_iota(jnp.int32, (ts, 1), 0)
    for g, win in enumerate(POOL_WINDOWS):
        c0 = g * POOL_GROUP_DIM
        cols = slice(c0, c0 + POOL_GROUP_DIM)
        hg = h[:, cols]
        if g + 1 < levels:
            acc = sums[g + 1, lo:n, cols]
        else:
            half = win // 2
            acc = sums[g, lo:n, cols] + sums[g, lo - half:n - half, cols]
        count = jnp.minimum(pos + 1, win).astype(F32)
        diff = acc / count - hg
        y = _dot(diff.astype(BF16), w_ref[g]) + b_ref[g]
        o_ref[0, :, cols] = x[:, cols] + y * scale_ref[:, cols]


def _pool_layer(x, norm, w, b, scale):
    B, S, D = x.shape
    ts = min(POOL_TILE, S)
    halo_blocks = ts // POOL_HALO
    return pl.pallas_call(
        _pool_kernel,
        grid=(B, S // ts),
        in_specs=[
            pl.BlockSpec((1, POOL_HALO, D), lambda bi, i: (bi, jnp.maximum(i * halo_blocks - 1, 0), 0)),
            pl.BlockSpec((1, ts, D), lambda bi, i: (bi, i, 0)),
            pl.BlockSpec((1, D), lambda bi, i: (0, 0)),
            pl.BlockSpec((len(POOL_WINDOWS), POOL_GROUP_DIM, POOL_GROUP_DIM), lambda bi, i: (0, 0, 0)),
            pl.BlockSpec((len(POOL_WINDOWS), 1, POOL_GROUP_DIM), lambda bi, i: (0, 0, 0)),
            pl.BlockSpec((1, D), lambda bi, i: (0, 0)),
        ],
        out_specs=pl.BlockSpec((1, ts, D), lambda bi, i: (bi, i, 0)),
        out_shape=jax.ShapeDtypeStruct((B, S, D), F32),
        scratch_shapes=[pltpu.VMEM((len(POOL_WINDOWS), POOL_LEAD + POOL_HALO + ts, D), F32)],
        compiler_params=_params("parallel", "parallel"),
        name="pool_layer",
    )(x, x, norm.reshape(1, D), w.astype(BF16), b.reshape(len(POOL_WINDOWS), 1, POOL_GROUP_DIM),
      scale.reshape(1, D))


ROUTER_GROUP_ROW0 = N_EXPERTS
ROUTER_ROWS = LANES
ROUTER_USED_ROWS = 40
ROUTER_COUNT_ROWS = 16
ROW_WIDTH = D_MODEL + LANES
MOE_ROW_TILE = 512
MOE_PAD_ROWS = N_GROUPS * MOE_ROW_TILE


def _router_kernel(*refs, mixer_out):
    if mixer_out:
        o_ref, wo_ref, *refs = refs
    x_ref, norm_ref, whi_ref, wlo_ref, bias_ref, tri_ref, row_ref, where_ref, count_ref, run_ref = refs

    @pl.when(pl.program_id(0) == 0)
    def _():
        run_ref[...] = jnp.zeros_like(run_ref)

    x = x_ref[...]
    if mixer_out:
        x = x + _dot(o_ref[...], wo_ref[...])
    h = _rms(x, norm_ref[...])
    h_hi, h_lo = _split_bf16(h)
    w_hi = whi_ref[...]

    def per_token(w, t):
        return sum(lax.dot_general(w[:, c:c + MXU_DEPTH], t[:, c:c + MXU_DEPTH], (((1,), (1,)), ((), ())),
                                   preferred_element_type=F32) for c in range(0, w.shape[1], MXU_DEPTH))

    all_logits = per_token(w_hi, h_hi) + per_token(w_hi, h_lo) + per_token(wlo_ref[...], h_hi) + bias_ref[...]
    tokens = all_logits.shape[1]
    shape = (ROUTER_USED_ROWS, LANES)
    row = lax.broadcasted_iota(jnp.int32, shape, 0).astype(F32)
    all_rows = lax.broadcasted_iota(jnp.int32, (ROUTER_ROWS, LANES), 0).astype(F32)
    few = row[:ROUTER_COUNT_ROWS]
    neg_inf = F32(-jnp.inf)
    no_row = F32(4 * ROUTER_ROWS)

    def first_argmax(v):
        m = jnp.max(v, axis=0, keepdims=True)
        idx = jnp.min(jnp.where(v == m, row, no_row), axis=0, keepdims=True)
        return m, idx

    gates, groups, members = [], [], []
    for c in range(0, tokens, LANES):
        logits = all_logits[:ROUTER_USED_ROWS, c:c + LANES]
        is_group = (row >= ROUTER_GROUP_ROW0) & (row < ROUTER_GROUP_ROW0 + N_GROUPS)
        gl = jnp.where(is_group, logits, neg_inf)
        gmax, grow = first_argmax(gl)
        g_prob = 1.0 / jnp.sum(jnp.exp(gl - gmax), axis=0, keepdims=True)
        group = grow - ROUTER_GROUP_ROW0
        e0 = group * EXPERTS_PER_GROUP
        el = jnp.where((row >= e0) & (row < e0 + EXPERTS_PER_GROUP), logits, neg_inf)
        m1, i1 = first_argmax(el)
        m2, i2 = first_argmax(jnp.where(row == i1, neg_inf, el))
        r = jnp.exp(m2 - m1)
        w1 = 1.0 / (1.0 + r)
        gates.append(jnp.where(all_rows == i1, g_prob * w1, 0.0) + jnp.where(all_rows == i2, g_prob * (r * w1), 0.0))
        groups.append(group)
        members.append(jnp.where(few == group, 1.0, 0.0))

    in_group = jnp.concatenate(members, axis=1).astype(BF16)
    run = run_ref[...]
    earlier = _dot(in_group, tri_ref[...]) + run
    run = run + _dot(in_group, jnp.ones(tri_ref.shape, BF16))
    run_ref[...] = run
    count_ref[...] = run[:, :LANES]

    row_ref[:, :D_MODEL] = x
    for j, c in enumerate(range(0, tokens, LANES)):
        rank = jnp.sum(members[j] * earlier[:, c:c + LANES], axis=0, keepdims=True)
        row_ref[c:c + LANES, D_MODEL:] = gates[j].T
        where_ref[0, :, c:c + LANES] = jnp.where(few[:8] == 0.0, groups[j],
                                                 jnp.where(few[:8] == 1.0, rank, 0.0)).astype(jnp.int32)


def _router(x2, norm, w_group, b_group, w_router, b_router, attn=None):
    T, D = x2.shape
    tm = min(ROUTER_TILE, T)
    groups = slice(ROUTER_GROUP_ROW0, ROUTER_GROUP_ROW0 + N_GROUPS)
    w = jnp.zeros((ROUTER_ROWS, D), F32).at[:N_EXPERTS].set(w_router.T).at[groups].set(w_group.T)
    w_hi = w.astype(BF16)
    w_lo = (w - w_hi.astype(F32)).astype(BF16)
    bias = jnp.zeros((ROUTER_ROWS,), F32).at[:N_EXPERTS].set(b_router).at[groups].set(b_group)
    bias = jnp.broadcast_to(bias[:, None], (ROUTER_ROWS, tm))
    idx = jnp.arange(tm)
    tri = (idx[:, None] < idx[None, :]).astype(BF16)
    mixer_specs, mixer_args = [], []
    if attn is not None:
        mixer_specs = [pl.BlockSpec((tm, D), lambda i: (i, 0)), pl.BlockSpec((D, D), lambda i: (0, 0))]
        mixer_args = [attn[0], attn[1].astype(BF16)]
    return pl.pallas_call(
        functools.partial(_router_kernel, mixer_out=attn is not None),
        grid=(T // tm,),
        in_specs=mixer_specs + [
            pl.BlockSpec((tm, D), lambda i: (i, 0)),
            pl.BlockSpec((1, D), lambda i: (0, 0)),
            pl.BlockSpec((ROUTER_ROWS, D), lambda i: (0, 0)),
            pl.BlockSpec((ROUTER_ROWS, D), lambda i: (0, 0)),
            pl.BlockSpec((ROUTER_ROWS, tm), lambda i: (0, 0)),
            pl.BlockSpec((tm, tm), lambda i: (0, 0)),
        ],
        out_specs=[pl.BlockSpec((tm, ROW_WIDTH), lambda i: (i, 0)),
                   pl.BlockSpec((1, 8, tm), lambda i: (i, 0, 0)),
                   pl.BlockSpec((ROUTER_COUNT_ROWS, LANES), lambda i: (0, 0))],
        out_shape=[jax.ShapeDtypeStruct((T, ROW_WIDTH), F32),
                   jax.ShapeDtypeStruct((T // tm, 8, tm), jnp.int32),
                   jax.ShapeDtypeStruct((ROUTER_COUNT_ROWS, LANES), F32)],
        scratch_shapes=[pltpu.VMEM((ROUTER_COUNT_ROWS, tm), F32)],
        compiler_params=_params("arbitrary"),
        name="moe_router",
    )(*mixer_args, x2, norm.reshape(1, D), w_hi, w_lo, bias, tri)


def _experts_kernel(tile_group_ref, tile_rows_ref, tok_prev_ref, tok_ref, tok_next_ref, rows_ref, norm_ref,
                    wg_ref, wu_ref, wd_ref, out_ref, rbuf, obuf, in_sem, out_sem):
    r = pl.program_id(0)
    last = pl.num_programs(0) - 1
    slot = r % 2

    def gather_row(dst_slot, t, tok):
        return pltpu.make_async_copy(rows_ref.at[pl.ds(tok[t], 1)], rbuf.at[dst_slot, pl.ds(t, 1)],
                                     in_sem.at[dst_slot])

    def gather(dst_slot, tok):
        for t in range(MOE_ROW_TILE):
            gather_row(dst_slot, t, tok).start(priority=t % DMA_PRIORITIES)

    def row_copy(src_slot, t, tok):
        return pltpu.make_async_copy(obuf.at[src_slot, pl.ds(t, 1)], out_ref.at[pl.ds(tok[t], 1)],
                                     out_sem.at[src_slot])

    def send_rows(n, src_slot, tok):
        def body(t, carry):
            row_copy(src_slot, t, tok).start()
            return carry
        lax.fori_loop(0, n, body, 0)

    def wait_rows(n, src_slot):
        @pl.when(n == MOE_ROW_TILE)
        def _():
            pltpu.make_async_copy(obuf.at[src_slot], obuf.at[src_slot], out_sem.at[src_slot]).wait()

        @pl.when(n < MOE_ROW_TILE)
        def _():
            def body(t, carry):
                pltpu.make_async_copy(obuf.at[src_slot, pl.ds(0, 1)], obuf.at[src_slot, pl.ds(0, 1)],
                                      out_sem.at[src_slot]).wait()
                return carry
            lax.fori_loop(0, n, body, 0)

    rows_now = tile_rows_ref[r]
    rows_prev = jnp.where(r >= 1, tile_rows_ref[jnp.maximum(r - 1, 0)], 0)

    @pl.when((r == 0) & (rows_now > 0))
    def _():
        gather(slot, tok_ref)

    @pl.when(r >= 2)
    def _():
        wait_rows(tile_rows_ref[jnp.maximum(r - 2, 0)], slot)

    rows_next = jnp.where(r < last, tile_rows_ref[jnp.minimum(r + 1, last)], 0)

    both_ways = (rows_next > 0) & (rows_prev == MOE_ROW_TILE)

    @pl.when(both_ways)
    def _():
        for t in range(MOE_ROW_TILE):
            gather_row(1 - slot, t, tok_next_ref).start(priority=0)
            row_copy(1 - slot, t, tok_prev_ref).start(priority=1)

    @pl.when(jnp.logical_not(both_ways))
    def _():
        @pl.when(rows_next > 0)
        def _():
            gather(1 - slot, tok_next_ref)

        @pl.when(rows_prev == MOE_ROW_TILE)
        def _():
            for t in range(MOE_ROW_TILE):
                row_copy(1 - slot, t, tok_prev_ref).start(priority=t % DMA_PRIORITIES)

        @pl.when(rows_prev < MOE_ROW_TILE)
        def _():
            send_rows(rows_prev, 1 - slot, tok_prev_ref)

    @pl.when(rows_now > 0)
    def _():
        pltpu.make_async_copy(rbuf.at[slot], rbuf.at[slot], in_sem.at[slot]).wait()

    @pl.when(rows_now > 0)
    def _():
        x = rbuf[slot, :, :D_MODEL]
        gate = rbuf[slot, :, D_MODEL:]
        h = _rms(x, norm_ref[...]).astype(BF16)
        lane = lax.broadcasted_iota(jnp.int32, gate.shape, 1)
        e0 = tile_group_ref[r] * EXPERTS_PER_GROUP
        acc = x
        for e in range(EXPERTS_PER_GROUP):
            a = _dot(h, wg_ref[e])
            u = _dot(h, wu_ref[e])
            gcol = jnp.sum(jnp.where(lane == e0 + e, gate, 0.0), axis=1, keepdims=True)
            hid = (a * jax.nn.sigmoid(a)) * u * gcol
            acc = acc + _dot(hid.astype(BF16), wd_ref[e])
        obuf[slot] = acc

    @pl.when(r == last)
    def _():
        send_rows(rows_now, slot, tok_ref)
        wait_rows(rows_prev, 1 - slot)
        wait_rows(rows_now, slot)


def _experts(rows, tile_group, tile_rows, tok, norm, w_gate, w_up, w_down, layer):
    T = rows.shape[0]
    D = D_MODEL
    n_tiles = tok.shape[0] // MOE_ROW_TILE
    group_w = lambda r, tg, tr: (layer * N_GROUPS + tg[r], 0, 0)
    smem_tile = lambda index: pl.BlockSpec((MOE_ROW_TILE,), index, memory_space=pltpu.SMEM)
    return pl.pallas_call(
        _experts_kernel,
        grid_spec=pltpu.PrefetchScalarGridSpec(
            num_scalar_prefetch=2,
            grid=(n_tiles,),
            in_specs=[smem_tile(lambda r, tg, tr: (jnp.maximum(r - 1, 0),)),
                      smem_tile(lambda r, tg, tr: (r,)),
                      smem_tile(lambda r, tg, tr: (jnp.minimum(r + 1, n_tiles - 1),)),
                      pl.BlockSpec(memory_space=pl.ANY),
                      pl.BlockSpec((1, D), lambda r, tg, tr: (0, 0)),
                      pl.BlockSpec((EXPERTS_PER_GROUP, D, D_EXPERT), group_w),
                      pl.BlockSpec((EXPERTS_PER_GROUP, D, D_EXPERT), group_w),
                      pl.BlockSpec((EXPERTS_PER_GROUP, D_EXPERT, D), group_w)],
            out_specs=pl.BlockSpec(memory_space=pl.ANY),
            scratch_shapes=[pltpu.VMEM((2, MOE_ROW_TILE, ROW_WIDTH), F32), pltpu.VMEM((2, MOE_ROW_TILE, D), F32),
                            pltpu.SemaphoreType.DMA((2,)), pltpu.SemaphoreType.DMA((2,))],
        ),
        out_shape=jax.ShapeDtypeStruct((T, D), F32),
        compiler_params=_params("arbitrary"),
        name="moe_experts",
    )(tile_group, tile_rows, tok, tok, tok, rows, norm.reshape(1, D), w_gate.astype(BF16), w_up.astype(BF16),
      w_down.astype(BF16))


def _invert_kernel(pad_lo_ref, pad_hi_ref, dest_ref, tok_ref):
    i = pl.program_id(0)
    tm = dest_ref.shape[0]

    @pl.when(i == 0)
    def _():
        for seg in range(N_GROUPS + 1):
            def clear(p, carry):
                tok_ref[p] = 0
                return carry
            lax.fori_loop(pad_lo_ref[seg], pad_hi_ref[seg], clear, 0)

    for t in range(tm):
        tok_ref[dest_ref[t]] = i * tm + t


def _invert(dest, pad_lo, pad_hi, total_rows):
    T = dest.shape[0]
    tm = min(INVERT_TILE, T)
    return pl.pallas_call(
        _invert_kernel,
        grid_spec=pltpu.PrefetchScalarGridSpec(
            num_scalar_prefetch=2,
            grid=(T // tm,),
            in_specs=[pl.BlockSpec((tm,), lambda i, lo, hi: (i,), memory_space=pltpu.SMEM)],
            out_specs=pl.BlockSpec(memory_space=pltpu.SMEM),
        ),
        out_shape=jax.ShapeDtypeStruct((total_rows,), jnp.int32),
        compiler_params=_params("arbitrary"),
        name="moe_invert",
    )(pad_lo, pad_hi, dest)


def _moe_layer(x, norm, w_group, b_group, w_router, b_router, w_gate, w_up, w_down, layer=0, attn=None):
    B, S, D = x.shape
    T = B * S
    rows, where, counts = _router(x.reshape(T, D), norm, w_group, b_group, w_router, b_router, attn)
    where = where.transpose(1, 0, 2).reshape(8, T)
    group, rank = where[0], where[1]
    counts = counts[:N_GROUPS, 0].astype(jnp.int32)
    tiles = (counts + MOE_ROW_TILE - 1) // MOE_ROW_TILE
    tile_end = jnp.cumsum(tiles)
    tile_first = tile_end - tiles
    dest = tile_first[group] * MOE_ROW_TILE + rank
    total_rows = T + MOE_PAD_ROWS
    pad_lo = jnp.concatenate([tile_first * MOE_ROW_TILE + counts, tile_end[-1:] * MOE_ROW_TILE]).astype(jnp.int32)
    pad_hi = jnp.concatenate([tile_end * MOE_ROW_TILE, jnp.full((1,), total_rows)]).astype(jnp.int32)
    tok = _invert(dest.astype(jnp.int32), pad_lo, pad_hi, total_rows)
    tile_idx = jnp.arange(total_rows // MOE_ROW_TILE)
    tile_group = jnp.minimum(jnp.sum(tile_idx[:, None] >= tile_end[None, :], axis=1), N_GROUPS - 1)
    tile_rows = jnp.clip(counts[tile_group] - (tile_idx - tile_first[tile_group]) * MOE_ROW_TILE, 0, MOE_ROW_TILE)
    tile_rows = jnp.where(tile_idx < tile_end[-1], tile_rows, 0)
    return _experts(rows, tile_group.astype(jnp.int32), tile_rows.astype(jnp.int32), tok, norm,
                    w_gate, w_up, w_down, layer).reshape(B, S, D)


def _qkv_kernel(x_ref, norm_ref, w_ref, seg_ref, segt_ref, qg_ref, kg_ref, q_ref, k_ref, v_ref):
    D = x_ref.shape[1]
    h = _rms(x_ref[...], norm_ref[...]).astype(BF16)
    qkv = _dot(h, w_ref[...])
    seg = seg_ref[...]
    segt = segt_ref[...]

    def head_norm(t, gain):
        ms = _dot((t * t).astype(BF16), seg) * (1.0 / SB_HEAD_DIM)
        r_hi, r_lo = _split_bf16(lax.rsqrt(ms + RMS_EPS))
        return t * (_dot(r_hi, segt) + _dot(r_lo, segt)) * gain

    q_ref[...] = head_norm(qkv[:, :D], qg_ref[...]).astype(BF16)
    k_ref[...] = head_norm(qkv[:, D:2 * D], kg_ref[...]).astype(BF16)
    v_ref[...] = qkv[:, 2 * D:].astype(BF16)


def _qkv(x2, norm, w_qkv, q_gain, k_gain):
    T, D = x2.shape
    tm = min(QKV_TILE, T)
    head_of = jnp.arange(D) // SB_HEAD_DIM
    seg = (head_of[:, None] == jnp.arange(LANES)[None, :]).astype(BF16)
    segt = seg.T
    qg = (jnp.tile(q_gain, SB_HEADS) * (SB_HEAD_DIM ** -0.5 * LOG2_E)).reshape(1, D)
    kg = jnp.tile(k_gain, SB_HEADS).reshape(1, D)
    row = pl.BlockSpec((tm, D), lambda i: (i, 0))
    vec = pl.BlockSpec((1, D), lambda i: (0, 0))
    return pl.pallas_call(
        _qkv_kernel,
        grid=(T // tm,),
        in_specs=[row, vec, pl.BlockSpec((D, 3 * D), lambda i: (0, 0)),
                  pl.BlockSpec((D, LANES), lambda i: (0, 0)), pl.BlockSpec((LANES, D), lambda i: (0, 0)),
                  vec, vec],
        out_specs=[row, row, row],
        out_shape=[jax.ShapeDtypeStruct((T, D), BF16)] * 3,
        compiler_params=_params("parallel"),
        name="sb_qkv",
    )(x2, norm.reshape(1, D), w_qkv.astype(BF16), seg, segt, qg, kg)


def _attn_kernel(q_ref, k_ref, v_ref, u_ref, ceil_ref, bias_ref, o_ref, qh_buf, acc_ref, rem_ref, *, t, subs):
    step = pl.program_id(2)
    lane = lax.broadcasted_iota(jnp.int32, (t, LANES), 1)
    first_head = lane < SB_HEAD_DIM
    for s in range(subs):
        q2 = q_ref[s * t:(s + 1) * t, :]
        zero = jnp.zeros_like(q2)
        qh_buf[s, 0] = jnp.where(first_head, q2, zero)
        qh_buf[s, 1] = jnp.where(first_head, zero, q2)

    def key_tile(qi, j):
        start = pl.multiple_of(jnp.maximum(qi - j, 0) * t, t)
        return k_ref[0, pl.ds(start, t), :], v_ref[0, pl.ds(start, t), :]

    def scores(s, hd, kt):
        return lax.dot_general(qh_buf[s, hd], kt, (((1,), (1,)), ((), ())), preferred_element_type=F32)

    def drop_of(z, ceil=None):
        drop = jnp.maximum(z, 0.0) + jnp.log2(1.0 + jnp.exp(jnp.abs(z) * (-LN_2)))
        log_beta = z - drop
        if ceil is not None:
            drop = jnp.minimum(drop, ceil)
        return drop, log_beta

    def suffix_sums(drop):
        return _dot(drop.astype(BF16), u_ref[...])

    def weights(log_beta, later, bias=None):
        arg = log_beta - later
        if bias is not None:
            arg = arg + bias
        return jnp.exp2(arg).astype(BF16)

    tiles = [step * subs + s for s in range(subs)]
    units = [(s, hd) for s in range(subs) for hd in range(2)]
    keys, values = [], []
    for s, qi in enumerate(tiles):
        (k0, v0), (k1, v1) = key_tile(qi, 0), key_tile(qi, 1)
        keys.append((k0, k1))
        values.append(jnp.concatenate([v0, jnp.where(qi >= 1, v1, jnp.zeros_like(v1))], axis=0))
    z = {(s, hd): [scores(s, hd, kt) for kt in keys[s]] for s, hd in units}
    dl = {un: [drop_of(z[un][0], ceil_ref[...]), drop_of(z[un][1])] for un in units}
    c = {un: [suffix_sums(dl[un][j][0]) for j in range(2)] for un in units}
    pv = {}
    for s, hd in units:
        (d0, lb0), (d1, lb1) = dl[s, hd]
        c0, c1 = c[s, hd]
        rem1 = c0[:, 0:1] + d0[:, 0:1]
        a0 = weights(lb0, c0, bias_ref[...])
        a1 = weights(lb1, c1 + rem1)
        pv[s, hd] = _dot(jnp.concatenate([a0, a1], axis=1), values[s])
        rem_ref[s, hd] = rem1 + (c1[:, 0:1] + d1[:, 0:1])
    for s in range(subs):
        acc_ref[s] = jnp.where(first_head, pv[s, 0], pv[s, 1])

    for s, qi in enumerate(tiles):
        def rem_min():
            return jnp.min(jnp.minimum(rem_ref[s, 0], rem_ref[s, 1]))

        def more(carry):
            j, smallest_rem = carry
            return (j <= qi) & (smallest_rem < ZERO_WEIGHT_LOG2)

        def sweep(carry):
            j, _ = carry
            kt, vt = key_tile(qi, j)
            pv = []
            for hd in range(2):
                drop, log_beta = drop_of(scores(s, hd, kt))
                c = suffix_sums(drop)
                rem = rem_ref[s, hd]
                pv.append(_dot(weights(log_beta, c + rem), vt))
                rem_ref[s, hd] = rem + (c[:, 0:1] + drop[:, 0:1])
            acc_ref[s] += jnp.where(first_head, pv[0], pv[1])
            return j + 1, rem_min()

        lax.while_loop(more, sweep, (jnp.int32(2), rem_min()))
        o_ref[s * t:(s + 1) * t, :] = acc_ref[s].astype(o_ref.dtype)


def _attention(q, k, v, B, S):
    T, D = q.shape
    t = min(ATTN_TILE, S)
    subs = ATTN_TILES_PER_STEP
    steps = S // (t * subs)
    idx = jnp.arange(t)
    u = (idx[:, None] > idx[None, :]).astype(BF16)
    causal = idx[None, :] < idx[:, None]
    ceil = jnp.where(causal, jnp.inf, 0.0).astype(F32)
    bias = jnp.where(causal, 0.0, -jnp.inf).astype(F32)
    k3 = k.reshape(B, S, D)
    v3 = v.reshape(B, S, D)
    kv_spec = pl.BlockSpec((1, S, LANES), lambda b, hp, i: (b, 0, hp))
    table_spec = pl.BlockSpec((t, t), lambda b, hp, i: (0, 0))
    q_spec = pl.BlockSpec((subs * t, LANES), lambda b, hp, i: (b * steps + i, hp))
    return pl.pallas_call(
        functools.partial(_attn_kernel, t=t, subs=subs),
        grid=(B, D // LANES, steps),
        in_specs=[q_spec, kv_spec, kv_spec, table_spec, table_spec, table_spec],
        out_specs=q_spec,
        out_shape=jax.ShapeDtypeStruct((T, D), BF16),
        scratch_shapes=[pltpu.VMEM((subs, 2, t, LANES), BF16), pltpu.VMEM((subs, t, LANES), F32),
                        pltpu.VMEM((subs, 2, t, 1), F32)],
        compiler_params=_params("parallel", "parallel", "arbitrary"),
        name="sb_attention",
    )(q, k3, v3, u, ceil, bias)


def _sb_mixer(x, norm, w_qkv, q_gain, k_gain):
    B, S, D = x.shape
    q, k, v = _qkv(x.reshape(B * S, D), norm, w_qkv, q_gain, k_gain)
    return _attention(q, k, v, B, S)


def kernel(x, pool_norm, pool_w, pool_b, pool_scale, sb_norm, sb_w_qkv, sb_q_gain, sb_k_gain, sb_w_o, moe_norm, moe_w_group, moe_b_group, moe_w_router, moe_b_router, moe_w_gate, moe_w_up, moe_w_down):
    stacked = lambda w: w.astype(BF16).reshape((DEPTH * N_EXPERTS,) + w.shape[2:])
    w_gate, w_up, w_down = stacked(moe_w_gate), stacked(moe_w_up), stacked(moe_w_down)
    for i in range(DEPTH):
        j = i // 2
        attn = None
        if i % 2 == 0:
            x = _pool_layer(x, pool_norm[j], pool_w[j], pool_b[j], pool_scale[j])
        else:
            attn = (_sb_mixer(x, sb_norm[j], sb_w_qkv[j], sb_q_gain[j], sb_k_gain[j]), sb_w_o[j])
        x = _moe_layer(x, moe_norm[i], moe_w_group[i], moe_b_group[i], moe_w_router[i], moe_b_router[i],
                       w_gate, w_up, w_down, layer=i, attn=attn)
    return x
```

```python
import functools

import jax
import jax.numpy as jnp
from jax import lax
from jax.experimental import pallas as pl
from jax.experimental.pallas import tpu as pltpu

F32 = jnp.float32
BF16 = jnp.bfloat16

D_MODEL = 1024
DEPTH = 4
POOL_WINDOWS = (2, 4, 8, 16)
POOL_GROUP_DIM = D_MODEL // len(POOL_WINDOWS)
POOL_HALO = 16
POOL_LEAD = 8
SB_HEADS = 16
SB_HEAD_DIM = D_MODEL // SB_HEADS
N_GROUPS = 4
EXPERTS_PER_GROUP = 8
N_EXPERTS = N_GROUPS * EXPERTS_PER_GROUP
D_EXPERT = D_MODEL // 4
RMS_EPS = 1e-6
LOG2_E = 1.4426950408889634
LN_2 = 0.6931471805599453
ZERO_WEIGHT_LOG2 = 150.0

LANES = 128
MXU_DEPTH = 256
DMA_PRIORITIES = 2
VMEM_LIMIT_BYTES = 56 * 1024 * 1024

POOL_TILE = 1024
ROUTER_TILE = 1024
INVERT_TILE = 2048
QKV_TILE = 512
ATTN_TILE = 256
ATTN_TILES_PER_STEP = 4


def _params(*sem):
    return pltpu.CompilerParams(dimension_semantics=sem, vmem_limit_bytes=VMEM_LIMIT_BYTES)


def _rms(v, gain):
    return v * lax.rsqrt(jnp.mean(v * v, axis=-1, keepdims=True) + RMS_EPS) * gain


def _dot(a, b):
    return jnp.dot(a, b, preferred_element_type=F32)


def _split_bf16(v):
    hi = v.astype(BF16)
    lo = (v - hi.astype(F32)).astype(BF16)
    return hi, lo


def _pool_kernel(xprev_ref, x_ref, norm_ref, w_ref, b_ref, scale_ref, o_ref, sums):
    i = pl.program_id(1)
    ts = x_ref.shape[1]
    x = x_ref[0]
    gain = norm_ref[...]
    h = _rms(x, gain)
    hprev = jnp.where(i > 0, _rms(xprev_ref[0], gain), 0.0)
    top, lo = POOL_LEAD, POOL_LEAD + POOL_HALO
    n = lo + ts
    sums[:, 0:top, :] = jnp.zeros((sums.shape[0], top, sums.shape[2]), F32)
    sums[0, top:lo, :] = hprev
    sums[0, lo:n, :] = h
    levels = len(POOL_WINDOWS)
    for k in range(1, levels):
        shift, c0 = 2 ** (k - 1), (k - 1) * POOL_GROUP_DIM
        sums[k, top:n, c0:] = sums[k - 1, top:n, c0:] + sums[k - 1, top - shift:n - shift, c0:]
    pos = i * ts + lax.broadcasted_iota(jnp.int32, (ts, 1), 0)
    for g, win in enumerate(POOL_WINDOWS):
        c0 = g * POOL_GROUP_DIM
        cols = slice(c0, c0 + POOL_GROUP_DIM)
        hg = h[:, cols]
        if g + 1 < levels:
            acc = sums[g + 1, lo:n, cols]
        else:
            half = win // 2
            acc = sums[g, lo:n, cols] + sums[g, lo - half:n - half, cols]
        count = jnp.minimum(pos + 1, win).astype(F32)
        diff = acc / count - hg
        y = _dot(diff.astype(BF16), w_ref[g]) + b_ref[g]
        o_ref[0, :, cols] = x[:, cols] + y * scale_ref[:, cols]


def _pool_layer(x, norm, w, b, scale):
    B, S, D = x.shape
    ts = min(POOL_TILE, S)
    halo_blocks = ts // POOL_HALO
    return pl.pallas_call(
        _pool_kernel,
        grid=(B, S // ts),
        in_specs=[
            pl.BlockSpec((1, POOL_HALO, D), lambda bi, i: (bi, jnp.maximum(i * halo_blocks - 1, 0), 0)),
            pl.BlockSpec((1, ts, D), lambda bi, i: (bi, i, 0)),
            pl.BlockSpec((1, D), lambda bi, i: (0, 0)),
            pl.BlockSpec((len(POOL_WINDOWS), POOL_GROUP_DIM, POOL_GROUP_DIM), lambda bi, i: (0, 0, 0)),
            pl.BlockSpec((len(POOL_WINDOWS), 1, POOL_GROUP_DIM), lambda bi, i: (0, 0, 0)),
            pl.BlockSpec((1, D), lambda bi, i: (0, 0)),
        ],
        out_specs=pl.BlockSpec((1, ts, D), lambda bi, i: (bi, i, 0)),
        out_shape=jax.ShapeDtypeStruct((B, S, D), F32),
        scratch_shapes=[pltpu.VMEM((len(POOL_WINDOWS), POOL_LEAD + POOL_HALO + ts, D), F32)],
        compiler_params=_params("parallel", "parallel"),
        name="pool_layer",
    )(x, x, norm.reshape(1, D), w.astype(BF16), b.reshape(len(POOL_WINDOWS), 1, POOL_GROUP_DIM),
      scale.reshape(1, D))


ROUTER_GROUP_ROW0 = N_EXPERTS
ROUTER_ROWS = LANES
ROUTER_USED_ROWS = 40
ROUTER_COUNT_ROWS = 16
ROW_WIDTH = D_MODEL + LANES
MOE_ROW_TILE = 512
MOE_PAD_ROWS = N_GROUPS * MOE_ROW_TILE


def _router_kernel(*refs, mixer_out):
    if mixer_out:
        o_ref, wo_ref, *refs = refs
    x_ref, norm_ref, whi_ref, wlo_ref, bias_ref, tri_ref, row_ref, where_ref, count_ref, run_ref = refs

    @pl.when(pl.program_id(0) == 0)
    def _():
        run_ref[...] = jnp.zeros_like(run_ref)

    x = x_ref[...]
    if mixer_out:
        x = x + _dot(o_ref[...], wo_ref[...])
    h = _rms(x, norm_ref[...])
    h_hi, h_lo = _split_bf16(h)
    w_hi = whi_ref[...]

    def per_token(w, t):
        return sum(lax.dot_general(w[:, c:c + MXU_DEPTH], t[:, c:c + MXU_DEPTH], (((1,), (1,)), ((), ())),
                                   preferred_element_type=F32) for c in range(0, w.shape[1], MXU_DEPTH))

    all_logits = per_token(w_hi, h_hi) + per_token(w_hi, h_lo) + per_token(wlo_ref[...], h_hi) + bias_ref[...]
    tokens = all_logits.shape[1]
    shape = (ROUTER_USED_ROWS, LANES)
    row = lax.broadcasted_iota(jnp.int32, shape, 0).astype(F32)
    all_rows = lax.broadcasted_iota(jnp.int32, (ROUTER_ROWS, LANES), 0).astype(F32)
    few = row[:ROUTER_COUNT_ROWS]
    neg_inf = F32(-jnp.inf)
    no_row = F32(4 * ROUTER_ROWS)

    def first_argmax(v):
        m = jnp.max(v, axis=0, keepdims=True)
        idx = jnp.min(jnp.where(v == m, row, no_row), axis=0, keepdims=True)
        return m, idx

    gates, groups, members = [], [], []
    for c in range(0, tokens, LANES):
        logits = all_logits[:ROUTER_USED_ROWS, c:c + LANES]
        is_group = (row >= ROUTER_GROUP_ROW0) & (row < ROUTER_GROUP_ROW0 + N_GROUPS)
        gl = jnp.where(is_group, logits, neg_inf)
        gmax, grow = first_argmax(gl)
        g_prob = 1.0 / jnp.sum(jnp.exp(gl - gmax), axis=0, keepdims=True)
        group = grow - ROUTER_GROUP_ROW0
        e0 = group * EXPERTS_PER_GROUP
        el = jnp.where((row >= e0) & (row < e0 + EXPERTS_PER_GROUP), logits, neg_inf)
        m1, i1 = first_argmax(el)
        m2, i2 = first_argmax(jnp.where(row == i1, neg_inf, el))
        r = jnp.exp(m2 - m1)
        w1 = 1.0 / (1.0 + r)
        gates.append(jnp.where(all_rows == i1, g_prob * w1, 0.0) + jnp.where(all_rows == i2, g_prob * (r * w1), 0.0))
        groups.append(group)
        members.append(jnp.where(few == group, 1.0, 0.0))

    in_group = jnp.concatenate(members, axis=1).astype(BF16)
    run = run_ref[...]
    earlier = _dot(in_group, tri_ref[...]) + run
    run = run + _dot(in_group, jnp.ones(tri_ref.shape, BF16))
    run_ref[...] = run
    count_ref[...] = run[:, :LANES]

    row_ref[:, :D_MODEL] = x
    for j, c in enumerate(range(0, tokens, LANES)):
        rank = jnp.sum(members[j] * earlier[:, c:c + LANES], axis=0, keepdims=True)
        row_ref[c:c + LANES, D_MODEL:] = gates[j].T
        where_ref[0, :, c:c + LANES] = jnp.where(few[:8] == 0.0, groups[j],
                                                 jnp.where(few[:8] == 1.0, rank, 0.0)).astype(jnp.int32)


def _router(x2, norm, w_group, b_group, w_router, b_router, attn=None):
    T, D = x2.shape
    tm = min(ROUTER_TILE, T)
    groups = slice(ROUTER_GROUP_ROW0, ROUTER_GROUP_ROW0 + N_GROUPS)
    w = jnp.zeros((ROUTER_ROWS, D), F32).at[:N_EXPERTS].set(w_router.T).at[groups].set(w_group.T)
    w_hi = w.astype(BF16)
    w_lo = (w - w_hi.astype(F32)).astype(BF16)
    bias = jnp.zeros((ROUTER_ROWS,), F32).at[:N_EXPERTS].set(b_router).at[groups].set(b_group)
    bias = jnp.broadcast_to(bias[:, None], (ROUTER_ROWS, tm))
    idx = jnp.arange(tm)
    tri = (idx[:, None] < idx[None, :]).astype(BF16)
    mixer_specs, mixer_args = [], []
    if attn is not None:
        mixer_specs = [pl.BlockSpec((tm, D), lambda i: (i, 0)), pl.BlockSpec((D, D), lambda i: (0, 0))]
        mixer_args = [attn[0], attn[1].astype(BF16)]
    return pl.pallas_call(
        functools.partial(_router_kernel, mixer_out=attn is not None),
        grid=(T // tm,),
        in_specs=mixer_specs + [
            pl.BlockSpec((tm, D), lambda i: (i, 0)),
            pl.BlockSpec((1, D), lambda i: (0, 0)),
            pl.BlockSpec((ROUTER_ROWS, D), lambda i: (0, 0)),
            pl.BlockSpec((ROUTER_ROWS, D), lambda i: (0, 0)),
            pl.BlockSpec((ROUTER_ROWS, tm), lambda i: (0, 0)),
            pl.BlockSpec((tm, tm), lambda i: (0, 0)),
        ],
        out_specs=[pl.BlockSpec((tm, ROW_WIDTH), lambda i: (i, 0)),
                   pl.BlockSpec((1, 8, tm), lambda i: (i, 0, 0)),
                   pl.BlockSpec((ROUTER_COUNT_ROWS, LANES), lambda i: (0, 0))],
        out_shape=[jax.ShapeDtypeStruct((T, ROW_WIDTH), F32),
                   jax.ShapeDtypeStruct((T // tm, 8, tm), jnp.int32),
                   jax.ShapeDtypeStruct((ROUTER_COUNT_ROWS, LANES), F32)],
        scratch_shapes=[pltpu.VMEM((ROUTER_COUNT_ROWS, tm), F32)],
        compiler_params=_params("arbitrary"),
        name="moe_router",
    )(*mixer_args, x2, norm.reshape(1, D), w_hi, w_lo, bias, tri)


def _experts_kernel(tile_group_ref, tile_rows_ref, tok_prev_ref, tok_ref, tok_next_ref, rows_ref, norm_ref,
                    wg_ref, wu_ref, wd_ref, out_ref, rbuf, obuf, in_sem, out_sem):
    r = pl.program_id(0)
    last = pl.num_programs(0) - 1
    slot = r % 2

    def gather(dst_slot, tok):
        for t in range(MOE_ROW_TILE):
            pltpu.make_async_copy(rows_ref.at[pl.ds(tok[t], 1)], rbuf.at[dst_slot, pl.ds(t, 1)],
                                  in_sem.at[dst_slot]).start(priority=t % DMA_PRIORITIES)

    def row_copy(src_slot, t, tok):
        return pltpu.make_async_copy(obuf.at[src_slot, pl.ds(t, 1)], out_ref.at[pl.ds(tok[t], 1)],
                                     out_sem.at[src_slot])

    def send_rows(n, src_slot, tok):
        def body(t, carry):
            row_copy(src_slot, t, tok).start()
            return carry
        lax.fori_loop(0, n, body, 0)

    def wait_rows(n, src_slot):
        @pl.when(n == MOE_ROW_TILE)
        def _():
            pltpu.make_async_copy(obuf.at[src_slot], obuf.at[src_slot], out_sem.at[src_slot]).wait()

        @pl.when(n < MOE_ROW_TILE)
        def _():
            def body(t, carry):
                pltpu.make_async_copy(obuf.at[src_slot, pl.ds(0, 1)], obuf.at[src_slot, pl.ds(0, 1)],
                                      out_sem.at[src_slot]).wait()
                return carry
            lax.fori_loop(0, n, body, 0)

    rows_now = tile_rows_ref[r]
    rows_prev = jnp.where(r >= 1, tile_rows_ref[jnp.maximum(r - 1, 0)], 0)

    @pl.when((r == 0) & (rows_now > 0))
    def _():
        gather(slot, tok_ref)

    @pl.when(r >= 2)
    def _():
        wait_rows(tile_rows_ref[jnp.maximum(r - 2, 0)], slot)

    rows_next = jnp.where(r < last, tile_rows_ref[jnp.minimum(r + 1, last)], 0)

    @pl.when(rows_next > 0)
    def _():
        gather(1 - slot, tok_next_ref)

    @pl.when(rows_now > 0)
    def _():
        pltpu.make_async_copy(rbuf.at[slot], rbuf.at[slot], in_sem.at[slot]).wait()

    @pl.when(rows_prev == MOE_ROW_TILE)
    def _():
        for t in range(MOE_ROW_TILE):
            row_copy(1 - slot, t, tok_prev_ref).start(priority=t % DMA_PRIORITIES)

    @pl.when(rows_prev < MOE_ROW_TILE)
    def _():
        send_rows(rows_prev, 1 - slot, tok_prev_ref)

    @pl.when(rows_now > 0)
    def _():
        x = rbuf[slot, :, :D_MODEL]
        gate = rbuf[slot, :, D_MODEL:]
        h = _rms(x, norm_ref[...]).astype(BF16)
        lane = lax.broadcasted_iota(jnp.int32, gate.shape, 1)
        e0 = tile_group_ref[r] * EXPERTS_PER_GROUP
        acc = x
        for e in range(EXPERTS_PER_GROUP):
            a = _dot(h, wg_ref[e])
            u = _dot(h, wu_ref[e])
            gcol = jnp.sum(jnp.where(lane == e0 + e, gate, 0.0), axis=1, keepdims=True)
            hid = (a * jax.nn.sigmoid(a)) * u * gcol
            acc = acc + _dot(hid.astype(BF16), wd_ref[e])
        obuf[slot] = acc

    @pl.when(r == last)
    def _():
        send_rows(rows_now, slot, tok_ref)
        wait_rows(rows_prev, 1 - slot)
        wait_rows(rows_now, slot)


def _experts(rows, tile_group, tile_rows, tok, norm, w_gate, w_up, w_down, layer):
    T = rows.shape[0]
    D = D_MODEL
    n_tiles = tok.shape[0] // MOE_ROW_TILE
    group_w = lambda r, tg, tr: (layer * N_GROUPS + tg[r], 0, 0)
    smem_tile = lambda index: pl.BlockSpec((MOE_ROW_TILE,), index, memory_space=pltpu.SMEM)
    return pl.pallas_call(
        _experts_kernel,
        grid_spec=pltpu.PrefetchScalarGridSpec(
            num_scalar_prefetch=2,
            grid=(n_tiles,),
            in_specs=[smem_tile(lambda r, tg, tr: (jnp.maximum(r - 1, 0),)),
                      smem_tile(lambda r, tg, tr: (r,)),
                      smem_tile(lambda r, tg, tr: (jnp.minimum(r + 1, n_tiles - 1),)),
                      pl.BlockSpec(memory_space=pl.ANY),
                      pl.BlockSpec((1, D), lambda r, tg, tr: (0, 0)),
                      pl.BlockSpec((EXPERTS_PER_GROUP, D, D_EXPERT), group_w),
                      pl.BlockSpec((EXPERTS_PER_GROUP, D, D_EXPERT), group_w),
                      pl.BlockSpec((EXPERTS_PER_GROUP, D_EXPERT, D), group_w)],
            out_specs=pl.BlockSpec(memory_space=pl.ANY),
            scratch_shapes=[pltpu.VMEM((2, MOE_ROW_TILE, ROW_WIDTH), F32), pltpu.VMEM((2, MOE_ROW_TILE, D), F32),
                            pltpu.SemaphoreType.DMA((2,)), pltpu.SemaphoreType.DMA((2,))],
        ),
        out_shape=jax.ShapeDtypeStruct((T, D), F32),
        compiler_params=_params("arbitrary"),
        name="moe_experts",
    )(tile_group, tile_rows, tok, tok, tok, rows, norm.reshape(1, D), w_gate.astype(BF16), w_up.astype(BF16),
      w_down.astype(BF16))


def _invert_kernel(pad_lo_ref, pad_hi_ref, dest_ref, tok_ref):
    i = pl.program_id(0)
    tm = dest_ref.shape[0]

    @pl.when(i == 0)
    def _():
        for seg in range(N_GROUPS + 1):
            def clear(p, carry):
                tok_ref[p] = 0
                return carry
            lax.fori_loop(pad_lo_ref[seg], pad_hi_ref[seg], clear, 0)

    for t in range(tm):
        tok_ref[dest_ref[t]] = i * tm + t


def _invert(dest, pad_lo, pad_hi, total_rows):
    T = dest.shape[0]
    tm = min(INVERT_TILE, T)
    return pl.pallas_call(
        _invert_kernel,
        grid_spec=pltpu.PrefetchScalarGridSpec(
            num_scalar_prefetch=2,
            grid=(T // tm,),
            in_specs=[pl.BlockSpec((tm,), lambda i, lo, hi: (i,), memory_space=pltpu.SMEM)],
            out_specs=pl.BlockSpec(memory_space=pltpu.SMEM),
        ),
        out_shape=jax.ShapeDtypeStruct((total_rows,), jnp.int32),
        compiler_params=_params("arbitrary"),
        name="moe_invert",
    )(pad_lo, pad_hi, dest)


def _moe_layer(x, norm, w_group, b_group, w_router, b_router, w_gate, w_up, w_down, layer=0, attn=None):
    B, S, D = x.shape
    T = B * S
    rows, where, counts = _router(x.reshape(T, D), norm, w_group, b_group, w_router, b_router, attn)
    where = where.transpose(1, 0, 2).reshape(8, T)
    group, rank = where[0], where[1]
    counts = counts[:N_GROUPS, 0].astype(jnp.int32)
    tiles = (counts + MOE_ROW_TILE - 1) // MOE_ROW_TILE
    tile_end = jnp.cumsum(tiles)
    tile_first = tile_end - tiles
    dest = tile_first[group] * MOE_ROW_TILE + rank
    total_rows = T + MOE_PAD_ROWS
    pad_lo = jnp.concatenate([tile_first * MOE_ROW_TILE + counts, tile_end[-1:] * MOE_ROW_TILE]).astype(jnp.int32)
    pad_hi = jnp.concatenate([tile_end * MOE_ROW_TILE, jnp.full((1,), total_rows)]).astype(jnp.int32)
    tok = _invert(dest.astype(jnp.int32), pad_lo, pad_hi, total_rows)
    tile_idx = jnp.arange(total_rows // MOE_ROW_TILE)
    tile_group = jnp.minimum(jnp.sum(tile_idx[:, None] >= tile_end[None, :], axis=1), N_GROUPS - 1)
    tile_rows = jnp.clip(counts[tile_group] - (tile_idx - tile_first[tile_group]) * MOE_ROW_TILE, 0, MOE_ROW_TILE)
    tile_rows = jnp.where(tile_idx < tile_end[-1], tile_rows, 0)
    return _experts(rows, tile_group.astype(jnp.int32), tile_rows.astype(jnp.int32), tok, norm,
                    w_gate, w_up, w_down, layer).reshape(B, S, D)


def _qkv_kernel(x_ref, norm_ref, w_ref, seg_ref, segt_ref, qg_ref, kg_ref, q_ref, k_ref, v_ref):
    D = x_ref.shape[1]
    h = _rms(x_ref[...], norm_ref[...]).astype(BF16)
    qkv = _dot(h, w_ref[...])
    seg = seg_ref[...]
    segt = segt_ref[...]

    def head_norm(t, gain):
        ms = _dot((t * t).astype(BF16), seg) * (1.0 / SB_HEAD_DIM)
        r_hi, r_lo = _split_bf16(lax.rsqrt(ms + RMS_EPS))
        return t * (_dot(r_hi, segt) + _dot(r_lo, segt)) * gain

    q_ref[...] = head_norm(qkv[:, :D], qg_ref[...]).astype(BF16)
    k_ref[...] = head_norm(qkv[:, D:2 * D], kg_ref[...]).astype(BF16)
    v_ref[...] = qkv[:, 2 * D:].astype(BF16)


def _qkv(x2, norm, w_qkv, q_gain, k_gain):
    T, D = x2.shape
    tm = min(QKV_TILE, T)
    head_of = jnp.arange(D) // SB_HEAD_DIM
    seg = (head_of[:, None] == jnp.arange(LANES)[None, :]).astype(BF16)
    segt = seg.T
    qg = (jnp.tile(q_gain, SB_HEADS) * (SB_HEAD_DIM ** -0.5 * LOG2_E)).reshape(1, D)
    kg = jnp.tile(k_gain, SB_HEADS).reshape(1, D)
    row = pl.BlockSpec((tm, D), lambda i: (i, 0))
    vec = pl.BlockSpec((1, D), lambda i: (0, 0))
    return pl.pallas_call(
        _qkv_kernel,
        grid=(T // tm,),
        in_specs=[row, vec, pl.BlockSpec((D, 3 * D), lambda i: (0, 0)),
                  pl.BlockSpec((D, LANES), lambda i: (0, 0)), pl.BlockSpec((LANES, D), lambda i: (0, 0)),
                  vec, vec],
        out_specs=[row, row, row],
        out_shape=[jax.ShapeDtypeStruct((T, D), BF16)] * 3,
        compiler_params=_params("parallel"),
        name="sb_qkv",
    )(x2, norm.reshape(1, D), w_qkv.astype(BF16), seg, segt, qg, kg)


def _attn_kernel(q_ref, k_ref, v_ref, u_ref, ceil_ref, bias_ref, o_ref, qh_buf, acc_ref, rem_ref, *, t, subs):
    step = pl.program_id(2)
    lane = lax.broadcasted_iota(jnp.int32, (t, LANES), 1)
    first_head = lane < SB_HEAD_DIM
    for s in range(subs):
        q2 = q_ref[s * t:(s + 1) * t, :]
        zero = jnp.zeros_like(q2)
        qh_buf[s, 0] = jnp.where(first_head, q2, zero)
        qh_buf[s, 1] = jnp.where(first_head, zero, q2)

    def key_tile(qi, j):
        start = pl.multiple_of(jnp.maximum(qi - j, 0) * t, t)
        return k_ref[0, pl.ds(start, t), :], v_ref[0, pl.ds(start, t), :]

    def scores(s, hd, kt):
        return lax.dot_general(qh_buf[s, hd], kt, (((1,), (1,)), ((), ())), preferred_element_type=F32)

    def drop_of(z, ceil=None):
        drop = jnp.maximum(z, 0.0) + jnp.log2(1.0 + jnp.exp(jnp.abs(z) * (-LN_2)))
        log_beta = z - drop
        if ceil is not None:
            drop = jnp.minimum(drop, ceil)
        return drop, log_beta

    def suffix_sums(drop):
        return _dot(drop.astype(BF16), u_ref[...])

    def weights(log_beta, later, bias=None):
        arg = log_beta - later
        if bias is not None:
            arg = arg + bias
        return jnp.exp2(arg).astype(BF16)

    tiles = [step * subs + s for s in range(subs)]
    units = [(s, hd) for s in range(subs) for hd in range(2)]
    keys, values = [], []
    for s, qi in enumerate(tiles):
        (k0, v0), (k1, v1) = key_tile(qi, 0), key_tile(qi, 1)
        keys.append((k0, k1))
        values.append(jnp.concatenate([v0, jnp.where(qi >= 1, v1, jnp.zeros_like(v1))], axis=0))
    z = {(s, hd): [scores(s, hd, kt) for kt in keys[s]] for s, hd in units}
    dl = {un: [drop_of(z[un][0], ceil_ref[...]), drop_of(z[un][1])] for un in units}
    c = {un: [suffix_sums(dl[un][j][0]) for j in range(2)] for un in units}
    pv = {}
    for s, hd in units:
        (d0, lb0), (d1, lb1) = dl[s, hd]
        c0, c1 = c[s, hd]
        rem1 = c0[:, 0:1] + d0[:, 0:1]
        a0 = weights(lb0, c0, bias_ref[...])
        a1 = weights(lb1, c1 + rem1)
        pv[s, hd] = _dot(jnp.concatenate([a0, a1], axis=1), values[s])
        rem_ref[s, hd] = rem1 + (c1[:, 0:1] + d1[:, 0:1])
    for s in range(subs):
        acc_ref[s] = jnp.where(first_head, pv[s, 0], pv[s, 1])

    for s, qi in enumerate(tiles):
        def rem_min():
            return jnp.min(jnp.minimum(rem_ref[s, 0], rem_ref[s, 1]))

        def more(carry):
            j, smallest_rem = carry
            return (j <= qi) & (smallest_rem < ZERO_WEIGHT_LOG2)

        def sweep(carry):
            j, _ = carry
            kt, vt = key_tile(qi, j)
            pv = []
            for hd in range(2):
                drop, log_beta = drop_of(scores(s, hd, kt))
                c = suffix_sums(drop)
                rem = rem_ref[s, hd]
                pv.append(_dot(weights(log_beta, c + rem), vt))
                rem_ref[s, hd] = rem + (c[:, 0:1] + drop[:, 0:1])
            acc_ref[s] += jnp.where(first_head, pv[0], pv[1])
            return j + 1, rem_min()

        lax.while_loop(more, sweep, (jnp.int32(2), rem_min()))
        o_ref[s * t:(s + 1) * t, :] = acc_ref[s].astype(o_ref.dtype)


def _attention(q, k, v, B, S):
    T, D = q.shape
    t = min(ATTN_TILE, S)
    subs = ATTN_TILES_PER_STEP
    steps = S // (t * subs)
    idx = jnp.arange(t)
    u = (idx[:, None] > idx[None, :]).astype(BF16)
    causal = idx[None, :] < idx[:, None]
    ceil = jnp.where(causal, jnp.inf, 0.0).astype(F32)
    bias = jnp.where(causal, 0.0, -jnp.inf).astype(F32)
    k3 = k.reshape(B, S, D)
    v3 = v.reshape(B, S, D)
    kv_spec = pl.BlockSpec((1, S, LANES), lambda b, hp, i: (b, 0, hp))
    table_spec = pl.BlockSpec((t, t), lambda b, hp, i: (0, 0))
    q_spec = pl.BlockSpec((subs * t, LANES), lambda b, hp, i: (b * steps + i, hp))
    return pl.pallas_call(
        functools.partial(_attn_kernel, t=t, subs=subs),
        grid=(B, D // LANES, steps),
        in_specs=[q_spec, kv_spec, kv_spec, table_spec, table_spec, table_spec],
        out_specs=q_spec,
        out_shape=jax.ShapeDtypeStruct((T, D), BF16),
        scratch_shapes=[pltpu.VMEM((subs, 2, t, LANES), BF16), pltpu.VMEM((subs, t, LANES), F32),
                        pltpu.VMEM((subs, 2, t, 1), F32)],
        compiler_params=_params("parallel", "parallel", "arbitrary"),
        name="sb_attention",
    )(q, k3, v3, u, ceil, bias)


def _sb_mixer(x, norm, w_qkv, q_gain, k_gain):
    B, S, D = x.shape
    q, k, v = _qkv(x.reshape(B * S, D), norm, w_qkv, q_gain, k_gain)
    return _attention(q, k, v, B, S)


def kernel(x, pool_norm, pool_w, pool_b, pool_scale, sb_norm, sb_w_qkv, sb_q_gain, sb_k_gain, sb_w_o, moe_norm, moe_w_group, moe_b_group, moe_w_router, moe_b_router, moe_w_gate, moe_w_up, moe_w_down):
    stacked = lambda w: w.astype(BF16).reshape((DEPTH * N_EXPERTS,) + w.shape[2:])
    w_gate, w_up, w_down = stacked(moe_w_gate), stacked(moe_w_up), stacked(moe_w_down)
    for i in range(DEPTH):
        j = i // 2
        attn = None
        if i % 2 == 0:
            x = _pool_layer(x, pool_norm[j], pool_w[j], pool_b[j], pool_scale[j])
        else:
            attn = (_sb_mixer(x, sb_norm[j], sb_w_qkv[j], sb_q_gain[j], sb_k_gain[j]), sb_w_o[j])
        x = _moe_layer(x, moe_norm[i], moe_w_group[i], moe_b_group[i], moe_w_router[i], moe_b_router[i],
                       w_gate, w_up, w_down, layer=i, attn=attn)
    return x
```

```python
import functools

import jax
import jax.numpy as jnp
from jax import lax
from jax.experimental import pallas as pl
from jax.experimental.pallas import tpu as pltpu

F32 = jnp.float32
BF16 = jnp.bfloat16

D_MODEL = 1024
DEPTH = 4
POOL_WINDOWS = (2, 4, 8, 16)
POOL_GROUP_DIM = D_MODEL // len(POOL_WINDOWS)
POOL_HALO = 16
POOL_LEAD = 8
SB_HEADS = 16
SB_HEAD_DIM = D_MODEL // SB_HEADS
N_GROUPS = 4
EXPERTS_PER_GROUP = 8
N_EXPERTS = N_GROUPS * EXPERTS_PER_GROUP
D_EXPERT = D_MODEL // 4
RMS_EPS = 1e-6
LOG2_E = 1.4426950408889634
LN_2 = 0.6931471805599453
ZERO_WEIGHT_LOG2 = 150.0

LANES = 128
MXU_DEPTH = 256
DMA_PRIORITIES = 2
VMEM_LIMIT_BYTES = 56 * 1024 * 1024

POOL_TILE = 1024
ROUTER_TILE = 1024
INVERT_TILE = 2048
QKV_TILE = 512
ATTN_TILE = 128
ATTN_NEAR_TILES = 2
ATTN_TILES_PER_STEP = 8


def _params(*sem):
    return pltpu.CompilerParams(dimension_semantics=sem, vmem_limit_bytes=VMEM_LIMIT_BYTES)


def _rms(v, gain):
    return v * lax.rsqrt(jnp.mean(v * v, axis=-1, keepdims=True) + RMS_EPS) * gain


def _dot(a, b):
    return jnp.dot(a, b, preferred_element_type=F32)


def _split_bf16(v):
    hi = v.astype(BF16)
    lo = (v - hi.astype(F32)).astype(BF16)
    return hi, lo


def _pool_kernel(xprev_ref, x_ref, norm_ref, w_ref, b_ref, scale_ref, o_ref, sums):
    i = pl.program_id(1)
    ts = x_ref.shape[1]
    x = x_ref[0]
    gain = norm_ref[...]
    h = _rms(x, gain)
    hprev = jnp.where(i > 0, _rms(xprev_ref[0], gain), 0.0)
    top, lo = POOL_LEAD, POOL_LEAD + POOL_HALO
    n = lo + ts
    sums[:, 0:top, :] = jnp.zeros((sums.shape[0], top, sums.shape[2]), F32)
    sums[0, top:lo, :] = hprev
    sums[0, lo:n, :] = h
    levels = len(POOL_WINDOWS)
    for k in range(1, levels):
        shift, c0 = 2 ** (k - 1), (k - 1) * POOL_GROUP_DIM
        sums[k, top:n, c0:] = sums[k - 1, top:n, c0:] + sums[k - 1, top - shift:n - shift, c0:]
    pos = i * ts + lax.broadcasted_iota(jnp.int32, (ts, 1), 0)
    for g, win in enumerate(POOL_WINDOWS):
        c0 = g * POOL_GROUP_DIM
        cols = slice(c0, c0 + POOL_GROUP_DIM)
        hg = h[:, cols]
        if g + 1 < levels:
            acc = sums[g + 1, lo:n, cols]
        else:
            half = win // 2
            acc = sums[g, lo:n, cols] + sums[g, lo - half:n - half, cols]
        count = jnp.minimum(pos + 1, win).astype(F32)
        diff = acc / count - hg
        y = _dot(diff.astype(BF16), w_ref[g]) + b_ref[g]
        o_ref[0, :, cols] = x[:, cols] + y * scale_ref[:, cols]


def _pool_layer(x, norm, w, b, scale):
    B, S, D = x.shape
    ts = min(POOL_TILE, S)
    halo_blocks = ts // POOL_HALO
    return pl.pallas_call(
        _pool_kernel,
        grid=(B, S // ts),
        in_specs=[
            pl.BlockSpec((1, POOL_HALO, D), lambda bi, i: (bi, jnp.maximum(i * halo_blocks - 1, 0), 0)),
            pl.BlockSpec((1, ts, D), lambda bi, i: (bi, i, 0)),
            pl.BlockSpec((1, D), lambda bi, i: (0, 0)),
            pl.BlockSpec((len(POOL_WINDOWS), POOL_GROUP_DIM, POOL_GROUP_DIM), lambda bi, i: (0, 0, 0)),
            pl.BlockSpec((len(POOL_WINDOWS), 1, POOL_GROUP_DIM), lambda bi, i: (0, 0, 0)),
            pl.BlockSpec((1, D), lambda bi, i: (0, 0)),
        ],
        out_specs=pl.BlockSpec((1, ts, D), lambda bi, i: (bi, i, 0)),
        out_shape=jax.ShapeDtypeStruct((B, S, D), F32),
        scratch_shapes=[pltpu.VMEM((len(POOL_WINDOWS), POOL_LEAD + POOL_HALO + ts, D), F32)],
        compiler_params=_params("parallel", "parallel"),
        name="pool_layer",
    )(x, x, norm.reshape(1, D), w.astype(BF16), b.reshape(len(POOL_WINDOWS), 1, POOL_GROUP_DIM),
      scale.reshape(1, D))


ROUTER_GROUP_ROW0 = N_EXPERTS
ROUTER_ROWS = LANES
ROUTER_USED_ROWS = 40
ROUTER_COUNT_ROWS = 16
ROW_WIDTH = D_MODEL + LANES
MOE_ROW_TILE = 512
MOE_PAD_ROWS = N_GROUPS * MOE_ROW_TILE


def _router_kernel(*refs, mixer_out):
    if mixer_out:
        o_ref, wo_ref, *refs = refs
    x_ref, norm_ref, whi_ref, wlo_ref, bias_ref, tri_ref, row_ref, where_ref, count_ref, run_ref = refs

    @pl.when(pl.program_id(0) == 0)
    def _():
        run_ref[...] = jnp.zeros_like(run_ref)

    x = x_ref[...]
    if mixer_out:
        x = x + _dot(o_ref[...], wo_ref[...])
    h = _rms(x, norm_ref[...])
    h_hi, h_lo = _split_bf16(h)
    w_hi = whi_ref[...]

    def per_token(w, t):
        return sum(lax.dot_general(w[:, c:c + MXU_DEPTH], t[:, c:c + MXU_DEPTH], (((1,), (1,)), ((), ())),
                                   preferred_element_type=F32) for c in range(0, w.shape[1], MXU_DEPTH))

    all_logits = per_token(w_hi, h_hi) + per_token(w_hi, h_lo) + per_token(wlo_ref[...], h_hi) + bias_ref[...]
    tokens = all_logits.shape[1]
    shape = (ROUTER_USED_ROWS, LANES)
    row = lax.broadcasted_iota(jnp.int32, shape, 0).astype(F32)
    all_rows = lax.broadcasted_iota(jnp.int32, (ROUTER_ROWS, LANES), 0).astype(F32)
    few = row[:ROUTER_COUNT_ROWS]
    neg_inf = F32(-jnp.inf)
    no_row = F32(4 * ROUTER_ROWS)

    def first_argmax(v):
        m = jnp.max(v, axis=0, keepdims=True)
        idx = jnp.min(jnp.where(v == m, row, no_row), axis=0, keepdims=True)
        return m, idx

    gates, groups, members = [], [], []
    for c in range(0, tokens, LANES):
        logits = all_logits[:ROUTER_USED_ROWS, c:c + LANES]
        is_group = (row >= ROUTER_GROUP_ROW0) & (row < ROUTER_GROUP_ROW0 + N_GROUPS)
        gl = jnp.where(is_group, logits, neg_inf)
        gmax, grow = first_argmax(gl)
        g_prob = 1.0 / jnp.sum(jnp.exp(gl - gmax), axis=0, keepdims=True)
        group = grow - ROUTER_GROUP_ROW0
        e0 = group * EXPERTS_PER_GROUP
        el = jnp.where((row >= e0) & (row < e0 + EXPERTS_PER_GROUP), logits, neg_inf)
        m1, i1 = first_argmax(el)
        m2, i2 = first_argmax(jnp.where(row == i1, neg_inf, el))
        r = jnp.exp(m2 - m1)
        w1 = 1.0 / (1.0 + r)
        gates.append(jnp.where(all_rows == i1, g_prob * w1, 0.0) + jnp.where(all_rows == i2, g_prob * (r * w1), 0.0))
        groups.append(group)
        members.append(jnp.where(few == group, 1.0, 0.0))

    in_group = jnp.concatenate(members, axis=1).astype(BF16)
    run = run_ref[...]
    earlier = _dot(in_group, tri_ref[...]) + run
    run = run + _dot(in_group, jnp.ones(tri_ref.shape, BF16))
    run_ref[...] = run
    count_ref[...] = run[:, :LANES]

    row_ref[:, :D_MODEL] = x
    for j, c in enumerate(range(0, tokens, LANES)):
        rank = jnp.sum(members[j] * earlier[:, c:c + LANES], axis=0, keepdims=True)
        row_ref[c:c + LANES, D_MODEL:] = gates[j].T
        where_ref[0, :, c:c + LANES] = jnp.where(few[:8] == 0.0, groups[j],
                                                 jnp.where(few[:8] == 1.0, rank, 0.0)).astype(jnp.int32)


def _router(x2, norm, w_group, b_group, w_router, b_router, attn=None):
    T, D = x2.shape
    tm = min(ROUTER_TILE, T)
    groups = slice(ROUTER_GROUP_ROW0, ROUTER_GROUP_ROW0 + N_GROUPS)
    w = jnp.zeros((ROUTER_ROWS, D), F32).at[:N_EXPERTS].set(w_router.T).at[groups].set(w_group.T)
    w_hi = w.astype(BF16)
    w_lo = (w - w_hi.astype(F32)).astype(BF16)
    bias = jnp.zeros((ROUTER_ROWS,), F32).at[:N_EXPERTS].set(b_router).at[groups].set(b_group)
    bias = jnp.broadcast_to(bias[:, None], (ROUTER_ROWS, tm))
    idx = jnp.arange(tm)
    tri = (idx[:, None] < idx[None, :]).astype(BF16)
    mixer_specs, mixer_args = [], []
    if attn is not None:
        mixer_specs = [pl.BlockSpec((tm, D), lambda i: (i, 0)), pl.BlockSpec((D, D), lambda i: (0, 0))]
        mixer_args = [attn[0], attn[1].astype(BF16)]
    return pl.pallas_call(
        functools.partial(_router_kernel, mixer_out=attn is not None),
        grid=(T // tm,),
        in_specs=mixer_specs + [
            pl.BlockSpec((tm, D), lambda i: (i, 0)),
            pl.BlockSpec((1, D), lambda i: (0, 0)),
            pl.BlockSpec((ROUTER_ROWS, D), lambda i: (0, 0)),
            pl.BlockSpec((ROUTER_ROWS, D), lambda i: (0, 0)),
            pl.BlockSpec((ROUTER_ROWS, tm), lambda i: (0, 0)),
            pl.BlockSpec((tm, tm), lambda i: (0, 0)),
        ],
        out_specs=[pl.BlockSpec((tm, ROW_WIDTH), lambda i: (i, 0)),
                   pl.BlockSpec((1, 8, tm), lambda i: (i, 0, 0)),
                   pl.BlockSpec((ROUTER_COUNT_ROWS, LANES), lambda i: (0, 0))],
        out_shape=[jax.ShapeDtypeStruct((T, ROW_WIDTH), F32),
                   jax.ShapeDtypeStruct((T // tm, 8, tm), jnp.int32),
                   jax.ShapeDtypeStruct((ROUTER_COUNT_ROWS, LANES), F32)],
        scratch_shapes=[pltpu.VMEM((ROUTER_COUNT_ROWS, tm), F32)],
        compiler_params=_params("arbitrary"),
        name="moe_router",
    )(*mixer_args, x2, norm.reshape(1, D), w_hi, w_lo, bias, tri)


def _experts_kernel(tile_group_ref, tile_rows_ref, tok_prev_ref, tok_ref, tok_next_ref, rows_ref, norm_ref,
                    wg_ref, wu_ref, wd_ref, out_ref, rbuf, obuf, in_sem, out_sem):
    r = pl.program_id(0)
    last = pl.num_programs(0) - 1
    slot = r % 2

    def gather(dst_slot, tok):
        for t in range(MOE_ROW_TILE):
            pltpu.make_async_copy(rows_ref.at[pl.ds(tok[t], 1)], rbuf.at[dst_slot, pl.ds(t, 1)],
                                  in_sem.at[dst_slot]).start(priority=t % DMA_PRIORITIES)

    def row_copy(src_slot, t, tok):
        return pltpu.make_async_copy(obuf.at[src_slot, pl.ds(t, 1)], out_ref.at[pl.ds(tok[t], 1)],
                                     out_sem.at[src_slot])

    def send_rows(n, src_slot, tok):
        def body(t, carry):
            row_copy(src_slot, t, tok).start()
            return carry
        lax.fori_loop(0, n, body, 0)

    def wait_rows(n, src_slot):
        @pl.when(n == MOE_ROW_TILE)
        def _():
            pltpu.make_async_copy(obuf.at[src_slot], obuf.at[src_slot], out_sem.at[src_slot]).wait()

        @pl.when(n < MOE_ROW_TILE)
        def _():
            def body(t, carry):
                pltpu.make_async_copy(obuf.at[src_slot, pl.ds(0, 1)], obuf.at[src_slot, pl.ds(0, 1)],
                                      out_sem.at[src_slot]).wait()
                return carry
            lax.fori_loop(0, n, body, 0)

    rows_now = tile_rows_ref[r]
    rows_prev = jnp.where(r >= 1, tile_rows_ref[jnp.maximum(r - 1, 0)], 0)

    @pl.when((r == 0) & (rows_now > 0))
    def _():
        gather(slot, tok_ref)

    @pl.when(r >= 2)
    def _():
        wait_rows(tile_rows_ref[jnp.maximum(r - 2, 0)], slot)

    rows_next = jnp.where(r < last, tile_rows_ref[jnp.minimum(r + 1, last)], 0)

    @pl.when(rows_next > 0)
    def _():
        gather(1 - slot, tok_next_ref)

    @pl.when(rows_now > 0)
    def _():
        pltpu.make_async_copy(rbuf.at[slot], rbuf.at[slot], in_sem.at[slot]).wait()

    @pl.when(rows_prev == MOE_ROW_TILE)
    def _():
        for t in range(MOE_ROW_TILE):
            row_copy(1 - slot, t, tok_prev_ref).start(priority=t % DMA_PRIORITIES)

    @pl.when(rows_prev < MOE_ROW_TILE)
    def _():
        send_rows(rows_prev, 1 - slot, tok_prev_ref)

    @pl.when(rows_now > 0)
    def _():
        x = rbuf[slot, :, :D_MODEL]
        gate = rbuf[slot, :, D_MODEL:]
        h = _rms(x, norm_ref[...]).astype(BF16)
        lane = lax.broadcasted_iota(jnp.int32, gate.shape, 1)
        e0 = tile_group_ref[r] * EXPERTS_PER_GROUP
        acc = x
        for e in range(EXPERTS_PER_GROUP):
            a = _dot(h, wg_ref[e])
            u = _dot(h, wu_ref[e])
            gcol = jnp.sum(jnp.where(lane == e0 + e, gate, 0.0), axis=1, keepdims=True)
            hid = (a * jax.nn.sigmoid(a)) * u * gcol
            acc = acc + _dot(hid.astype(BF16), wd_ref[e])
        obuf[slot] = acc

    @pl.when(r == last)
    def _():
        send_rows(rows_now, slot, tok_ref)
        wait_rows(rows_prev, 1 - slot)
        wait_rows(rows_now, slot)


def _experts(rows, tile_group, tile_rows, tok, norm, w_gate, w_up, w_down, layer):
    T = rows.shape[0]
    D = D_MODEL
    n_tiles = tok.shape[0] // MOE_ROW_TILE
    group_w = lambda r, tg, tr: (layer * N_GROUPS + tg[r], 0, 0)
    smem_tile = lambda index: pl.BlockSpec((MOE_ROW_TILE,), index, memory_space=pltpu.SMEM)
    return pl.pallas_call(
        _experts_kernel,
        grid_spec=pltpu.PrefetchScalarGridSpec(
            num_scalar_prefetch=2,
            grid=(n_tiles,),
            in_specs=[smem_tile(lambda r, tg, tr: (jnp.maximum(r - 1, 0),)),
                      smem_tile(lambda r, tg, tr: (r,)),
                      smem_tile(lambda r, tg, tr: (jnp.minimum(r + 1, n_tiles - 1),)),
                      pl.BlockSpec(memory_space=pl.ANY),
                      pl.BlockSpec((1, D), lambda r, tg, tr: (0, 0)),
                      pl.BlockSpec((EXPERTS_PER_GROUP, D, D_EXPERT), group_w),
                      pl.BlockSpec((EXPERTS_PER_GROUP, D, D_EXPERT), group_w),
                      pl.BlockSpec((EXPERTS_PER_GROUP, D_EXPERT, D), group_w)],
            out_specs=pl.BlockSpec(memory_space=pl.ANY),
            scratch_shapes=[pltpu.VMEM((2, MOE_ROW_TILE, ROW_WIDTH), F32), pltpu.VMEM((2, MOE_ROW_TILE, D), F32),
                            pltpu.SemaphoreType.DMA((2,)), pltpu.SemaphoreType.DMA((2,))],
        ),
        out_shape=jax.ShapeDtypeStruct((T, D), F32),
        compiler_params=_params("arbitrary"),
        name="moe_experts",
    )(tile_group, tile_rows, tok, tok, tok, rows, norm.reshape(1, D), w_gate.astype(BF16), w_up.astype(BF16),
      w_down.astype(BF16))


def _invert_kernel(pad_lo_ref, pad_hi_ref, dest_ref, tok_ref):
    i = pl.program_id(0)
    tm = dest_ref.shape[0]

    @pl.when(i == 0)
    def _():
        for seg in range(N_GROUPS + 1):
            def clear(p, carry):
                tok_ref[p] = 0
                return carry
            lax.fori_loop(pad_lo_ref[seg], pad_hi_ref[seg], clear, 0)

    for t in range(tm):
        tok_ref[dest_ref[t]] = i * tm + t


def _invert(dest, pad_lo, pad_hi, total_rows):
    T = dest.shape[0]
    tm = min(INVERT_TILE, T)
    return pl.pallas_call(
        _invert_kernel,
        grid_spec=pltpu.PrefetchScalarGridSpec(
            num_scalar_prefetch=2,
            grid=(T // tm,),
            in_specs=[pl.BlockSpec((tm,), lambda i, lo, hi: (i,), memory_space=pltpu.SMEM)],
            out_specs=pl.BlockSpec(memory_space=pltpu.SMEM),
        ),
        out_shape=jax.ShapeDtypeStruct((total_rows,), jnp.int32),
        compiler_params=_params("arbitrary"),
        name="moe_invert",
    )(pad_lo, pad_hi, dest)


def _moe_layer(x, norm, w_group, b_group, w_router, b_router, w_gate, w_up, w_down, layer=0, attn=None):
    B, S, D = x.shape
    T = B * S
    rows, where, counts = _router(x.reshape(T, D), norm, w_group, b_group, w_router, b_router, attn)
    where = where.transpose(1, 0, 2).reshape(8, T)
    group, rank = where[0], where[1]
    counts = counts[:N_GROUPS, 0].astype(jnp.int32)
    tiles = (counts + MOE_ROW_TILE - 1) // MOE_ROW_TILE
    tile_end = jnp.cumsum(tiles)
    tile_first = tile_end - tiles
    dest = tile_first[group] * MOE_ROW_TILE + rank
    total_rows = T + MOE_PAD_ROWS
    pad_lo = jnp.concatenate([tile_first * MOE_ROW_TILE + counts, tile_end[-1:] * MOE_ROW_TILE]).astype(jnp.int32)
    pad_hi = jnp.concatenate([tile_end * MOE_ROW_TILE, jnp.full((1,), total_rows)]).astype(jnp.int32)
    tok = _invert(dest.astype(jnp.int32), pad_lo, pad_hi, total_rows)
    tile_idx = jnp.arange(total_rows // MOE_ROW_TILE)
    tile_group = jnp.minimum(jnp.sum(tile_idx[:, None] >= tile_end[None, :], axis=1), N_GROUPS - 1)
    tile_rows = jnp.clip(counts[tile_group] - (tile_idx - tile_first[tile_group]) * MOE_ROW_TILE, 0, MOE_ROW_TILE)
    tile_rows = jnp.where(tile_idx < tile_end[-1], tile_rows, 0)
    return _experts(rows, tile_group.astype(jnp.int32), tile_rows.astype(jnp.int32), tok, norm,
                    w_gate, w_up, w_down, layer).reshape(B, S, D)


def _qkv_kernel(x_ref, norm_ref, w_ref, seg_ref, segt_ref, qg_ref, kg_ref, q_ref, k_ref, v_ref):
    D = x_ref.shape[1]
    h = _rms(x_ref[...], norm_ref[...]).astype(BF16)
    qkv = _dot(h, w_ref[...])
    seg = seg_ref[...]
    segt = segt_ref[...]

    def head_norm(t, gain):
        ms = _dot((t * t).astype(BF16), seg) * (1.0 / SB_HEAD_DIM)
        r_hi, r_lo = _split_bf16(lax.rsqrt(ms + RMS_EPS))
        return t * (_dot(r_hi, segt) + _dot(r_lo, segt)) * gain

    q_ref[...] = head_norm(qkv[:, :D], qg_ref[...]).astype(BF16)
    k_ref[...] = head_norm(qkv[:, D:2 * D], kg_ref[...]).astype(BF16)
    v_ref[...] = qkv[:, 2 * D:].astype(BF16)


def _qkv(x2, norm, w_qkv, q_gain, k_gain):
    T, D = x2.shape
    tm = min(QKV_TILE, T)
    head_of = jnp.arange(D) // SB_HEAD_DIM
    seg = (head_of[:, None] == jnp.arange(LANES)[None, :]).astype(BF16)
    segt = seg.T
    qg = (jnp.tile(q_gain, SB_HEADS) * (SB_HEAD_DIM ** -0.5 * LOG2_E)).reshape(1, D)
    kg = jnp.tile(k_gain, SB_HEADS).reshape(1, D)
    row = pl.BlockSpec((tm, D), lambda i: (i, 0))
    vec = pl.BlockSpec((1, D), lambda i: (0, 0))
    return pl.pallas_call(
        _qkv_kernel,
        grid=(T // tm,),
        in_specs=[row, vec, pl.BlockSpec((D, 3 * D), lambda i: (0, 0)),
                  pl.BlockSpec((D, LANES), lambda i: (0, 0)), pl.BlockSpec((LANES, D), lambda i: (0, 0)),
                  vec, vec],
        out_specs=[row, row, row],
        out_shape=[jax.ShapeDtypeStruct((T, D), BF16)] * 3,
        compiler_params=_params("parallel"),
        name="sb_qkv",
    )(x2, norm.reshape(1, D), w_qkv.astype(BF16), seg, segt, qg, kg)


def _attn_kernel(q_ref, k_ref, v_ref, u_ref, ceil_ref, bias_ref, o_ref, qh_buf, acc_ref, rem_ref, *, t, subs):
    step = pl.program_id(2)
    near = ATTN_NEAR_TILES * t
    lane = lax.broadcasted_iota(jnp.int32, (t, LANES), 1)
    first_head = lane < SB_HEAD_DIM
    for s in range(subs):
        q2 = q_ref[s * t:(s + 1) * t, :]
        zero = jnp.zeros_like(q2)
        qh_buf[s, 0] = jnp.where(first_head, q2, zero)
        qh_buf[s, 1] = jnp.where(first_head, zero, q2)

    def keys(start, n):
        return k_ref[0, pl.ds(start, n), :], v_ref[0, pl.ds(start, n), :]

    def scores(s, hd, kt):
        return lax.dot_general(qh_buf[s, hd], kt, (((1,), (1,)), ((), ())), preferred_element_type=F32)

    def drop_of(z, ceil=None):
        drop = jnp.maximum(z, 0.0) + jnp.log2(1.0 + jnp.exp(jnp.abs(z) * (-LN_2)))
        log_beta = z - drop
        if ceil is not None:
            drop = jnp.minimum(drop, ceil)
        return drop, log_beta

    def suffix_sums(drop):
        n = drop.shape[1]
        return _dot(drop.astype(BF16), u_ref[:n, :n])

    def weights(log_beta, later, bias=None):
        arg = log_beta - later
        if bias is not None:
            arg = arg + bias
        return jnp.exp2(arg).astype(BF16)

    def diagonal(s, hd, kd):
        drop, log_beta = drop_of(scores(s, hd, kd), ceil_ref[...])
        c = suffix_sums(drop)
        return weights(log_beta, c, bias_ref[...]), c[:, 0:1] + drop[:, 0:1]

    def sweep(s, qi, first):
        def rem_min():
            return jnp.min(jnp.minimum(rem_ref[s, 0], rem_ref[s, 1]))

        def more(carry):
            j, smallest_rem = carry
            return (j <= qi) & (smallest_rem < ZERO_WEIGHT_LOG2)

        def body(carry):
            j, _ = carry
            kt, vt = keys(pl.multiple_of((qi - j) * t, t), t)
            pv = []
            for hd in range(2):
                drop, log_beta = drop_of(scores(s, hd, kt))
                c = suffix_sums(drop)
                rem = rem_ref[s, hd]
                pv.append(_dot(weights(log_beta, c + rem), vt))
                rem_ref[s, hd] = rem + (c[:, 0:1] + drop[:, 0:1])
            acc_ref[s] += jnp.where(first_head, pv[0], pv[1])
            return j + 1, rem_min()

        lax.while_loop(more, body, (jnp.int32(first), rem_min()))

    units = [(s, hd) for s in range(subs) for hd in range(2)]

    @pl.when(step > 0)
    def _():
        tiles = [step * subs + s for s in range(subs)]
        diag = [keys(pl.multiple_of(qi * t, t), t) for qi in tiles]
        left = [keys(pl.multiple_of((qi - ATTN_NEAR_TILES) * t, t), near) for qi in tiles]
        values = [jnp.concatenate([diag[s][1], left[s][1]], axis=0) for s in range(subs)]
        z = {(s, hd): (scores(s, hd, diag[s][0]), scores(s, hd, left[s][0])) for s, hd in units}
        dl = {un: (drop_of(z[un][0], ceil_ref[...]), drop_of(z[un][1])) for un in units}
        c = {un: (suffix_sums(dl[un][0][0]), suffix_sums(dl[un][1][0])) for un in units}
        pv = {}
        for s, hd in units:
            (d0, lb0), (d1, lb1) = dl[s, hd]
            c0, c1 = c[s, hd]
            rem1 = c0[:, 0:1] + d0[:, 0:1]
            a0 = weights(lb0, c0, bias_ref[...])
            a1 = weights(lb1, c1 + rem1)
            pv[s, hd] = _dot(jnp.concatenate([a0, a1], axis=1), values[s])
            rem_ref[s, hd] = rem1 + (c1[:, 0:1] + d1[:, 0:1])
        smallest_rem = None
        for s in range(subs):
            acc_ref[s] = jnp.where(first_head, pv[s, 0], pv[s, 1])
            for hd in range(2):
                rem = rem_ref[s, hd]
                smallest_rem = rem if smallest_rem is None else jnp.minimum(smallest_rem, rem)

        @pl.when(jnp.min(smallest_rem) < ZERO_WEIGHT_LOG2)
        def _():
            for s, qi in enumerate(tiles):
                sweep(s, qi, ATTN_NEAR_TILES + 1)

    @pl.when(step == 0)
    def _():
        for s in range(subs):
            kd, vd = keys(s * t, t)
            pv = []
            for hd in range(2):
                a, rem1 = diagonal(s, hd, kd)
                pv.append(_dot(a, vd))
                rem_ref[s, hd] = rem1
            acc_ref[s] = jnp.where(first_head, pv[0], pv[1])
            sweep(s, s, 1)

    for s in range(subs):
        o_ref[s * t:(s + 1) * t, :] = acc_ref[s].astype(o_ref.dtype)


def _attention(q, k, v, B, S):
    T, D = q.shape
    t = min(ATTN_TILE, S)
    subs = ATTN_TILES_PER_STEP
    assert subs >= ATTN_NEAR_TILES and S % (t * subs) == 0
    steps = S // (t * subs)
    near = ATTN_NEAR_TILES * t
    idx = jnp.arange(near)
    u = (idx[:, None] > idx[None, :]).astype(BF16)
    causal = idx[None, :t] < idx[:t, None]
    ceil = jnp.where(causal, jnp.inf, 0.0).astype(F32)
    bias = jnp.where(causal, 0.0, -jnp.inf).astype(F32)
    k3 = k.reshape(B, S, D)
    v3 = v.reshape(B, S, D)
    kv_spec = pl.BlockSpec((1, S, LANES), lambda b, hp, i: (b, 0, hp))
    table_spec = pl.BlockSpec((t, t), lambda b, hp, i: (0, 0))
    q_spec = pl.BlockSpec((subs * t, LANES), lambda b, hp, i: (b * steps + i, hp))
    return pl.pallas_call(
        functools.partial(_attn_kernel, t=t, subs=subs),
        grid=(B, D // LANES, steps),
        in_specs=[q_spec, kv_spec, kv_spec, pl.BlockSpec((near, near), lambda b, hp, i: (0, 0)),
                  table_spec, table_spec],
        out_specs=q_spec,
        out_shape=jax.ShapeDtypeStruct((T, D), BF16),
        scratch_shapes=[pltpu.VMEM((subs, 2, t, LANES), BF16), pltpu.VMEM((subs, t, LANES), F32),
                        pltpu.VMEM((subs, 2, t, 1), F32)],
        compiler_params=_params("parallel", "parallel", "arbitrary"),
        name="sb_attention",
    )(q, k3, v3, u, ceil, bias)


def _sb_mixer(x, norm, w_qkv, q_gain, k_gain):
    B, S, D = x.shape
    q, k, v = _qkv(x.reshape(B * S, D), norm, w_qkv, q_gain, k_gain)
    return _attention(q, k, v, B, S)


def kernel(x, pool_norm, pool_w, pool_b, pool_scale, sb_norm, sb_w_qkv, sb_q_gain, sb_k_gain, sb_w_o, moe_norm, moe_w_group, moe_b_group, moe_w_router, moe_b_router, moe_w_gate, moe_w_up, moe_w_down):
    stacked = lambda w: w.astype(BF16).reshape((DEPTH * N_EXPERTS,) + w.shape[2:])
    w_gate, w_up, w_down = stacked(moe_w_gate), stacked(moe_w_up), stacked(moe_w_down)
    for i in range(DEPTH):
        j = i // 2
        attn = None
        if i % 2 == 0:
            x = _pool_layer(x, pool_norm[j], pool_w[j], pool_b[j], pool_scale[j])
        else:
            attn = (_sb_mixer(x, sb_norm[j], sb_w_qkv[j], sb_q_gain[j], sb_k_gain[j]), sb_w_o[j])
        x = _moe_layer(x, moe_norm[i], moe_w_group[i], moe_b_group[i], moe_w_router[i], moe_b_router[i],
                       w_gate, w_up, w_down, layer=i, attn=attn)
    return x
```

```python
import functools

import jax
import jax.numpy as jnp
from jax import lax
from jax.experimental import pallas as pl
from jax.experimental.pallas import tpu as pltpu

F32 = jnp.float32
BF16 = jnp.bfloat16

D_MODEL = 1024
DEPTH = 4
POOL_WINDOWS = (2, 4, 8, 16)
POOL_GROUP_DIM = D_MODEL // len(POOL_WINDOWS)
POOL_HALO = 16
POOL_LEAD = 8
SB_HEADS = 16
SB_HEAD_DIM = D_MODEL // SB_HEADS
N_GROUPS = 4
EXPERTS_PER_GROUP = 8
N_EXPERTS = N_GROUPS * EXPERTS_PER_GROUP
D_EXPERT = D_MODEL // 4
RMS_EPS = 1e-6
LOG2_E = 1.4426950408889634
LN_2 = 0.6931471805599453
ZERO_WEIGHT_LOG2 = 150.0

LANES = 128
MXU_DEPTH = 256
DMA_PRIORITIES = 2
VMEM_LIMIT_BYTES = 56 * 1024 * 1024

POOL_TILE = 1024
ROUTER_TILE = 1024
INVERT_TILE = 2048
QKV_TILE = 512
ATTN_TILE = 128
ATTN_NEAR_TILES = 2
ATTN_TILES_PER_STEP = 8


def _params(*sem):
    return pltpu.CompilerParams(dimension_semantics=sem, vmem_limit_bytes=VMEM_LIMIT_BYTES)


def _rms(v, gain):
    return v * lax.rsqrt(jnp.mean(v * v, axis=-1, keepdims=True) + RMS_EPS) * gain


def _dot(a, b):
    return jnp.dot(a, b, preferred_element_type=F32)


def _split_bf16(v):
    hi = v.astype(BF16)
    lo = (v - hi.astype(F32)).astype(BF16)
    return hi, lo


def _pool_kernel(xprev_ref, x_ref, norm_ref, w_ref, b_ref, scale_ref, o_ref, sums):
    i = pl.program_id(1)
    ts = x_ref.shape[1]
    x = x_ref[0]
    gain = norm_ref[...]
    h = _rms(x, gain)
    hprev = jnp.where(i > 0, _rms(xprev_ref[0], gain), 0.0)
    top, lo = POOL_LEAD, POOL_LEAD + POOL_HALO
    n = lo + ts
    sums[:, 0:top, :] = jnp.zeros((sums.shape[0], top, sums.shape[2]), F32)
    sums[0, top:lo, :] = hprev
    sums[0, lo:n, :] = h
    levels = len(POOL_WINDOWS)
    for k in range(1, levels):
        shift, c0 = 2 ** (k - 1), (k - 1) * POOL_GROUP_DIM
        sums[k, top:n, c0:] = sums[k - 1, top:n, c0:] + sums[k - 1, top - shift:n - shift, c0:]
    pos = i * ts + lax.broadcasted_iota(jnp.int32, (ts, 1), 0)
    for g, win in enumerate(POOL_WINDOWS):
        c0 = g * POOL_GROUP_DIM
        cols = slice(c0, c0 + POOL_GROUP_DIM)
        hg = h[:, cols]
        if g + 1 < levels:
            acc = sums[g + 1, lo:n, cols]
        else:
            half = win // 2
            acc = sums[g, lo:n, cols] + sums[g, lo - half:n - half, cols]
        count = jnp.minimum(pos + 1, win).astype(F32)
        diff = acc / count - hg
        y = _dot(diff.astype(BF16), w_ref[g]) + b_ref[g]
        o_ref[0, :, cols] = x[:, cols] + y * scale_ref[:, cols]


def _pool_layer(x, norm, w, b, scale):
    B, S, D = x.shape
    ts = min(POOL_TILE, S)
    halo_blocks = ts // POOL_HALO
    return pl.pallas_call(
        _pool_kernel,
        grid=(B, S // ts),
        in_specs=[
            pl.BlockSpec((1, POOL_HALO, D), lambda bi, i: (bi, jnp.maximum(i * halo_blocks - 1, 0), 0)),
            pl.BlockSpec((1, ts, D), lambda bi, i: (bi, i, 0)),
            pl.BlockSpec((1, D), lambda bi, i: (0, 0)),
            pl.BlockSpec((len(POOL_WINDOWS), POOL_GROUP_DIM, POOL_GROUP_DIM), lambda bi, i: (0, 0, 0)),
            pl.BlockSpec((len(POOL_WINDOWS), 1, POOL_GROUP_DIM), lambda bi, i: (0, 0, 0)),
            pl.BlockSpec((1, D), lambda bi, i: (0, 0)),
        ],
        out_specs=pl.BlockSpec((1, ts, D), lambda bi, i: (bi, i, 0)),
        out_shape=jax.ShapeDtypeStruct((B, S, D), F32),
        scratch_shapes=[pltpu.VMEM((len(POOL_WINDOWS), POOL_LEAD + POOL_HALO + ts, D), F32)],
        compiler_params=_params("parallel", "parallel"),
        name="pool_layer",
    )(x, x, norm.reshape(1, D), w.astype(BF16), b.reshape(len(POOL_WINDOWS), 1, POOL_GROUP_DIM),
      scale.reshape(1, D))


ROUTER_GROUP_ROW0 = N_EXPERTS
ROUTER_ROWS = LANES
ROUTER_USED_ROWS = 40
ROUTER_COUNT_ROWS = 16
ROW_WIDTH = D_MODEL + LANES
MOE_ROW_TILE = 512
MOE_PAD_ROWS = N_GROUPS * MOE_ROW_TILE


def _router_kernel(*refs, mixer_out):
    if mixer_out:
        o_ref, wo_ref, *refs = refs
    x_ref, norm_ref, whi_ref, wlo_ref, bias_ref, tri_ref, row_ref, where_ref, count_ref, run_ref = refs

    @pl.when(pl.program_id(0) == 0)
    def _():
        run_ref[...] = jnp.zeros_like(run_ref)

    x = x_ref[...]
    if mixer_out:
        x = x + _dot(o_ref[...], wo_ref[...])
    h = _rms(x, norm_ref[...])
    h_hi, h_lo = _split_bf16(h)
    w_hi = whi_ref[...]

    def per_token(w, t):
        return sum(lax.dot_general(w[:, c:c + MXU_DEPTH], t[:, c:c + MXU_DEPTH], (((1,), (1,)), ((), ())),
                                   preferred_element_type=F32) for c in range(0, w.shape[1], MXU_DEPTH))

    all_logits = per_token(w_hi, h_hi) + per_token(w_hi, h_lo) + per_token(wlo_ref[...], h_hi) + bias_ref[...]
    tokens = all_logits.shape[1]
    shape = (ROUTER_USED_ROWS, LANES)
    row = lax.broadcasted_iota(jnp.int32, shape, 0).astype(F32)
    all_rows = lax.broadcasted_iota(jnp.int32, (ROUTER_ROWS, LANES), 0).astype(F32)
    few = row[:ROUTER_COUNT_ROWS]
    neg_inf = F32(-jnp.inf)
    no_row = F32(4 * ROUTER_ROWS)

    def first_argmax(v):
        m = jnp.max(v, axis=0, keepdims=True)
        idx = jnp.min(jnp.where(v == m, row, no_row), axis=0, keepdims=True)
        return m, idx

    gates, groups, members = [], [], []
    for c in range(0, tokens, LANES):
        logits = all_logits[:ROUTER_USED_ROWS, c:c + LANES]
        is_group = (row >= ROUTER_GROUP_ROW0) & (row < ROUTER_GROUP_ROW0 + N_GROUPS)
        gl = jnp.where(is_group, logits, neg_inf)
        gmax, grow = first_argmax(gl)
        g_prob = 1.0 / jnp.sum(jnp.exp(gl - gmax), axis=0, keepdims=True)
        group = grow - ROUTER_GROUP_ROW0
        e0 = group * EXPERTS_PER_GROUP
        el = jnp.where((row >= e0) & (row < e0 + EXPERTS_PER_GROUP), logits, neg_inf)
        m1, i1 = first_argmax(el)
        m2, i2 = first_argmax(jnp.where(row == i1, neg_inf, el))
        r = jnp.exp(m2 - m1)
        w1 = 1.0 / (1.0 + r)
        gates.append(jnp.where(all_rows == i1, g_prob * w1, 0.0) + jnp.where(all_rows == i2, g_prob * (r * w1), 0.0))
        groups.append(group)
        members.append(jnp.where(few == group, 1.0, 0.0))

    in_group = jnp.concatenate(members, axis=1).astype(BF16)
    run = run_ref[...]
    earlier = _dot(in_group, tri_ref[...]) + run
    run = run + _dot(in_group, jnp.ones(tri_ref.shape, BF16))
    run_ref[...] = run
    count_ref[...] = run[:, :LANES]

    row_ref[:, :D_MODEL] = x
    for j, c in enumerate(range(0, tokens, LANES)):
        rank = jnp.sum(members[j] * earlier[:, c:c + LANES], axis=0, keepdims=True)
        row_ref[c:c + LANES, D_MODEL:] = gates[j].T
        where_ref[0, :, c:c + LANES] = jnp.where(few[:8] == 0.0, groups[j],
                                                 jnp.where(few[:8] == 1.0, rank, 0.0)).astype(jnp.int32)


def _router(x2, norm, w_group, b_group, w_router, b_router, attn=None):
    T, D = x2.shape
    tm = min(ROUTER_TILE, T)
    groups = slice(ROUTER_GROUP_ROW0, ROUTER_GROUP_ROW0 + N_GROUPS)
    w = jnp.zeros((ROUTER_ROWS, D), F32).at[:N_EXPERTS].set(w_router.T).at[groups].set(w_group.T)
    w_hi = w.astype(BF16)
    w_lo = (w - w_hi.astype(F32)).astype(BF16)
    bias = jnp.zeros((ROUTER_ROWS,), F32).at[:N_EXPERTS].set(b_router).at[groups].set(b_group)
    bias = jnp.broadcast_to(bias[:, None], (ROUTER_ROWS, tm))
    idx = jnp.arange(tm)
    tri = (idx[:, None] < idx[None, :]).astype(BF16)
    mixer_specs, mixer_args = [], []
    if attn is not None:
        mixer_specs = [pl.BlockSpec((tm, D), lambda i: (i, 0)), pl.BlockSpec((D, D), lambda i: (0, 0))]
        mixer_args = [attn[0], attn[1].astype(BF16)]
    return pl.pallas_call(
        functools.partial(_router_kernel, mixer_out=attn is not None),
        grid=(T // tm,),
        in_specs=mixer_specs + [
            pl.BlockSpec((tm, D), lambda i: (i, 0)),
            pl.BlockSpec((1, D), lambda i: (0, 0)),
            pl.BlockSpec((ROUTER_ROWS, D), lambda i: (0, 0)),
            pl.BlockSpec((ROUTER_ROWS, D), lambda i: (0, 0)),
            pl.BlockSpec((ROUTER_ROWS, tm), lambda i: (0, 0)),
            pl.BlockSpec((tm, tm), lambda i: (0, 0)),
        ],
        out_specs=[pl.BlockSpec((tm, ROW_WIDTH), lambda i: (i, 0)),
                   pl.BlockSpec((1, 8, tm), lambda i: (i, 0, 0)),
                   pl.BlockSpec((ROUTER_COUNT_ROWS, LANES), lambda i: (0, 0))],
        out_shape=[jax.ShapeDtypeStruct((T, ROW_WIDTH), F32),
                   jax.ShapeDtypeStruct((T // tm, 8, tm), jnp.int32),
                   jax.ShapeDtypeStruct((ROUTER_COUNT_ROWS, LANES), F32)],
        scratch_shapes=[pltpu.VMEM((ROUTER_COUNT_ROWS, tm), F32)],
        compiler_params=_params("arbitrary"),
        name="moe_router",
    )(*mixer_args, x2, norm.reshape(1, D), w_hi, w_lo, bias, tri)


def _experts_kernel(tile_group_ref, tile_rows_ref, tok_prev_ref, tok_ref, tok_next_ref, rows_ref, norm_ref,
                    wg_ref, wu_ref, wd_ref, out_ref, rbuf, obuf, in_sem, out_sem):
    r = pl.program_id(0)
    last = pl.num_programs(0) - 1
    slot = r % 2

    def gather(dst_slot, tok):
        for t in range(MOE_ROW_TILE):
            pltpu.make_async_copy(rows_ref.at[pl.ds(tok[t], 1)], rbuf.at[dst_slot, pl.ds(t, 1)],
                                  in_sem.at[dst_slot]).start(priority=t % DMA_PRIORITIES)

    def row_copy(src_slot, t, tok):
        return pltpu.make_async_copy(obuf.at[src_slot, pl.ds(t, 1)], out_ref.at[pl.ds(tok[t], 1)],
                                     out_sem.at[src_slot])

    def send_rows(n, src_slot, tok):
        def body(t, carry):
            row_copy(src_slot, t, tok).start()
            return carry
        lax.fori_loop(0, n, body, 0)

    def wait_rows(n, src_slot):
        @pl.when(n == MOE_ROW_TILE)
        def _():
            pltpu.make_async_copy(obuf.at[src_slot], obuf.at[src_slot], out_sem.at[src_slot]).wait()

        @pl.when(n < MOE_ROW_TILE)
        def _():
            def body(t, carry):
                pltpu.make_async_copy(obuf.at[src_slot, pl.ds(0, 1)], obuf.at[src_slot, pl.ds(0, 1)],
                                      out_sem.at[src_slot]).wait()
                return carry
            lax.fori_loop(0, n, body, 0)

    rows_now = tile_rows_ref[r]
    rows_prev = jnp.where(r >= 1, tile_rows_ref[jnp.maximum(r - 1, 0)], 0)

    @pl.when((r == 0) & (rows_now > 0))
    def _():
        gather(slot, tok_ref)

    @pl.when(r >= 2)
    def _():
        wait_rows(tile_rows_ref[jnp.maximum(r - 2, 0)], slot)

    rows_next = jnp.where(r < last, tile_rows_ref[jnp.minimum(r + 1, last)], 0)

    @pl.when(rows_next > 0)
    def _():
        gather(1 - slot, tok_next_ref)

    @pl.when(rows_now > 0)
    def _():
        pltpu.make_async_copy(rbuf.at[slot], rbuf.at[slot], in_sem.at[slot]).wait()

    @pl.when(rows_prev == MOE_ROW_TILE)
    def _():
        for t in range(MOE_ROW_TILE):
            row_copy(1 - slot, t, tok_prev_ref).start(priority=t % DMA_PRIORITIES)

    @pl.when(rows_prev < MOE_ROW_TILE)
    def _():
        send_rows(rows_prev, 1 - slot, tok_prev_ref)

    @pl.when(rows_now > 0)
    def _():
        x = rbuf[slot, :, :D_MODEL]
        gate = rbuf[slot, :, D_MODEL:]
        h = _rms(x, norm_ref[...]).astype(BF16)
        lane = lax.broadcasted_iota(jnp.int32, gate.shape, 1)
        e0 = tile_group_ref[r] * EXPERTS_PER_GROUP
        acc = x
        for e in range(EXPERTS_PER_GROUP):
            a = _dot(h, wg_ref[e])
            u = _dot(h, wu_ref[e])
            gcol = jnp.sum(jnp.where(lane == e0 + e, gate, 0.0), axis=1, keepdims=True)
            hid = (a * jax.nn.sigmoid(a)) * u * gcol
            acc = acc + _dot(hid.astype(BF16), wd_ref[e])
        obuf[slot] = acc

    @pl.when(r == last)
    def _():
        send_rows(rows_now, slot, tok_ref)
        wait_rows(rows_prev, 1 - slot)
        wait_rows(rows_now, slot)


def _experts(rows, tile_group, tile_rows, tok, norm, w_gate, w_up, w_down, layer):
    T = rows.shape[0]
    D = D_MODEL
    n_tiles = tok.shape[0] // MOE_ROW_TILE
    group_w = lambda r, tg, tr: (layer * N_GROUPS + tg[r], 0, 0)
    smem_tile = lambda index: pl.BlockSpec((MOE_ROW_TILE,), index, memory_space=pltpu.SMEM)
    return pl.pallas_call(
        _experts_kernel,
        grid_spec=pltpu.PrefetchScalarGridSpec(
            num_scalar_prefetch=2,
            grid=(n_tiles,),
            in_specs=[smem_tile(lambda r, tg, tr: (jnp.maximum(r - 1, 0),)),
                      smem_tile(lambda r, tg, tr: (r,)),
                      smem_tile(lambda r, tg, tr: (jnp.minimum(r + 1, n_tiles - 1),)),
                      pl.BlockSpec(memory_space=pl.ANY),
                      pl.BlockSpec((1, D), lambda r, tg, tr: (0, 0)),
                      pl.BlockSpec((EXPERTS_PER_GROUP, D, D_EXPERT), group_w),
                      pl.BlockSpec((EXPERTS_PER_GROUP, D, D_EXPERT), group_w),
                      pl.BlockSpec((EXPERTS_PER_GROUP, D_EXPERT, D), group_w)],
            out_specs=pl.BlockSpec(memory_space=pl.ANY),
            scratch_shapes=[pltpu.VMEM((2, MOE_ROW_TILE, ROW_WIDTH), F32), pltpu.VMEM((2, MOE_ROW_TILE, D), F32),
                            pltpu.SemaphoreType.DMA((2,)), pltpu.SemaphoreType.DMA((2,))],
        ),
        out_shape=jax.ShapeDtypeStruct((T, D), F32),
        compiler_params=_params("arbitrary"),
        name="moe_experts",
    )(tile_group, tile_rows, tok, tok, tok, rows, norm.reshape(1, D), w_gate.astype(BF16), w_up.astype(BF16),
      w_down.astype(BF16))


def _invert_kernel(pad_lo_ref, pad_hi_ref, dest_ref, tok_ref):
    i = pl.program_id(0)
    tm = dest_ref.shape[0]

    @pl.when(i == 0)
    def _():
        for seg in range(N_GROUPS + 1):
            def clear(p, carry):
                tok_ref[p] = 0
                return carry
            lax.fori_loop(pad_lo_ref[seg], pad_hi_ref[seg], clear, 0)

    for t in range(tm):
        tok_ref[dest_ref[t]] = i * tm + t


def _invert(dest, pad_lo, pad_hi, total_rows):
    T = dest.shape[0]
    tm = min(INVERT_TILE, T)
    return pl.pallas_call(
        _invert_kernel,
        grid_spec=pltpu.PrefetchScalarGridSpec(
            num_scalar_prefetch=2,
            grid=(T // tm,),
            in_specs=[pl.BlockSpec((tm,), lambda i, lo, hi: (i,), memory_space=pltpu.SMEM)],
            out_specs=pl.BlockSpec(memory_space=pltpu.SMEM),
        ),
        out_shape=jax.ShapeDtypeStruct((total_rows,), jnp.int32),
        compiler_params=_params("arbitrary"),
        name="moe_invert",
    )(pad_lo, pad_hi, dest)


def _moe_layer(x, norm, w_group, b_group, w_router, b_router, w_gate, w_up, w_down, layer=0, attn=None):
    B, S, D = x.shape
    T = B * S
    rows, where, counts = _router(x.reshape(T, D), norm, w_group, b_group, w_router, b_router, attn)
    where = where.transpose(1, 0, 2).reshape(8, T)
    group, rank = where[0], where[1]
    counts = counts[:N_GROUPS, 0].astype(jnp.int32)
    tiles = (counts + MOE_ROW_TILE - 1) // MOE_ROW_TILE
    tile_end = jnp.cumsum(tiles)
    tile_first = tile_end - tiles
    dest = tile_first[group] * MOE_ROW_TILE + rank
    total_rows = T + MOE_PAD_ROWS
    pad_lo = jnp.concatenate([tile_first * MOE_ROW_TILE + counts, tile_end[-1:] * MOE_ROW_TILE]).astype(jnp.int32)
    pad_hi = jnp.concatenate([tile_end * MOE_ROW_TILE, jnp.full((1,), total_rows)]).astype(jnp.int32)
    tok = _invert(dest.astype(jnp.int32), pad_lo, pad_hi, total_rows)
    tile_idx = jnp.arange(total_rows // MOE_ROW_TILE)
    tile_group = jnp.minimum(jnp.sum(tile_idx[:, None] >= tile_end[None, :], axis=1), N_GROUPS - 1)
    tile_rows = jnp.clip(counts[tile_group] - (tile_idx - tile_first[tile_group]) * MOE_ROW_TILE, 0, MOE_ROW_TILE)
    tile_rows = jnp.where(tile_idx < tile_end[-1], tile_rows, 0)
    return _experts(rows, tile_group.astype(jnp.int32), tile_rows.astype(jnp.int32), tok, norm,
                    w_gate, w_up, w_down, layer).reshape(B, S, D)


def _qkv_kernel(x_ref, norm_ref, w_ref, seg_ref, segt_ref, qg_ref, kg_ref, q_ref, k_ref, v_ref):
    D = x_ref.shape[1]
    h = _rms(x_ref[...], norm_ref[...]).astype(BF16)
    qkv = _dot(h, w_ref[...])
    seg = seg_ref[...]
    segt = segt_ref[...]

    def head_norm(t, gain):
        ms = _dot((t * t).astype(BF16), seg) * (1.0 / SB_HEAD_DIM)
        r_hi, r_lo = _split_bf16(lax.rsqrt(ms + RMS_EPS))
        return t * (_dot(r_hi, segt) + _dot(r_lo, segt)) * gain

    q_ref[...] = head_norm(qkv[:, :D], qg_ref[...]).astype(BF16)
    k_ref[...] = head_norm(qkv[:, D:2 * D], kg_ref[...]).astype(BF16)
    v_ref[...] = qkv[:, 2 * D:].astype(BF16)


def _qkv(x2, norm, w_qkv, q_gain, k_gain):
    T, D = x2.shape
    tm = min(QKV_TILE, T)
    head_of = jnp.arange(D) // SB_HEAD_DIM
    seg = (head_of[:, None] == jnp.arange(LANES)[None, :]).astype(BF16)
    segt = seg.T
    qg = (jnp.tile(q_gain, SB_HEADS) * (SB_HEAD_DIM ** -0.5 * LOG2_E)).reshape(1, D)
    kg = jnp.tile(k_gain, SB_HEADS).reshape(1, D)
    row = pl.BlockSpec((tm, D), lambda i: (i, 0))
    vec = pl.BlockSpec((1, D), lambda i: (0, 0))
    return pl.pallas_call(
        _qkv_kernel,
        grid=(T // tm,),
        in_specs=[row, vec, pl.BlockSpec((D, 3 * D), lambda i: (0, 0)),
                  pl.BlockSpec((D, LANES), lambda i: (0, 0)), pl.BlockSpec((LANES, D), lambda i: (0, 0)),
                  vec, vec],
        out_specs=[row, row, row],
        out_shape=[jax.ShapeDtypeStruct((T, D), BF16)] * 3,
        compiler_params=_params("parallel"),
        name="sb_qkv",
    )(x2, norm.reshape(1, D), w_qkv.astype(BF16), seg, segt, qg, kg)


def _attn_kernel(q_ref, k_ref, v_ref, u_ref, ceil_ref, bias_ref, o_ref, qh_buf, acc_ref, rem_ref, *, t, subs):
    step = pl.program_id(2)
    near = ATTN_NEAR_TILES * t
    lane = lax.broadcasted_iota(jnp.int32, (t, LANES), 1)
    first_head = lane < SB_HEAD_DIM
    for s in range(subs):
        q2 = q_ref[s * t:(s + 1) * t, :]
        zero = jnp.zeros_like(q2)
        qh_buf[s, 0] = jnp.where(first_head, q2, zero)
        qh_buf[s, 1] = jnp.where(first_head, zero, q2)

    def keys(start, n):
        return k_ref[0, pl.ds(start, n), :], v_ref[0, pl.ds(start, n), :]

    def scores(s, hd, kt):
        return lax.dot_general(qh_buf[s, hd], kt, (((1,), (1,)), ((), ())), preferred_element_type=F32)

    def drop_of(z, ceil=None):
        drop = jnp.maximum(z, 0.0) + jnp.log2(1.0 + jnp.exp(jnp.abs(z) * (-LN_2)))
        log_beta = z - drop
        if ceil is not None:
            drop = jnp.minimum(drop, ceil)
        return drop, log_beta

    def suffix_sums(drop):
        n = drop.shape[1]
        return _dot(drop.astype(BF16), u_ref[:n, :n])

    def weights(log_beta, later, bias=None):
        arg = log_beta - later
        if bias is not None:
            arg = arg + bias
        return jnp.exp2(arg).astype(BF16)

    def diagonal(s, hd, kd):
        drop, log_beta = drop_of(scores(s, hd, kd), ceil_ref[...])
        c = suffix_sums(drop)
        return weights(log_beta, c, bias_ref[...]), c[:, 0:1] + drop[:, 0:1]

    def sweep(s, qi, first):
        def rem_min():
            return jnp.min(jnp.minimum(rem_ref[s, 0], rem_ref[s, 1]))

        def more(carry):
            j, smallest_rem = carry
            return (j <= qi) & (smallest_rem < ZERO_WEIGHT_LOG2)

        def body(carry):
            j, _ = carry
            kt, vt = keys(pl.multiple_of((qi - j) * t, t), t)
            pv = []
            for hd in range(2):
                drop, log_beta = drop_of(scores(s, hd, kt))
                c = suffix_sums(drop)
                rem = rem_ref[s, hd]
                pv.append(_dot(weights(log_beta, c + rem), vt))
                rem_ref[s, hd] = rem + (c[:, 0:1] + drop[:, 0:1])
            acc_ref[s] += jnp.where(first_head, pv[0], pv[1])
            return j + 1, rem_min()

        lax.while_loop(more, body, (jnp.int32(first), rem_min()))

    def near_block(subset):
        tiles = {s: step * subs + s for s in subset}
        units = [(s, hd) for s in subset for hd in range(2)]
        diag = {s: keys(pl.multiple_of(qi * t, t), t) for s, qi in tiles.items()}
        left = {s: keys(pl.multiple_of((qi - ATTN_NEAR_TILES) * t, t), near) for s, qi in tiles.items()}
        values = {s: jnp.concatenate([diag[s][1], left[s][1]], axis=0) for s in subset}
        z = {(s, hd): (scores(s, hd, diag[s][0]), scores(s, hd, left[s][0])) for s, hd in units}
        dl = {un: (drop_of(z[un][0], ceil_ref[...]), drop_of(z[un][1])) for un in units}
        c = {un: (suffix_sums(dl[un][0][0]), suffix_sums(dl[un][1][0])) for un in units}
        pv = {}
        for s, hd in units:
            (d0, lb0), (d1, lb1) = dl[s, hd]
            c0, c1 = c[s, hd]
            rem1 = c0[:, 0:1] + d0[:, 0:1]
            a0 = weights(lb0, c0, bias_ref[...])
            a1 = weights(lb1, c1 + rem1)
            pv[s, hd] = _dot(jnp.concatenate([a0, a1], axis=1), values[s])
            rem_ref[s, hd] = rem1 + (c1[:, 0:1] + d1[:, 0:1])
        smallest_rem = None
        for s in subset:
            acc_ref[s] = jnp.where(first_head, pv[s, 0], pv[s, 1])
            for hd in range(2):
                rem = rem_ref[s, hd]
                smallest_rem = rem if smallest_rem is None else jnp.minimum(smallest_rem, rem)

        @pl.when(jnp.min(smallest_rem) < ZERO_WEIGHT_LOG2)
        def _():
            for s, qi in tiles.items():
                sweep(s, qi, ATTN_NEAR_TILES + 1)

    @pl.when(step > 0)
    def _():
        near_block(range(subs))

    @pl.when(step == 0)
    def _():
        for s in range(ATTN_NEAR_TILES):
            kd, vd = keys(s * t, t)
            pv = []
            for hd in range(2):
                a, rem1 = diagonal(s, hd, kd)
                pv.append(_dot(a, vd))
                rem_ref[s, hd] = rem1
            acc_ref[s] = jnp.where(first_head, pv[0], pv[1])
            sweep(s, s, 1)
        near_block(range(ATTN_NEAR_TILES, subs))

    for s in range(subs):
        o_ref[s * t:(s + 1) * t, :] = acc_ref[s].astype(o_ref.dtype)


def _attention(q, k, v, B, S):
    T, D = q.shape
    t = min(ATTN_TILE, S)
    subs = ATTN_TILES_PER_STEP
    assert subs >= ATTN_NEAR_TILES and S % (t * subs) == 0
    steps = S // (t * subs)
    near = ATTN_NEAR_TILES * t
    idx = jnp.arange(near)
    u = (idx[:, None] > idx[None, :]).astype(BF16)
    causal = idx[None, :t] < idx[:t, None]
    ceil = jnp.where(causal, jnp.inf, 0.0).astype(F32)
    bias = jnp.where(causal, 0.0, -jnp.inf).astype(F32)
    k3 = k.reshape(B, S, D)
    v3 = v.reshape(B, S, D)
    kv_spec = pl.BlockSpec((1, S, LANES), lambda b, hp, i: (b, 0, hp))
    table_spec = pl.BlockSpec((t, t), lambda b, hp, i: (0, 0))
    q_spec = pl.BlockSpec((subs * t, LANES), lambda b, hp, i: (b * steps + i, hp))
    return pl.pallas_call(
        functools.partial(_attn_kernel, t=t, subs=subs),
        grid=(B, D // LANES, steps),
        in_specs=[q_spec, kv_spec, kv_spec, pl.BlockSpec((near, near), lambda b, hp, i: (0, 0)),
                  table_spec, table_spec],
        out_specs=q_spec,
        out_shape=jax.ShapeDtypeStruct((T, D), BF16),
        scratch_shapes=[pltpu.VMEM((subs, 2, t, LANES), BF16), pltpu.VMEM((subs, t, LANES), F32),
                        pltpu.VMEM((subs, 2, t, 1), F32)],
        compiler_params=_params("parallel", "parallel", "arbitrary"),
        name="sb_attention",
    )(q, k3, v3, u, ceil, bias)


def _sb_mixer(x, norm, w_qkv, q_gain, k_gain):
    B, S, D = x.shape
    q, k, v = _qkv(x.reshape(B * S, D), norm, w_qkv, q_gain, k_gain)
    return _attention(q, k, v, B, S)


def kernel(x, pool_norm, pool_w, pool_b, pool_scale, sb_norm, sb_w_qkv, sb_q_gain, sb_k_gain, sb_w_o, moe_norm, moe_w_group, moe_b_group, moe_w_router, moe_b_router, moe_w_gate, moe_w_up, moe_w_down):
    stacked = lambda w: w.astype(BF16).reshape((DEPTH * N_EXPERTS,) + w.shape[2:])
    w_gate, w_up, w_down = stacked(moe_w_gate), stacked(moe_w_up), stacked(moe_w_down)
    for i in range(DEPTH):
        j = i // 2
        attn = None
        if i % 2 == 0:
            x = _pool_layer(x, pool_norm[j], pool_w[j], pool_b[j], pool_scale[j])
        else:
            attn = (_sb_mixer(x, sb_norm[j], sb_w_qkv[j], sb_q_gain[j], sb_k_gain[j]), sb_w_o[j])
        x = _moe_layer(x, moe_norm[i], moe_w_group[i], moe_b_group[i], moe_w_router[i], moe_b_router[i],
                       w_gate, w_up, w_down, layer=i, attn=attn)
    return x
```

```python
import functools

import jax
import jax.numpy as jnp
from jax import lax
from jax.experimental import pallas as pl
from jax.experimental.pallas import tpu as pltpu

F32 = jnp.float32
BF16 = jnp.bfloat16

D_MODEL = 1024
DEPTH = 4
POOL_WINDOWS = (2, 4, 8, 16)
POOL_GROUP_DIM = D_MODEL // len(POOL_WINDOWS)
POOL_HALO = 16
POOL_LEAD = 8
SB_HEADS = 16
SB_HEAD_DIM = D_MODEL // SB_HEADS
N_GROUPS = 4
EXPERTS_PER_GROUP = 8
N_EXPERTS = N_GROUPS * EXPERTS_PER_GROUP
D_EXPERT = D_MODEL // 4
RMS_EPS = 1e-6
LOG2_E = 1.4426950408889634
LN_2 = 0.6931471805599453
ZERO_WEIGHT_LOG2 = 150.0

LANES = 128
MXU_DEPTH = 256
DMA_PRIORITIES = 2
VMEM_LIMIT_BYTES = 56 * 1024 * 1024

POOL_TILE = 1024
ROUTER_TILE = 1024
INVERT_TILE = 2048
QKV_TILE = 512
ATTN_TILE = 128
ATTN_NEAR_TILES = 2
ATTN_TILES_PER_STEP = 16


def _params(*sem):
    return pltpu.CompilerParams(dimension_semantics=sem, vmem_limit_bytes=VMEM_LIMIT_BYTES)


def _rms(v, gain):
    return v * lax.rsqrt(jnp.mean(v * v, axis=-1, keepdims=True) + RMS_EPS) * gain


def _dot(a, b):
    return jnp.dot(a, b, preferred_element_type=F32)


def _split_bf16(v):
    hi = v.astype(BF16)
    lo = (v - hi.astype(F32)).astype(BF16)
    return hi, lo


def _pool_kernel(xprev_ref, x_ref, norm_ref, w_ref, b_ref, scale_ref, o_ref, sums):
    i = pl.program_id(1)
    ts = x_ref.shape[1]
    x = x_ref[0]
    gain = norm_ref[...]
    h = _rms(x, gain)
    hprev = jnp.where(i > 0, _rms(xprev_ref[0], gain), 0.0)
    top, lo = POOL_LEAD, POOL_LEAD + POOL_HALO
    n = lo + ts
    sums[:, 0:top, :] = jnp.zeros((sums.shape[0], top, sums.shape[2]), F32)
    sums[0, top:lo, :] = hprev
    sums[0, lo:n, :] = h
    levels = len(POOL_WINDOWS)
    for k in range(1, levels):
        shift, c0 = 2 ** (k - 1), (k - 1) * POOL_GROUP_DIM
        sums[k, top:n, c0:] = sums[k - 1, top:n, c0:] + sums[k - 1, top - shift:n - shift, c0:]
    pos = i * ts + lax.broadcasted_iota(jnp.int32, (ts, 1), 0)
    for g, win in enumerate(POOL_WINDOWS):
        c0 = g * POOL_GROUP_DIM
        cols = slice(c0, c0 + POOL_GROUP_DIM)
        hg = h[:, cols]
        if g + 1 < levels:
            acc = sums[g + 1, lo:n, cols]
        else:
            half = win // 2
            acc = sums[g, lo:n, cols] + sums[g, lo - half:n - half, cols]
        count = jnp.minimum(pos + 1, win).astype(F32)
        diff = acc / count - hg
        y = _dot(diff.astype(BF16), w_ref[g]) + b_ref[g]
        o_ref[0, :, cols] = x[:, cols] + y * scale_ref[:, cols]


def _pool_layer(x, norm, w, b, scale):
    B, S, D = x.shape
    ts = min(POOL_TILE, S)
    halo_blocks = ts // POOL_HALO
    return pl.pallas_call(
        _pool_kernel,
        grid=(B, S // ts),
        in_specs=[
            pl.BlockSpec((1, POOL_HALO, D), lambda bi, i: (bi, jnp.maximum(i * halo_blocks - 1, 0), 0)),
            pl.BlockSpec((1, ts, D), lambda bi, i: (bi, i, 0)),
            pl.BlockSpec((1, D), lambda bi, i: (0, 0)),
            pl.BlockSpec((len(POOL_WINDOWS), POOL_GROUP_DIM, POOL_GROUP_DIM), lambda bi, i: (0, 0, 0)),
            pl.BlockSpec((len(POOL_WINDOWS), 1, POOL_GROUP_DIM), lambda bi, i: (0, 0, 0)),
            pl.BlockSpec((1, D), lambda bi, i: (0, 0)),
        ],
        out_specs=pl.BlockSpec((1, ts, D), lambda bi, i: (bi, i, 0)),
        out_shape=jax.ShapeDtypeStruct((B, S, D), F32),
        scratch_shapes=[pltpu.VMEM((len(POOL_WINDOWS), POOL_LEAD + POOL_HALO + ts, D), F32)],
        compiler_params=_params("parallel", "parallel"),
        name="pool_layer",
    )(x, x, norm.reshape(1, D), w.astype(BF16), b.reshape(len(POOL_WINDOWS), 1, POOL_GROUP_DIM),
      scale.reshape(1, D))


ROUTER_GROUP_ROW0 = N_EXPERTS
ROUTER_ROWS = LANES
ROUTER_USED_ROWS = 40
ROUTER_COUNT_ROWS = 16
ROW_WIDTH = D_MODEL + LANES
MOE_ROW_TILE = 512
MOE_PAD_ROWS = N_GROUPS * MOE_ROW_TILE


def _router_kernel(*refs, mixer_out):
    if mixer_out:
        o_ref, wo_ref, *refs = refs
    x_ref, norm_ref, whi_ref, wlo_ref, bias_ref, tri_ref, row_ref, where_ref, count_ref, run_ref = refs

    @pl.when(pl.program_id(0) == 0)
    def _():
        run_ref[...] = jnp.zeros_like(run_ref)

    x = x_ref[...]
    if mixer_out:
        x = x + _dot(o_ref[...], wo_ref[...])
    h = _rms(x, norm_ref[...])
    h_hi, h_lo = _split_bf16(h)
    w_hi = whi_ref[...]

    def per_token(w, t):
        return sum(lax.dot_general(w[:, c:c + MXU_DEPTH], t[:, c:c + MXU_DEPTH], (((1,), (1,)), ((), ())),
                                   preferred_element_type=F32) for c in range(0, w.shape[1], MXU_DEPTH))

    all_logits = per_token(w_hi, h_hi) + per_token(w_hi, h_lo) + per_token(wlo_ref[...], h_hi) + bias_ref[...]
    tokens = all_logits.shape[1]
    shape = (ROUTER_USED_ROWS, LANES)
    row = lax.broadcasted_iota(jnp.int32, shape, 0).astype(F32)
    all_rows = lax.broadcasted_iota(jnp.int32, (ROUTER_ROWS, LANES), 0).astype(F32)
    few = row[:ROUTER_COUNT_ROWS]
    neg_inf = F32(-jnp.inf)
    no_row = F32(4 * ROUTER_ROWS)

    def first_argmax(v):
        m = jnp.max(v, axis=0, keepdims=True)
        idx = jnp.min(jnp.where(v == m, row, no_row), axis=0, keepdims=True)
        return m, idx

    gates, groups, members = [], [], []
    for c in range(0, tokens, LANES):
        logits = all_logits[:ROUTER_USED_ROWS, c:c + LANES]
        is_group = (row >= ROUTER_GROUP_ROW0) & (row < ROUTER_GROUP_ROW0 + N_GROUPS)
        gl = jnp.where(is_group, logits, neg_inf)
        gmax, grow = first_argmax(gl)
        g_prob = 1.0 / jnp.sum(jnp.exp(gl - gmax), axis=0, keepdims=True)
        group = grow - ROUTER_GROUP_ROW0
        e0 = group * EXPERTS_PER_GROUP
        el = jnp.where((row >= e0) & (row < e0 + EXPERTS_PER_GROUP), logits, neg_inf)
        m1, i1 = first_argmax(el)
        m2, i2 = first_argmax(jnp.where(row == i1, neg_inf, el))
        r = jnp.exp(m2 - m1)
        w1 = 1.0 / (1.0 + r)
        gates.append(jnp.where(all_rows == i1, g_prob * w1, 0.0) + jnp.where(all_rows == i2, g_prob * (r * w1), 0.0))
        groups.append(group)
        members.append(jnp.where(few == group, 1.0, 0.0))

    in_group = jnp.concatenate(members, axis=1).astype(BF16)
    run = run_ref[...]
    earlier = _dot(in_group, tri_ref[...]) + run
    run = run + _dot(in_group, jnp.ones(tri_ref.shape, BF16))
    run_ref[...] = run
    count_ref[...] = run[:, :LANES]

    row_ref[:, :D_MODEL] = x
    for j, c in enumerate(range(0, tokens, LANES)):
        rank = jnp.sum(members[j] * earlier[:, c:c + LANES], axis=0, keepdims=True)
        row_ref[c:c + LANES, D_MODEL:] = gates[j].T
        where_ref[0, :, c:c + LANES] = jnp.where(few[:8] == 0.0, groups[j],
                                                 jnp.where(few[:8] == 1.0, rank, 0.0)).astype(jnp.int32)


def _router(x2, norm, w_group, b_group, w_router, b_router, attn=None):
    T, D = x2.shape
    tm = min(ROUTER_TILE, T)
    groups = slice(ROUTER_GROUP_ROW0, ROUTER_GROUP_ROW0 + N_GROUPS)
    w = jnp.zeros((ROUTER_ROWS, D), F32).at[:N_EXPERTS].set(w_router.T).at[groups].set(w_group.T)
    w_hi = w.astype(BF16)
    w_lo = (w - w_hi.astype(F32)).astype(BF16)
    bias = jnp.zeros((ROUTER_ROWS,), F32).at[:N_EXPERTS].set(b_router).at[groups].set(b_group)
    bias = jnp.broadcast_to(bias[:, None], (ROUTER_ROWS, tm))
    idx = jnp.arange(tm)
    tri = (idx[:, None] < idx[None, :]).astype(BF16)
    mixer_specs, mixer_args = [], []
    if attn is not None:
        mixer_specs = [pl.BlockSpec((tm, D), lambda i: (i, 0)), pl.BlockSpec((D, D), lambda i: (0, 0))]
        mixer_args = [attn[0], attn[1].astype(BF16)]
    return pl.pallas_call(
        functools.partial(_router_kernel, mixer_out=attn is not None),
        grid=(T // tm,),
        in_specs=mixer_specs + [
            pl.BlockSpec((tm, D), lambda i: (i, 0)),
            pl.BlockSpec((1, D), lambda i: (0, 0)),
            pl.BlockSpec((ROUTER_ROWS, D), lambda i: (0, 0)),
            pl.BlockSpec((ROUTER_ROWS, D), lambda i: (0, 0)),
            pl.BlockSpec((ROUTER_ROWS, tm), lambda i: (0, 0)),
            pl.BlockSpec((tm, tm), lambda i: (0, 0)),
        ],
        out_specs=[pl.BlockSpec((tm, ROW_WIDTH), lambda i: (i, 0)),
                   pl.BlockSpec((1, 8, tm), lambda i: (i, 0, 0)),
                   pl.BlockSpec((ROUTER_COUNT_ROWS, LANES), lambda i: (0, 0))],
        out_shape=[jax.ShapeDtypeStruct((T, ROW_WIDTH), F32),
                   jax.ShapeDtypeStruct((T // tm, 8, tm), jnp.int32),
                   jax.ShapeDtypeStruct((ROUTER_COUNT_ROWS, LANES), F32)],
        scratch_shapes=[pltpu.VMEM((ROUTER_COUNT_ROWS, tm), F32)],
        compiler_params=_params("arbitrary"),
        name="moe_router",
    )(*mixer_args, x2, norm.reshape(1, D), w_hi, w_lo, bias, tri)


def _experts_kernel(tile_group_ref, tile_rows_ref, tok_prev_ref, tok_ref, tok_next_ref, rows_ref, norm_ref,
                    wg_ref, wu_ref, wd_ref, out_ref, rbuf, obuf, in_sem, out_sem):
    r = pl.program_id(0)
    last = pl.num_programs(0) - 1
    slot = r % 2

    def gather(dst_slot, tok):
        for t in range(MOE_ROW_TILE):
            pltpu.make_async_copy(rows_ref.at[pl.ds(tok[t], 1)], rbuf.at[dst_slot, pl.ds(t, 1)],
                                  in_sem.at[dst_slot]).start(priority=t % DMA_PRIORITIES)

    def row_copy(src_slot, t, tok):
        return pltpu.make_async_copy(obuf.at[src_slot, pl.ds(t, 1)], out_ref.at[pl.ds(tok[t], 1)],
                                     out_sem.at[src_slot])

    def send_rows(n, src_slot, tok):
        def body(t, carry):
            row_copy(src_slot, t, tok).start()
            return carry
        lax.fori_loop(0, n, body, 0)

    def wait_rows(n, src_slot):
        @pl.when(n == MOE_ROW_TILE)
        def _():
            pltpu.make_async_copy(obuf.at[src_slot], obuf.at[src_slot], out_sem.at[src_slot]).wait()

        @pl.when(n < MOE_ROW_TILE)
        def _():
            def body(t, carry):
                pltpu.make_async_copy(obuf.at[src_slot, pl.ds(0, 1)], obuf.at[src_slot, pl.ds(0, 1)],
                                      out_sem.at[src_slot]).wait()
                return carry
            lax.fori_loop(0, n, body, 0)

    rows_now = tile_rows_ref[r]
    rows_prev = jnp.where(r >= 1, tile_rows_ref[jnp.maximum(r - 1, 0)], 0)

    @pl.when((r == 0) & (rows_now > 0))
    def _():
        gather(slot, tok_ref)

    @pl.when(r >= 2)
    def _():
        wait_rows(tile_rows_ref[jnp.maximum(r - 2, 0)], slot)

    rows_next = jnp.where(r < last, tile_rows_ref[jnp.minimum(r + 1, last)], 0)

    @pl.when(rows_next > 0)
    def _():
        gather(1 - slot, tok_next_ref)

    @pl.when(rows_now > 0)
    def _():
        pltpu.make_async_copy(rbuf.at[slot], rbuf.at[slot], in_sem.at[slot]).wait()

    @pl.when(rows_prev == MOE_ROW_TILE)
    def _():
        for t in range(MOE_ROW_TILE):
            row_copy(1 - slot, t, tok_prev_ref).start(priority=t % DMA_PRIORITIES)

    @pl.when(rows_prev < MOE_ROW_TILE)
    def _():
        send_rows(rows_prev, 1 - slot, tok_prev_ref)

    @pl.when(rows_now > 0)
    def _():
        x = rbuf[slot, :, :D_MODEL]
        gate = rbuf[slot, :, D_MODEL:]
        h = _rms(x, norm_ref[...]).astype(BF16)
        lane = lax.broadcasted_iota(jnp.int32, gate.shape, 1)
        e0 = tile_group_ref[r] * EXPERTS_PER_GROUP
        acc = x
        for e in range(EXPERTS_PER_GROUP):
            a = _dot(h, wg_ref[e])
            u = _dot(h, wu_ref[e])
            gcol = jnp.sum(jnp.where(lane == e0 + e, gate, 0.0), axis=1, keepdims=True)
            hid = (a * jax.nn.sigmoid(a)) * u * gcol
            acc = acc + _dot(hid.astype(BF16), wd_ref[e])
        obuf[slot] = acc

    @pl.when(r == last)
    def _():
        send_rows(rows_now, slot, tok_ref)
        wait_rows(rows_prev, 1 - slot)
        wait_rows(rows_now, slot)


def _experts(rows, tile_group, tile_rows, tok, norm, w_gate, w_up, w_down, layer):
    T = rows.shape[0]
    D = D_MODEL
    n_tiles = tok.shape[0] // MOE_ROW_TILE
    group_w = lambda r, tg, tr: (layer * N_GROUPS + tg[r], 0, 0)
    smem_tile = lambda index: pl.BlockSpec((MOE_ROW_TILE,), index, memory_space=pltpu.SMEM)
    return pl.pallas_call(
        _experts_kernel,
        grid_spec=pltpu.PrefetchScalarGridSpec(
            num_scalar_prefetch=2,
            grid=(n_tiles,),
            in_specs=[smem_tile(lambda r, tg, tr: (jnp.maximum(r - 1, 0),)),
                      smem_tile(lambda r, tg, tr: (r,)),
                      smem_tile(lambda r, tg, tr: (jnp.minimum(r + 1, n_tiles - 1),)),
                      pl.BlockSpec(memory_space=pl.ANY),
                      pl.BlockSpec((1, D), lambda r, tg, tr: (0, 0)),
                      pl.BlockSpec((EXPERTS_PER_GROUP, D, D_EXPERT), group_w),
                      pl.BlockSpec((EXPERTS_PER_GROUP, D, D_EXPERT), group_w),
                      pl.BlockSpec((EXPERTS_PER_GROUP, D_EXPERT, D), group_w)],
            out_specs=pl.BlockSpec(memory_space=pl.ANY),
            scratch_shapes=[pltpu.VMEM((2, MOE_ROW_TILE, ROW_WIDTH), F32), pltpu.VMEM((2, MOE_ROW_TILE, D), F32),
                            pltpu.SemaphoreType.DMA((2,)), pltpu.SemaphoreType.DMA((2,))],
        ),
        out_shape=jax.ShapeDtypeStruct((T, D), F32),
        compiler_params=_params("arbitrary"),
        name="moe_experts",
    )(tile_group, tile_rows, tok, tok, tok, rows, norm.reshape(1, D), w_gate.astype(BF16), w_up.astype(BF16),
      w_down.astype(BF16))


def _invert_kernel(pad_lo_ref, pad_hi_ref, dest_ref, tok_ref):
    i = pl.program_id(0)
    tm = dest_ref.shape[0]

    @pl.when(i == 0)
    def _():
        for seg in range(N_GROUPS + 1):
            def clear(p, carry):
                tok_ref[p] = 0
                return carry
            lax.fori_loop(pad_lo_ref[seg], pad_hi_ref[seg], clear, 0)

    for t in range(tm):
        tok_ref[dest_ref[t]] = i * tm + t


def _invert(dest, pad_lo, pad_hi, total_rows):
    T = dest.shape[0]
    tm = min(INVERT_TILE, T)
    return pl.pallas_call(
        _invert_kernel,
        grid_spec=pltpu.PrefetchScalarGridSpec(
            num_scalar_prefetch=2,
            grid=(T // tm,),
            in_specs=[pl.BlockSpec((tm,), lambda i, lo, hi: (i,), memory_space=pltpu.SMEM)],
            out_specs=pl.BlockSpec(memory_space=pltpu.SMEM),
        ),
        out_shape=jax.ShapeDtypeStruct((total_rows,), jnp.int32),
        compiler_params=_params("arbitrary"),
        name="moe_invert",
    )(pad_lo, pad_hi, dest)


def _moe_layer(x, norm, w_group, b_group, w_router, b_router, w_gate, w_up, w_down, layer=0, attn=None):
    B, S, D = x.shape
    T = B * S
    rows, where, counts = _router(x.reshape(T, D), norm, w_group, b_group, w_router, b_router, attn)
    where = where.transpose(1, 0, 2).reshape(8, T)
    group, rank = where[0], where[1]
    counts = counts[:N_GROUPS, 0].astype(jnp.int32)
    tiles = (counts + MOE_ROW_TILE - 1) // MOE_ROW_TILE
    tile_end = jnp.cumsum(tiles)
    tile_first = tile_end - tiles
    dest = tile_first[group] * MOE_ROW_TILE + rank
    total_rows = T + MOE_PAD_ROWS
    pad_lo = jnp.concatenate([tile_first * MOE_ROW_TILE + counts, tile_end[-1:] * MOE_ROW_TILE]).astype(jnp.int32)
    pad_hi = jnp.concatenate([tile_end * MOE_ROW_TILE, jnp.full((1,), total_rows)]).astype(jnp.int32)
    tok = _invert(dest.astype(jnp.int32), pad_lo, pad_hi, total_rows)
    tile_idx = jnp.arange(total_rows // MOE_ROW_TILE)
    tile_group = jnp.minimum(jnp.sum(tile_idx[:, None] >= tile_end[None, :], axis=1), N_GROUPS - 1)
    tile_rows = jnp.clip(counts[tile_group] - (tile_idx - tile_first[tile_group]) * MOE_ROW_TILE, 0, MOE_ROW_TILE)
    tile_rows = jnp.where(tile_idx < tile_end[-1], tile_rows, 0)
    return _experts(rows, tile_group.astype(jnp.int32), tile_rows.astype(jnp.int32), tok, norm,
                    w_gate, w_up, w_down, layer).reshape(B, S, D)


def _qkv_kernel(x_ref, norm_ref, w_ref, seg_ref, segt_ref, qg_ref, kg_ref, q_ref, k_ref, v_ref):
    D = x_ref.shape[1]
    h = _rms(x_ref[...], norm_ref[...]).astype(BF16)
    qkv = _dot(h, w_ref[...])
    seg = seg_ref[...]
    segt = segt_ref[...]

    def head_norm(t, gain):
        ms = _dot((t * t).astype(BF16), seg) * (1.0 / SB_HEAD_DIM)
        r_hi, r_lo = _split_bf16(lax.rsqrt(ms + RMS_EPS))
        return t * (_dot(r_hi, segt) + _dot(r_lo, segt)) * gain

    q_ref[...] = head_norm(qkv[:, :D], qg_ref[...]).astype(BF16)
    k_ref[...] = head_norm(qkv[:, D:2 * D], kg_ref[...]).astype(BF16)
    v_ref[...] = qkv[:, 2 * D:].astype(BF16)


def _qkv(x2, norm, w_qkv, q_gain, k_gain):
    T, D = x2.shape
    tm = min(QKV_TILE, T)
    head_of = jnp.arange(D) // SB_HEAD_DIM
    seg = (head_of[:, None] == jnp.arange(LANES)[None, :]).astype(BF16)
    segt = seg.T
    qg = (jnp.tile(q_gain, SB_HEADS) * (SB_HEAD_DIM ** -0.5 * LOG2_E)).reshape(1, D)
    kg = jnp.tile(k_gain, SB_HEADS).reshape(1, D)
    row = pl.BlockSpec((tm, D), lambda i: (i, 0))
    vec = pl.BlockSpec((1, D), lambda i: (0, 0))
    return pl.pallas_call(
        _qkv_kernel,
        grid=(T // tm,),
        in_specs=[row, vec, pl.BlockSpec((D, 3 * D), lambda i: (0, 0)),
                  pl.BlockSpec((D, LANES), lambda i: (0, 0)), pl.BlockSpec((LANES, D), lambda i: (0, 0)),
                  vec, vec],
        out_specs=[row, row, row],
        out_shape=[jax.ShapeDtypeStruct((T, D), BF16)] * 3,
        compiler_params=_params("parallel"),
        name="sb_qkv",
    )(x2, norm.reshape(1, D), w_qkv.astype(BF16), seg, segt, qg, kg)


def _attn_kernel(q_ref, k_ref, v_ref, u_ref, ceil_ref, bias_ref, o_ref, qh_buf, acc_ref, rem_ref, *, t, subs):
    step = pl.program_id(2)
    near = ATTN_NEAR_TILES * t
    lane = lax.broadcasted_iota(jnp.int32, (t, LANES), 1)
    first_head = lane < SB_HEAD_DIM
    for s in range(subs):
        q2 = q_ref[s * t:(s + 1) * t, :]
        zero = jnp.zeros_like(q2)
        qh_buf[s, 0] = jnp.where(first_head, q2, zero)
        qh_buf[s, 1] = jnp.where(first_head, zero, q2)

    def keys(start, n):
        return k_ref[0, pl.ds(start, n), :], v_ref[0, pl.ds(start, n), :]

    def scores(s, hd, kt):
        return lax.dot_general(qh_buf[s, hd], kt, (((1,), (1,)), ((), ())), preferred_element_type=F32)

    def drop_of(z, ceil=None):
        drop = jnp.maximum(z, 0.0) + jnp.log2(1.0 + jnp.exp(jnp.abs(z) * (-LN_2)))
        log_beta = z - drop
        if ceil is not None:
            drop = jnp.minimum(drop, ceil)
        return drop, log_beta

    def suffix_sums(drop):
        n = drop.shape[1]
        return _dot(drop.astype(BF16), u_ref[:n, :n])

    def weights(log_beta, later, bias=None):
        arg = log_beta - later
        if bias is not None:
            arg = arg + bias
        return jnp.exp2(arg).astype(BF16)

    def diagonal(s, hd, kd):
        drop, log_beta = drop_of(scores(s, hd, kd), ceil_ref[...])
        c = suffix_sums(drop)
        return weights(log_beta, c, bias_ref[...]), c[:, 0:1] + drop[:, 0:1]

    def sweep(s, qi, first):
        def rem_min():
            return jnp.min(jnp.minimum(rem_ref[s, 0], rem_ref[s, 1]))

        def more(carry):
            j, smallest_rem = carry
            return (j <= qi) & (smallest_rem < ZERO_WEIGHT_LOG2)

        def body(carry):
            j, _ = carry
            kt, vt = keys(pl.multiple_of((qi - j) * t, t), t)
            pv = []
            for hd in range(2):
                drop, log_beta = drop_of(scores(s, hd, kt))
                c = suffix_sums(drop)
                rem = rem_ref[s, hd]
                pv.append(_dot(weights(log_beta, c + rem), vt))
                rem_ref[s, hd] = rem + (c[:, 0:1] + drop[:, 0:1])
            acc_ref[s] += jnp.where(first_head, pv[0], pv[1])
            return j + 1, rem_min()

        lax.while_loop(more, body, (jnp.int32(first), rem_min()))

    def near_block(subset):
        tiles = {s: step * subs + s for s in subset}
        units = [(s, hd) for s in subset for hd in range(2)]
        diag = {s: keys(pl.multiple_of(qi * t, t), t) for s, qi in tiles.items()}
        left = {s: keys(pl.multiple_of((qi - ATTN_NEAR_TILES) * t, t), near) for s, qi in tiles.items()}
        values = {s: jnp.concatenate([diag[s][1], left[s][1]], axis=0) for s in subset}
        z = {(s, hd): (scores(s, hd, diag[s][0]), scores(s, hd, left[s][0])) for s, hd in units}
        dl = {un: (drop_of(z[un][0], ceil_ref[...]), drop_of(z[un][1])) for un in units}
        c = {un: (suffix_sums(dl[un][0][0]), suffix_sums(dl[un][1][0])) for un in units}
        pv = {}
        for s, hd in units:
            (d0, lb0), (d1, lb1) = dl[s, hd]
            c0, c1 = c[s, hd]
            rem1 = c0[:, 0:1] + d0[:, 0:1]
            a0 = weights(lb0, c0, bias_ref[...])
            a1 = weights(lb1, c1 + rem1)
            pv[s, hd] = _dot(jnp.concatenate([a0, a1], axis=1), values[s])
            rem_ref[s, hd] = rem1 + (c1[:, 0:1] + d1[:, 0:1])
        smallest_rem = None
        for s in subset:
            acc_ref[s] = jnp.where(first_head, pv[s, 0], pv[s, 1])
            for hd in range(2):
                rem = rem_ref[s, hd]
                smallest_rem = rem if smallest_rem is None else jnp.minimum(smallest_rem, rem)

        @pl.when(jnp.min(smallest_rem) < ZERO_WEIGHT_LOG2)
        def _():
            for s, qi in tiles.items():
                sweep(s, qi, ATTN_NEAR_TILES + 1)

    @pl.when(step > 0)
    def _():
        near_block(range(subs))

    @pl.when(step == 0)
    def _():
        for s in range(ATTN_NEAR_TILES):
            kd, vd = keys(s * t, t)
            pv = []
            for hd in range(2):
                a, rem1 = diagonal(s, hd, kd)
                pv.append(_dot(a, vd))
                rem_ref[s, hd] = rem1
            acc_ref[s] = jnp.where(first_head, pv[0], pv[1])
            sweep(s, s, 1)
        near_block(range(ATTN_NEAR_TILES, subs))

    for s in range(subs):
        o_ref[s * t:(s + 1) * t, :] = acc_ref[s].astype(o_ref.dtype)


def _attention(q, k, v, B, S):
    T, D = q.shape
    t = min(ATTN_TILE, S)
    subs = ATTN_TILES_PER_STEP
    assert subs >= ATTN_NEAR_TILES and S % (t * subs) == 0
    steps = S // (t * subs)
    near = ATTN_NEAR_TILES * t
    idx = jnp.arange(near)
    u = (idx[:, None] > idx[None, :]).astype(BF16)
    causal = idx[None, :t] < idx[:t, None]
    ceil = jnp.where(causal, jnp.inf, 0.0).astype(F32)
    bias = jnp.where(causal, 0.0, -jnp.inf).astype(F32)
    k3 = k.reshape(B, S, D)
    v3 = v.reshape(B, S, D)
    kv_spec = pl.BlockSpec((1, S, LANES), lambda b, hp, i: (b, 0, hp))
    table_spec = pl.BlockSpec((t, t), lambda b, hp, i: (0, 0))
    q_spec = pl.BlockSpec((subs * t, LANES), lambda b, hp, i: (b * steps + i, hp))
    return pl.pallas_call(
        functools.partial(_attn_kernel, t=t, subs=subs),
        grid=(B, D // LANES, steps),
        in_specs=[q_spec, kv_spec, kv_spec, pl.BlockSpec((near, near), lambda b, hp, i: (0, 0)),
                  table_spec, table_spec],
        out_specs=q_spec,
        out_shape=jax.ShapeDtypeStruct((T, D), BF16),
        scratch_shapes=[pltpu.VMEM((subs, 2, t, LANES), BF16), pltpu.VMEM((subs, t, LANES), F32),
                        pltpu.VMEM((subs, 2, t, 1), F32)],
        compiler_params=_params("parallel", "parallel", "arbitrary"),
        name="sb_attention",
    )(q, k3, v3, u, ceil, bias)


def _sb_mixer(x, norm, w_qkv, q_gain, k_gain):
    B, S, D = x.shape
    q, k, v = _qkv(x.reshape(B * S, D), norm, w_qkv, q_gain, k_gain)
    return _attention(q, k, v, B, S)


def kernel(x, pool_norm, pool_w, pool_b, pool_scale, sb_norm, sb_w_qkv, sb_q_gain, sb_k_gain, sb_w_o, moe_norm, moe_w_group, moe_b_group, moe_w_router, moe_b_router, moe_w_gate, moe_w_up, moe_w_down):
    stacked = lambda w: w.astype(BF16).reshape((DEPTH * N_EXPERTS,) + w.shape[2:])
    w_gate, w_up, w_down = stacked(moe_w_gate), stacked(moe_w_up), stacked(moe_w_down)
    for i in range(DEPTH):
        j = i // 2
        attn = None
        if i % 2 == 0:
            x = _pool_layer(x, pool_norm[j], pool_w[j], pool_b[j], pool_scale[j])
        else:
            attn = (_sb_mixer(x, sb_norm[j], sb_w_qkv[j], sb_q_gain[j], sb_k_gain[j]), sb_w_o[j])
        x = _moe_layer(x, moe_norm[i], moe_w_group[i], moe_b_group[i], moe_w_router[i], moe_b_router[i],
                       w_gate, w_up, w_down, layer=i, attn=attn)
    return x
```

```python
import functools

import jax
import jax.numpy as jnp
from jax import lax
from jax.experimental import pallas as pl
from jax.experimental.pallas import tpu as pltpu

F32 = jnp.float32
BF16 = jnp.bfloat16

D_MODEL = 1024
DEPTH = 4
POOL_WINDOWS = (2, 4, 8, 16)
POOL_GROUP_DIM = D_MODEL // len(POOL_WINDOWS)
POOL_HALO = 16
POOL_LEAD = 8
SB_HEADS = 16
SB_HEAD_DIM = D_MODEL // SB_HEADS
N_GROUPS = 4
EXPERTS_PER_GROUP = 8
N_EXPERTS = N_GROUPS * EXPERTS_PER_GROUP
D_EXPERT = D_MODEL // 4
RMS_EPS = 1e-6
LOG2_E = 1.4426950408889634
LN_2 = 0.6931471805599453
ZERO_WEIGHT_LOG2 = 150.0

LANES = 128
MXU_DEPTH = 256
DMA_PRIORITIES = 2
VMEM_LIMIT_BYTES = 56 * 1024 * 1024

POOL_TILE = 1024
ROUTER_TILE = 1024
INVERT_TILE = 2048
QKV_TILE = 1024
ATTN_TILE = 128
ATTN_NEAR_TILES = 2
ATTN_TILES_PER_STEP = 16


def _params(*sem):
    return pltpu.CompilerParams(dimension_semantics=sem, vmem_limit_bytes=VMEM_LIMIT_BYTES)


def _rms(v, gain):
    return v * lax.rsqrt(jnp.mean(v * v, axis=-1, keepdims=True) + RMS_EPS) * gain


def _dot(a, b):
    return jnp.dot(a, b, preferred_element_type=F32)


def _split_bf16(v):
    hi = v.astype(BF16)
    lo = (v - hi.astype(F32)).astype(BF16)
    return hi, lo


def _pool_kernel(xprev_ref, x_ref, norm_ref, w_ref, b_ref, scale_ref, o_ref, sums):
    i = pl.program_id(1)
    ts = x_ref.shape[1]
    x = x_ref[0]
    gain = norm_ref[...]
    h = _rms(x, gain)
    hprev = jnp.where(i > 0, _rms(xprev_ref[0], gain), 0.0)
    top, lo = POOL_LEAD, POOL_LEAD + POOL_HALO
    n = lo + ts
    sums[:, 0:top, :] = jnp.zeros((sums.shape[0], top, sums.shape[2]), F32)
    sums[0, top:lo, :] = hprev
    sums[0, lo:n, :] = h
    levels = len(POOL_WINDOWS)
    for k in range(1, levels):
        shift, c0 = 2 ** (k - 1), (k - 1) * POOL_GROUP_DIM
        sums[k, top:n, c0:] = sums[k - 1, top:n, c0:] + sums[k - 1, top - shift:n - shift, c0:]
    pos = i * ts + lax.broadcasted_iota(jnp.int32, (ts, 1), 0)
    for g, win in enumerate(POOL_WINDOWS):
        c0 = g * POOL_GROUP_DIM
        cols = slice(c0, c0 + POOL_GROUP_DIM)
        hg = h[:, cols]
        if g + 1 < levels:
            acc = sums[g + 1, lo:n, cols]
        else:
            half = win // 2
            acc = sums[g, lo:n, cols] + sums[g, lo - half:n - half, cols]
        count = jnp.minimum(pos + 1, win).astype(F32)
        diff = acc / count - hg
        y = _dot(diff.astype(BF16), w_ref[g]) + b_ref[g]
        o_ref[0, :, cols] = x[:, cols] + y * scale_ref[:, cols]


def _pool_layer(x, norm, w, b, scale):
    B, S, D = x.shape
    ts = min(POOL_TILE, S)
    halo_blocks = ts // POOL_HALO
    return pl.pallas_call(
        _pool_kernel,
        grid=(B, S // ts),
        in_specs=[
            pl.BlockSpec((1, POOL_HALO, D), lambda bi, i: (bi, jnp.maximum(i * halo_blocks - 1, 0), 0)),
            pl.BlockSpec((1, ts, D), lambda bi, i: (bi, i, 0)),
            pl.BlockSpec((1, D), lambda bi, i: (0, 0)),
            pl.BlockSpec((len(POOL_WINDOWS), POOL_GROUP_DIM, POOL_GROUP_DIM), lambda bi, i: (0, 0, 0)),
            pl.BlockSpec((len(POOL_WINDOWS), 1, POOL_GROUP_DIM), lambda bi, i: (0, 0, 0)),
            pl.BlockSpec((1, D), lambda bi, i: (0, 0)),
        ],
        out_specs=pl.BlockSpec((1, ts, D), lambda bi, i: (bi, i, 0)),
        out_shape=jax.ShapeDtypeStruct((B, S, D), F32),
        scratch_shapes=[pltpu.VMEM((len(POOL_WINDOWS), POOL_LEAD + POOL_HALO + ts, D), F32)],
        compiler_params=_params("parallel", "parallel"),
        name="pool_layer",
    )(x, x, norm.reshape(1, D), w.astype(BF16), b.reshape(len(POOL_WINDOWS), 1, POOL_GROUP_DIM),
      scale.reshape(1, D))


ROUTER_GROUP_ROW0 = N_EXPERTS
ROUTER_ROWS = LANES
ROUTER_USED_ROWS = 40
ROUTER_COUNT_ROWS = 16
ROW_WIDTH = D_MODEL + LANES
MOE_ROW_TILE = 512
MOE_PAD_ROWS = N_GROUPS * MOE_ROW_TILE


def _router_kernel(*refs, mixer_out):
    if mixer_out:
        o_ref, wo_ref, *refs = refs
    x_ref, norm_ref, whi_ref, wlo_ref, bias_ref, tri_ref, row_ref, where_ref, count_ref, run_ref = refs

    @pl.when(pl.program_id(0) == 0)
    def _():
        run_ref[...] = jnp.zeros_like(run_ref)

    x = x_ref[...]
    if mixer_out:
        x = x + _dot(o_ref[...], wo_ref[...])
    h = _rms(x, norm_ref[...])
    h_hi, h_lo = _split_bf16(h)
    w_hi = whi_ref[...]

    def per_token(w, t):
        return sum(lax.dot_general(w[:, c:c + MXU_DEPTH], t[:, c:c + MXU_DEPTH], (((1,), (1,)), ((), ())),
                                   preferred_element_type=F32) for c in range(0, w.shape[1], MXU_DEPTH))

    all_logits = per_token(w_hi, h_hi) + per_token(w_hi, h_lo) + per_token(wlo_ref[...], h_hi) + bias_ref[...]
    tokens = all_logits.shape[1]
    shape = (ROUTER_USED_ROWS, LANES)
    row = lax.broadcasted_iota(jnp.int32, shape, 0).astype(F32)
    all_rows = lax.broadcasted_iota(jnp.int32, (ROUTER_ROWS, LANES), 0).astype(F32)
    few = row[:ROUTER_COUNT_ROWS]
    neg_inf = F32(-jnp.inf)
    no_row = F32(4 * ROUTER_ROWS)

    def first_argmax(v):
        m = jnp.max(v, axis=0, keepdims=True)
        idx = jnp.min(jnp.where(v == m, row, no_row), axis=0, keepdims=True)
        return m, idx

    gates, groups, members = [], [], []
    for c in range(0, tokens, LANES):
        logits = all_logits[:ROUTER_USED_ROWS, c:c + LANES]
        is_group = (row >= ROUTER_GROUP_ROW0) & (row < ROUTER_GROUP_ROW0 + N_GROUPS)
        gl = jnp.where(is_group, logits, neg_inf)
        gmax, grow = first_argmax(gl)
        g_prob = 1.0 / jnp.sum(jnp.exp(gl - gmax), axis=0, keepdims=True)
        group = grow - ROUTER_GROUP_ROW0
        e0 = group * EXPERTS_PER_GROUP
        el = jnp.where((row >= e0) & (row < e0 + EXPERTS_PER_GROUP), logits, neg_inf)
        m1, i1 = first_argmax(el)
        m2, i2 = first_argmax(jnp.where(row == i1, neg_inf, el))
        r = jnp.exp(m2 - m1)
        w1 = 1.0 / (1.0 + r)
        gates.append(jnp.where(all_rows == i1, g_prob * w1, 0.0) + jnp.where(all_rows == i2, g_prob * (r * w1), 0.0))
        groups.append(group)
        members.append(jnp.where(few == group, 1.0, 0.0))

    in_group = jnp.concatenate(members, axis=1).astype(BF16)
    run = run_ref[...]
    earlier = _dot(in_group, tri_ref[...]) + run
    run = run + _dot(in_group, jnp.ones(tri_ref.shape, BF16))
    run_ref[...] = run
    count_ref[...] = run[:, :LANES]

    row_ref[:, :D_MODEL] = x
    for j, c in enumerate(range(0, tokens, LANES)):
        rank = jnp.sum(members[j] * earlier[:, c:c + LANES], axis=0, keepdims=True)
        row_ref[c:c + LANES, D_MODEL:] = gates[j].T
        where_ref[0, :, c:c + LANES] = jnp.where(few[:8] == 0.0, groups[j],
                                                 jnp.where(few[:8] == 1.0, rank, 0.0)).astype(jnp.int32)


def _router(x2, norm, w_group, b_group, w_router, b_router, attn=None):
    T, D = x2.shape
    tm = min(ROUTER_TILE, T)
    groups = slice(ROUTER_GROUP_ROW0, ROUTER_GROUP_ROW0 + N_GROUPS)
    w = jnp.zeros((ROUTER_ROWS, D), F32).at[:N_EXPERTS].set(w_router.T).at[groups].set(w_group.T)
    w_hi = w.astype(BF16)
    w_lo = (w - w_hi.astype(F32)).astype(BF16)
    bias = jnp.zeros((ROUTER_ROWS,), F32).at[:N_EXPERTS].set(b_router).at[groups].set(b_group)
    bias = jnp.broadcast_to(bias[:, None], (ROUTER_ROWS, tm))
    idx = jnp.arange(tm)
    tri = (idx[:, None] < idx[None, :]).astype(BF16)
    mixer_specs, mixer_args = [], []
    if attn is not None:
        mixer_specs = [pl.BlockSpec((tm, D), lambda i: (i, 0)), pl.BlockSpec((D, D), lambda i: (0, 0))]
        mixer_args = [attn[0], attn[1].astype(BF16)]
    return pl.pallas_call(
        functools.partial(_router_kernel, mixer_out=attn is not None),
        grid=(T // tm,),
        in_specs=mixer_specs + [
            pl.BlockSpec((tm, D), lambda i: (i, 0)),
            pl.BlockSpec((1, D), lambda i: (0, 0)),
            pl.BlockSpec((ROUTER_ROWS, D), lambda i: (0, 0)),
            pl.BlockSpec((ROUTER_ROWS, D), lambda i: (0, 0)),
            pl.BlockSpec((ROUTER_ROWS, tm), lambda i: (0, 0)),
            pl.BlockSpec((tm, tm), lambda i: (0, 0)),
        ],
        out_specs=[pl.BlockSpec((tm, ROW_WIDTH), lambda i: (i, 0)),
                   pl.BlockSpec((1, 8, tm), lambda i: (i, 0, 0)),
                   pl.BlockSpec((ROUTER_COUNT_ROWS, LANES), lambda i: (0, 0))],
        out_shape=[jax.ShapeDtypeStruct((T, ROW_WIDTH), F32),
                   jax.ShapeDtypeStruct((T // tm, 8, tm), jnp.int32),
                   jax.ShapeDtypeStruct((ROUTER_COUNT_ROWS, LANES), F32)],
        scratch_shapes=[pltpu.VMEM((ROUTER_COUNT_ROWS, tm), F32)],
        compiler_params=_params("arbitrary"),
        name="moe_router",
    )(*mixer_args, x2, norm.reshape(1, D), w_hi, w_lo, bias, tri)


def _experts_kernel(tile_group_ref, tile_rows_ref, tok_prev_ref, tok_ref, tok_next_ref, rows_ref, norm_ref,
                    wg_ref, wu_ref, wd_ref, out_ref, rbuf, obuf, in_sem, out_sem):
    r = pl.program_id(0)
    last = pl.num_programs(0) - 1
    slot = r % 2

    def gather(dst_slot, tok):
        for t in range(MOE_ROW_TILE):
            pltpu.make_async_copy(rows_ref.at[pl.ds(tok[t], 1)], rbuf.at[dst_slot, pl.ds(t, 1)],
                                  in_sem.at[dst_slot]).start(priority=t % DMA_PRIORITIES)

    def row_copy(src_slot, t, tok):
        return pltpu.make_async_copy(obuf.at[src_slot, pl.ds(t, 1)], out_ref.at[pl.ds(tok[t], 1)],
                                     out_sem.at[src_slot])

    def send_rows(n, src_slot, tok):
        def body(t, carry):
            row_copy(src_slot, t, tok).start()
            return carry
        lax.fori_loop(0, n, body, 0)

    def wait_rows(n, src_slot):
        @pl.when(n == MOE_ROW_TILE)
        def _():
            pltpu.make_async_copy(obuf.at[src_slot], obuf.at[src_slot], out_sem.at[src_slot]).wait()

        @pl.when(n < MOE_ROW_TILE)
        def _():
            def body(t, carry):
                pltpu.make_async_copy(obuf.at[src_slot, pl.ds(0, 1)], obuf.at[src_slot, pl.ds(0, 1)],
                                      out_sem.at[src_slot]).wait()
                return carry
            lax.fori_loop(0, n, body, 0)

    rows_now = tile_rows_ref[r]
    rows_prev = jnp.where(r >= 1, tile_rows_ref[jnp.maximum(r - 1, 0)], 0)

    @pl.when((r == 0) & (rows_now > 0))
    def _():
        gather(slot, tok_ref)

    @pl.when(r >= 2)
    def _():
        wait_rows(tile_rows_ref[jnp.maximum(r - 2, 0)], slot)

    rows_next = jnp.where(r < last, tile_rows_ref[jnp.minimum(r + 1, last)], 0)

    @pl.when(rows_next > 0)
    def _():
        gather(1 - slot, tok_next_ref)

    @pl.when(rows_now > 0)
    def _():
        pltpu.make_async_copy(rbuf.at[slot], rbuf.at[slot], in_sem.at[slot]).wait()

    @pl.when(rows_prev == MOE_ROW_TILE)
    def _():
        for t in range(MOE_ROW_TILE):
            row_copy(1 - slot, t, tok_prev_ref).start(priority=t % DMA_PRIORITIES)

    @pl.when(rows_prev < MOE_ROW_TILE)
    def _():
        send_rows(rows_prev, 1 - slot, tok_prev_ref)

    @pl.when(rows_now > 0)
    def _():
        x = rbuf[slot, :, :D_MODEL]
        gate = rbuf[slot, :, D_MODEL:]
        h = _rms(x, norm_ref[...]).astype(BF16)
        lane = lax.broadcasted_iota(jnp.int32, gate.shape, 1)
        e0 = tile_group_ref[r] * EXPERTS_PER_GROUP
        acc = x
        for e in range(EXPERTS_PER_GROUP):
            a = _dot(h, wg_ref[e])
            u = _dot(h, wu_ref[e])
            gcol = jnp.sum(jnp.where(lane == e0 + e, gate, 0.0), axis=1, keepdims=True)
            hid = (a * jax.nn.sigmoid(a)) * u * gcol
            acc = acc + _dot(hid.astype(BF16), wd_ref[e])
        obuf[slot] = acc

    @pl.when(r == last)
    def _():
        send_rows(rows_now, slot, tok_ref)
        wait_rows(rows_prev, 1 - slot)
        wait_rows(rows_now, slot)


def _experts(rows, tile_group, tile_rows, tok, norm, w_gate, w_up, w_down, layer):
    T = rows.shape[0]
    D = D_MODEL
    n_tiles = tok.shape[0] // MOE_ROW_TILE
    group_w = lambda r, tg, tr: (layer * N_GROUPS + tg[r], 0, 0)
    smem_tile = lambda index: pl.BlockSpec((MOE_ROW_TILE,), index, memory_space=pltpu.SMEM)
    return pl.pallas_call(
        _experts_kernel,
        grid_spec=pltpu.PrefetchScalarGridSpec(
            num_scalar_prefetch=2,
            grid=(n_tiles,),
            in_specs=[smem_tile(lambda r, tg, tr: (jnp.maximum(r - 1, 0),)),
                      smem_tile(lambda r, tg, tr: (r,)),
                      smem_tile(lambda r, tg, tr: (jnp.minimum(r + 1, n_tiles - 1),)),
                      pl.BlockSpec(memory_space=pl.ANY),
                      pl.BlockSpec((1, D), lambda r, tg, tr: (0, 0)),
                      pl.BlockSpec((EXPERTS_PER_GROUP, D, D_EXPERT), group_w),
                      pl.BlockSpec((EXPERTS_PER_GROUP, D, D_EXPERT), group_w),
                      pl.BlockSpec((EXPERTS_PER_GROUP, D_EXPERT, D), group_w)],
            out_specs=pl.BlockSpec(memory_space=pl.ANY),
            scratch_shapes=[pltpu.VMEM((2, MOE_ROW_TILE, ROW_WIDTH), F32), pltpu.VMEM((2, MOE_ROW_TILE, D), F32),
                            pltpu.SemaphoreType.DMA((2,)), pltpu.SemaphoreType.DMA((2,))],
        ),
        out_shape=jax.ShapeDtypeStruct((T, D), F32),
        compiler_params=_params("arbitrary"),
        name="moe_experts",
    )(tile_group, tile_rows, tok, tok, tok, rows, norm.reshape(1, D), w_gate.astype(BF16), w_up.astype(BF16),
      w_down.astype(BF16))


def _invert_kernel(pad_lo_ref, pad_hi_ref, dest_ref, tok_ref):
    i = pl.program_id(0)
    tm = dest_ref.shape[0]

    @pl.when(i == 0)
    def _():
        for seg in range(N_GROUPS + 1):
            def clear(p, carry):
                tok_ref[p] = 0
                return carry
            lax.fori_loop(pad_lo_ref[seg], pad_hi_ref[seg], clear, 0)

    for t in range(tm):
        tok_ref[dest_ref[t]] = i * tm + t


def _invert(dest, pad_lo, pad_hi, total_rows):
    T = dest.shape[0]
    tm = min(INVERT_TILE, T)
    return pl.pallas_call(
        _invert_kernel,
        grid_spec=pltpu.PrefetchScalarGridSpec(
            num_scalar_prefetch=2,
            grid=(T // tm,),
            in_specs=[pl.BlockSpec((tm,), lambda i, lo, hi: (i,), memory_space=pltpu.SMEM)],
            out_specs=pl.BlockSpec(memory_space=pltpu.SMEM),
        ),
        out_shape=jax.ShapeDtypeStruct((total_rows,), jnp.int32),
        compiler_params=_params("arbitrary"),
        name="moe_invert",
    )(pad_lo, pad_hi, dest)


def _moe_layer(x, norm, w_group, b_group, w_router, b_router, w_gate, w_up, w_down, layer=0, attn=None):
    B, S, D = x.shape
    T = B * S
    rows, where, counts = _router(x.reshape(T, D), norm, w_group, b_group, w_router, b_router, attn)
    where = where.transpose(1, 0, 2).reshape(8, T)
    group, rank = where[0], where[1]
    counts = counts[:N_GROUPS, 0].astype(jnp.int32)
    tiles = (counts + MOE_ROW_TILE - 1) // MOE_ROW_TILE
    tile_end = jnp.cumsum(tiles)
    tile_first = tile_end - tiles
    dest = tile_first[group] * MOE_ROW_TILE + rank
    total_rows = T + MOE_PAD_ROWS
    pad_lo = jnp.concatenate([tile_first * MOE_ROW_TILE + counts, tile_end[-1:] * MOE_ROW_TILE]).astype(jnp.int32)
    pad_hi = jnp.concatenate([tile_end * MOE_ROW_TILE, jnp.full((1,), total_rows)]).astype(jnp.int32)
    tok = _invert(dest.astype(jnp.int32), pad_lo, pad_hi, total_rows)
    tile_idx = jnp.arange(total_rows // MOE_ROW_TILE)
    tile_group = jnp.minimum(jnp.sum(tile_idx[:, None] >= tile_end[None, :], axis=1), N_GROUPS - 1)
    tile_rows = jnp.clip(counts[tile_group] - (tile_idx - tile_first[tile_group]) * MOE_ROW_TILE, 0, MOE_ROW_TILE)
    tile_rows = jnp.where(tile_idx < tile_end[-1], tile_rows, 0)
    return _experts(rows, tile_group.astype(jnp.int32), tile_rows.astype(jnp.int32), tok, norm,
                    w_gate, w_up, w_down, layer).reshape(B, S, D)


def _qkv_kernel(x_ref, norm_ref, w_ref, seg_ref, segt_ref, qg_ref, kg_ref, q_ref, k_ref, v_ref):
    D = x_ref.shape[1]
    h = _rms(x_ref[...], norm_ref[...]).astype(BF16)
    qkv = _dot(h, w_ref[...])
    seg = seg_ref[...]
    segt = segt_ref[...]

    def head_norm(t, gain):
        ms = _dot((t * t).astype(BF16), seg) * (1.0 / SB_HEAD_DIM)
        r_hi, r_lo = _split_bf16(lax.rsqrt(ms + RMS_EPS))
        return t * (_dot(r_hi, segt) + _dot(r_lo, segt)) * gain

    q_ref[...] = head_norm(qkv[:, :D], qg_ref[...]).astype(BF16)
    k_ref[...] = head_norm(qkv[:, D:2 * D], kg_ref[...]).astype(BF16)
    v_ref[...] = qkv[:, 2 * D:].astype(BF16)


def _qkv(x2, norm, w_qkv, q_gain, k_gain):
    T, D = x2.shape
    tm = min(QKV_TILE, T)
    head_of = jnp.arange(D) // SB_HEAD_DIM
    seg = (head_of[:, None] == jnp.arange(LANES)[None, :]).astype(BF16)
    segt = seg.T
    qg = (jnp.tile(q_gain, SB_HEADS) * (SB_HEAD_DIM ** -0.5 * LOG2_E)).reshape(1, D)
    kg = jnp.tile(k_gain, SB_HEADS).reshape(1, D)
    row = pl.BlockSpec((tm, D), lambda i: (i, 0))
    vec = pl.BlockSpec((1, D), lambda i: (0, 0))
    return pl.pallas_call(
        _qkv_kernel,
        grid=(T // tm,),
        in_specs=[row, vec, pl.BlockSpec((D, 3 * D), lambda i: (0, 0)),
                  pl.BlockSpec((D, LANES), lambda i: (0, 0)), pl.BlockSpec((LANES, D), lambda i: (0, 0)),
                  vec, vec],
        out_specs=[row, row, row],
        out_shape=[jax.ShapeDtypeStruct((T, D), BF16)] * 3,
        compiler_params=_params("parallel"),
        name="sb_qkv",
    )(x2, norm.reshape(1, D), w_qkv.astype(BF16), seg, segt, qg, kg)


def _attn_kernel(q_ref, k_ref, v_ref, u_ref, ceil_ref, bias_ref, o_ref, qh_buf, acc_ref, rem_ref, *, t, subs):
    step = pl.program_id(2)
    near = ATTN_NEAR_TILES * t
    lane = lax.broadcasted_iota(jnp.int32, (t, LANES), 1)
    first_head = lane < SB_HEAD_DIM
    for s in range(subs):
        q2 = q_ref[s * t:(s + 1) * t, :]
        zero = jnp.zeros_like(q2)
        qh_buf[s, 0] = jnp.where(first_head, q2, zero)
        qh_buf[s, 1] = jnp.where(first_head, zero, q2)

    def keys(start, n):
        return k_ref[0, pl.ds(start, n), :], v_ref[0, pl.ds(start, n), :]

    def scores(s, hd, kt):
        return lax.dot_general(qh_buf[s, hd], kt, (((1,), (1,)), ((), ())), preferred_element_type=F32)

    def drop_of(z, ceil=None):
        drop = jnp.maximum(z, 0.0) + jnp.log2(1.0 + jnp.exp(jnp.abs(z) * (-LN_2)))
        log_beta = z - drop
        if ceil is not None:
            drop = jnp.minimum(drop, ceil)
        return drop, log_beta

    def suffix_sums(drop):
        n = drop.shape[1]
        return _dot(drop.astype(BF16), u_ref[:n, :n])

    def weights(log_beta, later, bias=None):
        arg = log_beta - later
        if bias is not None:
            arg = arg + bias
        return jnp.exp2(arg).astype(BF16)

    def diagonal(s, hd, kd):
        drop, log_beta = drop_of(scores(s, hd, kd), ceil_ref[...])
        c = suffix_sums(drop)
        return weights(log_beta, c, bias_ref[...]), c[:, 0:1] + drop[:, 0:1]

    def sweep(s, qi, first):
        def rem_min():
            return jnp.min(jnp.minimum(rem_ref[s, 0], rem_ref[s, 1]))

        def more(carry):
            j, smallest_rem = carry
            return (j <= qi) & (smallest_rem < ZERO_WEIGHT_LOG2)

        def body(carry):
            j, _ = carry
            kt, vt = keys(pl.multiple_of((qi - j) * t, t), t)
            pv = []
            for hd in range(2):
                drop, log_beta = drop_of(scores(s, hd, kt))
                c = suffix_sums(drop)
                rem = rem_ref[s, hd]
                pv.append(_dot(weights(log_beta, c + rem), vt))
                rem_ref[s, hd] = rem + (c[:, 0:1] + drop[:, 0:1])
            acc_ref[s] += jnp.where(first_head, pv[0], pv[1])
            return j + 1, rem_min()

        lax.while_loop(more, body, (jnp.int32(first), rem_min()))

    def near_block(subset):
        tiles = {s: step * subs + s for s in subset}
        units = [(s, hd) for s in subset for hd in range(2)]
        diag = {s: keys(pl.multiple_of(qi * t, t), t) for s, qi in tiles.items()}
        left = {s: keys(pl.multiple_of((qi - ATTN_NEAR_TILES) * t, t), near) for s, qi in tiles.items()}
        values = {s: jnp.concatenate([diag[s][1], left[s][1]], axis=0) for s in subset}
        z = {(s, hd): (scores(s, hd, diag[s][0]), scores(s, hd, left[s][0])) for s, hd in units}
        dl = {un: (drop_of(z[un][0], ceil_ref[...]), drop_of(z[un][1])) for un in units}
        c = {un: (suffix_sums(dl[un][0][0]), suffix_sums(dl[un][1][0])) for un in units}
        pv = {}
        for s, hd in units:
            (d0, lb0), (d1, lb1) = dl[s, hd]
            c0, c1 = c[s, hd]
            rem1 = c0[:, 0:1] + d0[:, 0:1]
            a0 = weights(lb0, c0, bias_ref[...])
            a1 = weights(lb1, c1 + rem1)
            pv[s, hd] = _dot(jnp.concatenate([a0, a1], axis=1), values[s])
            rem_ref[s, hd] = rem1 + (c1[:, 0:1] + d1[:, 0:1])
        smallest_rem = None
        for s in subset:
            acc_ref[s] = jnp.where(first_head, pv[s, 0], pv[s, 1])
            for hd in range(2):
                rem = rem_ref[s, hd]
                smallest_rem = rem if smallest_rem is None else jnp.minimum(smallest_rem, rem)

        @pl.when(jnp.min(smallest_rem) < ZERO_WEIGHT_LOG2)
        def _():
            for s, qi in tiles.items():
                sweep(s, qi, ATTN_NEAR_TILES + 1)

    @pl.when(step > 0)
    def _():
        near_block(range(subs))

    @pl.when(step == 0)
    def _():
        for s in range(ATTN_NEAR_TILES):
            kd, vd = keys(s * t, t)
            pv = []
            for hd in range(2):
                a, rem1 = diagonal(s, hd, kd)
                pv.append(_dot(a, vd))
                rem_ref[s, hd] = rem1
            acc_ref[s] = jnp.where(first_head, pv[0], pv[1])
            sweep(s, s, 1)
        near_block(range(ATTN_NEAR_TILES, subs))

    for s in range(subs):
        o_ref[s * t:(s + 1) * t, :] = acc_ref[s].astype(o_ref.dtype)


def _attention(q, k, v, B, S):
    T, D = q.shape
    t = min(ATTN_TILE, S)
    subs = ATTN_TILES_PER_STEP
    assert subs >= ATTN_NEAR_TILES and S % (t * subs) == 0
    steps = S // (t * subs)
    near = ATTN_NEAR_TILES * t
    idx = jnp.arange(near)
    u = (idx[:, None] > idx[None, :]).astype(BF16)
    causal = idx[None, :t] < idx[:t, None]
    ceil = jnp.where(causal, jnp.inf, 0.0).astype(F32)
    bias = jnp.where(causal, 0.0, -jnp.inf).astype(F32)
    k3 = k.reshape(B, S, D)
    v3 = v.reshape(B, S, D)
    kv_spec = pl.BlockSpec((1, S, LANES), lambda b, hp, i: (b, 0, hp))
    table_spec = pl.BlockSpec((t, t), lambda b, hp, i: (0, 0))
    q_spec = pl.BlockSpec((subs * t, LANES), lambda b, hp, i: (b * steps + i, hp))
    return pl.pallas_call(
        functools.partial(_attn_kernel, t=t, subs=subs),
        grid=(B, D // LANES, steps),
        in_specs=[q_spec, kv_spec, kv_spec, pl.BlockSpec((near, near), lambda b, hp, i: (0, 0)),
                  table_spec, table_spec],
        out_specs=q_spec,
        out_shape=jax.ShapeDtypeStruct((T, D), BF16),
        scratch_shapes=[pltpu.VMEM((subs, 2, t, LANES), BF16), pltpu.VMEM((subs, t, LANES), F32),
                        pltpu.VMEM((subs, 2, t, 1), F32)],
        compiler_params=_params("parallel", "parallel", "arbitrary"),
        name="sb_attention",
    )(q, k3, v3, u, ceil, bias)


def _sb_mixer(x, norm, w_qkv, q_gain, k_gain):
    B, S, D = x.shape
    q, k, v = _qkv(x.reshape(B * S, D), norm, w_qkv, q_gain, k_gain)
    return _attention(q, k, v, B, S)


def kernel(x, pool_norm, pool_w, pool_b, pool_scale, sb_norm, sb_w_qkv, sb_q_gain, sb_k_gain, sb_w_o, moe_norm, moe_w_group, moe_b_group, moe_w_router, moe_b_router, moe_w_gate, moe_w_up, moe_w_down):
    stacked = lambda w: w.astype(BF16).reshape((DEPTH * N_EXPERTS,) + w.shape[2:])
    w_gate, w_up, w_down = stacked(moe_w_gate), stacked(moe_w_up), stacked(moe_w_down)
    for i in range(DEPTH):
        j = i // 2
        attn = None
        if i % 2 == 0:
            x = _pool_layer(x, pool_norm[j], pool_w[j], pool_b[j], pool_scale[j])
        else:
            attn = (_sb_mixer(x, sb_norm[j], sb_w_qkv[j], sb_q_gain[j], sb_k_gain[j]), sb_w_o[j])
        x = _moe_layer(x, moe_norm[i], moe_w_group[i], moe_b_group[i], moe_w_router[i], moe_b_router[i],
                       w_gate, w_up, w_down, layer=i, attn=attn)
    return x
```
